```python
import math
import jax
import jax.numpy as jnp
from jax import lax
import numpy as np


D_MODEL = 1024
BATCH = 8
SEQ = 8192
DEPTH = 4

GRID_W = 64
CTX_LEN = 256
HEAD_DIM = 64
N_Q_HEADS = 8
N_KV_HEADS = 2
GQA_GROUP = N_Q_HEADS // N_KV_HEADS
ROPE_THETA = 10000.0
Q_BLOCK = 128
DN_HEADS = 4
DN_HEAD_DIM = 64
CONV_K = 5
HG_HEADS = 4
HG_HEAD_DIM = 64
CHUNK = 64
N_EXPERTS = 16
D_EXPERT = 1408
CAPACITY_FACTOR = 2
MOD_INIT = 0.5
ATTN_Q_W = N_Q_HEADS * HEAD_DIM
ATTN_KV_W = N_KV_HEADS * HEAD_DIM
DN_W = DN_HEADS * DN_HEAD_DIM
HG_W = HG_HEADS * HG_HEAD_DIM
IN_COLS = ATTN_Q_W + 2 * ATTN_KV_W + 4 * DN_W + 4 * DN_HEADS + 5 * HG_W

kernel_name = 'hybrid_prefix_dit_trunk'


def _split_cols(p):
    names = ('attn_q', 'attn_k', 'attn_v', 'dn_qkv', 'dn_z', 'dn_beta', 'dn_a',
             'hg_q', 'hg_f', 'hg_i', 'hg_g')
    sizes = (ATTN_Q_W, ATTN_KV_W, ATTN_KV_W, 3 * DN_W, DN_W, 2 * DN_HEADS, 2 * DN_HEADS,
             HG_W, 2 * HG_W, HG_W, HG_W)
    out = {}
    off = 0
    for n, s in zip(names, sizes):
        out[n] = p[..., off:off + s]
        off += s
    return out


def _rms(x, eps=1e-6):
    xf = x.astype(jnp.float32)
    return (xf * lax.rsqrt(jnp.mean(xf * xf, axis=-1, keepdims=True) + eps)).astype(x.dtype)


def _l2norm(x, eps=1e-6):
    return x * lax.rsqrt(jnp.sum(x * x, axis=-1, keepdims=True) + eps)


def _modulate(x, shift, scale):
    return _rms(x) * (1 + scale) + shift


def _heads(t, n):
    b, s, w = t.shape
    return t.reshape(b, s, n, w // n).transpose(0, 2, 1, 3)


def _merge_heads(t):
    b, h, s, d = t.shape
    return t.transpose(0, 2, 1, 3).reshape(b, s, h * d)


def _rope_tables(n_tokens, dtype):
    rows = n_tokens // GRID_W
    row = jnp.repeat(jnp.arange(rows, dtype=jnp.float32), GRID_W)
    col = jnp.tile(jnp.arange(GRID_W, dtype=jnp.float32), rows)
    n_freq = HEAD_DIM // 4
    inv_freq = ROPE_THETA ** (-jnp.arange(n_freq, dtype=jnp.float32) / n_freq)
    ang_r = row[:, None] * inv_freq
    ang_c = col[:, None] * inv_freq
    return (jnp.cos(ang_r).astype(dtype), jnp.sin(ang_r).astype(dtype),
            jnp.cos(ang_c).astype(dtype), jnp.sin(ang_c).astype(dtype))


def _rope_axis(x, cos, sin):
    x1, x2 = jnp.split(x, 2, axis=-1)
    return jnp.concatenate([x1 * cos - x2 * sin, x1 * sin + x2 * cos], axis=-1)


def _rope2d(x, rope):
    cr, sr, cc, sc = rope
    xr, xc = jnp.split(x, 2, axis=-1)
    return jnp.concatenate([_rope_axis(xr, cr, sr), _rope_axis(xc, cc, sc)], axis=-1)


def _attend(q, k, v):
    s = jnp.einsum('bngqd,bntd->bngqt', q, k).astype(jnp.float32)
    p = jax.nn.softmax(s, axis=-1).astype(v.dtype)
    return jnp.einsum('bngqt,bntd->bngqd', p, v)


def _attention_group(cl, cc, q_norm, k_norm, rope, ctx_out):
    def q_heads(cols, rotate):
        q = _rms(_heads(cols['attn_q'], N_Q_HEADS)) * q_norm
        if rotate:
            q = _rope2d(q, rope)
        b, _, t, _ = q.shape
        return (q * HEAD_DIM ** -0.5).reshape(b, N_KV_HEADS, GQA_GROUP, t, HEAD_DIM)

    def kv_heads(cols, rotate):
        k = _rms(_heads(cols['attn_k'], N_KV_HEADS)) * k_norm
        if rotate:
            k = _rope2d(k, rope)
        return k, _heads(cols['attn_v'], N_KV_HEADS)

    ql = q_heads(cl, True)
    kl, vl = kv_heads(cl, True)
    kc, vc = kv_heads(cc, False)
    k_all = jnp.concatenate([kc, kl], axis=2)
    v_all = jnp.concatenate([vc, vl], axis=2)
    b, _, _, s, _ = ql.shape
    nb = s // Q_BLOCK
    qb = jnp.moveaxis(ql.reshape(b, N_KV_HEADS, GQA_GROUP, nb, Q_BLOCK, HEAD_DIM), 3, 0)
    ob = lax.map(lambda qq: _attend(qq, k_all, v_all), qb)
    o_lat = jnp.moveaxis(ob, 0, 3).reshape(b, N_Q_HEADS, s, HEAD_DIM)
    out_lat = _merge_heads(o_lat)
    out_ctx = None
    if ctx_out:
        qc = q_heads(cc, False)
        oc = _attend(qc, kc, vc).reshape(b, N_Q_HEADS, -1, HEAD_DIM)
        out_ctx = _merge_heads(oc)
    return out_lat, out_ctx


def _short_conv(t, w):
    k, ch = w.shape
    y = lax.conv_general_dilated(t, w.astype(t.dtype)[:, None, :], window_strides=(1,),
                                 padding=[(k // 2, k // 2)],
                                 dimension_numbers=('NWC', 'WIO', 'NWC'),
                                 feature_group_count=ch)
    return jax.nn.silu(y)


def _to_chunks(t):
    b, h, s = t.shape[:3]
    t = t.reshape((b, h, s // CHUNK, CHUNK) + t.shape[3:])
    return jnp.moveaxis(t, 2, 0)


def _chunk_scan(step, xs, s0):
    s_final, o = lax.scan(step, s0, tuple(_to_chunks(t) for t in xs))
    n, b, h, c, d = o.shape
    return jnp.moveaxis(o, 0, 2).reshape(b, h, n * c, d), s_final


def _flip_time(xs):
    return tuple(jnp.flip(t, axis=2) for t in xs)


def _bidirectional(step, ctx_f, lat_f, ctx_b, lat_b, s0):
    o_cf, s_f = _chunk_scan(step, ctx_f, s0)
    o_lf, _ = _chunk_scan(step, lat_f, s_f)
    o_cb, s_b = _chunk_scan(step, _flip_time(ctx_b), s0)
    o_lb, _ = _chunk_scan(step, _flip_time(lat_b), s_b)
    return o_lf + jnp.flip(o_lb, axis=2), o_cf + jnp.flip(o_cb, axis=2)


def _gated_delta_chunk(state, xs):
    q, k, v, g, beta = xs
    c = q.shape[2]
    cum = jnp.cumsum(g, axis=-1)
    causal = jnp.tril(jnp.ones((c, c), dtype=bool))
    strict = jnp.tril(jnp.ones((c, c), dtype=bool), -1)
    decay = jnp.exp(jnp.where(causal, cum[..., :, None] - cum[..., None, :], -jnp.inf))
    kb = k * beta[..., None]
    m = jnp.where(strict, jnp.einsum('bhtd,bhsd->bhts', kb, k) * decay, 0.0)
    rhs = jnp.concatenate([v * beta[..., None], kb * jnp.exp(cum)[..., None]], axis=-1)
    sol = lax.linalg.triangular_solve(m, rhs, left_side=True, lower=True, unit_diagonal=True)
    dv = v.shape[-1]
    u, w = sol[..., :dv], sol[..., dv:]
    v_new = u - jnp.einsum('bhtk,bhkv->bhtv', w, state)
    attn = jnp.einsum('bhtk,bhsk->bhts', q, k) * decay
    out = (jnp.einsum('bhtk,bhkv->bhtv', q * jnp.exp(cum)[..., None], state)
           + jnp.einsum('bhts,bhsv->bhtv', attn, v_new))
    c_last = cum[..., -1]
    new_state = (state * jnp.exp(c_last)[..., None, None]
                 + jnp.einsum('bhsk,bhsv->bhkv', k * jnp.exp(c_last[..., None] - cum)[..., None], v_new))
    return new_state, out


def _hgrn2_chunk(state, xs):
    q, k, v, g = xs
    c = q.shape[2]
    cum = jnp.cumsum(g, axis=2)
    causal = jnp.tril(jnp.ones((c, c), dtype=bool))[:, :, None]
    decay = jnp.exp(jnp.where(causal, cum[:, :, :, None, :] - cum[:, :, None, :, :], -jnp.inf))
    attn = jnp.einsum('bhtk,bhsk,bhtsk->bhts', q, k, decay)
    out = (jnp.einsum('bhtk,bhkv->bhtv', q * jnp.exp(cum), state)
           + jnp.einsum('bhts,bhsv->bhtv', attn, v))
    c_last = cum[:, :, -1]
    new_state = (state * jnp.exp(c_last)[..., None]
                 + jnp.einsum('bhsk,bhsv->bhkv', k * jnp.exp(c_last[:, :, None] - cum), v))
    return new_state, out


def _gated_out(o, z, norm_w, n_heads):
    y = _rms(o) * norm_w.astype(jnp.float32) * jax.nn.silu(_heads(z, n_heads).astype(jnp.float32))
    return _merge_heads(y).astype(z.dtype)


def _deltanet_group(cl, cc, conv_w, a_log, dt_bias, norm_w, ctx_out):
    f32 = jnp.float32

    def prep(cols):
        qkv = _short_conv(cols['dn_qkv'], conv_w)
        b, t, _ = qkv.shape
        q, k, v = jnp.split(qkv.astype(f32), 3, axis=-1)
        q = _l2norm(_heads(q, DN_HEADS)) * DN_HEAD_DIM ** -0.5
        k = _l2norm(_heads(k, DN_HEADS))
        v = _heads(v, DN_HEADS)
        beta = jax.nn.sigmoid(cols['dn_beta'].astype(f32)).reshape(b, t, 2, DN_HEADS)
        a = cols['dn_a'].astype(f32).reshape(b, t, 2, DN_HEADS)
        g = -jnp.exp(a_log.astype(f32)) * jax.nn.softplus(a + dt_bias.astype(f32))
        beta = beta.transpose(2, 0, 3, 1)
        g = g.transpose(2, 0, 3, 1)
        return (q, k, v, g[0], beta[0]), (q, k, v, g[1], beta[1])

    lat_f, lat_b = prep(cl)
    ctx_f, ctx_b = prep(cc)
    b = cl['dn_qkv'].shape[0]
    s0 = jnp.zeros((b, DN_HEADS, DN_HEAD_DIM, DN_HEAD_DIM), f32)
    o_lat, o_ctx = _bidirectional(_gated_delta_chunk, ctx_f, lat_f, ctx_b, lat_b, s0)
    y_lat = _gated_out(o_lat, cl['dn_z'], norm_w, DN_HEADS)
    y_ctx = _gated_out(o_ctx, cc['dn_z'], norm_w, DN_HEADS) if ctx_out else None
    return y_lat, y_ctx


def _hgrn2_group(cl, cc, lb, norm_w, ctx_out):
    f32 = jnp.float32

    def prep(cols):
        b, t, _ = cols['hg_q'].shape
        q = _heads(jax.nn.silu(cols['hg_q'].astype(f32)), HG_HEADS)
        i = _heads(cols['hg_i'].astype(f32), HG_HEADS)
        f = lb + (1.0 - lb) * jax.nn.sigmoid(cols['hg_f'].astype(f32).reshape(b, t, 2, HG_W))
        f = f.reshape(b, t, 2, HG_HEADS, HG_HEAD_DIM).transpose(2, 0, 3, 1, 4)
        k = 1.0 - f
        g = jnp.log(f)
        return (q, k[0], i, g[0]), (q, k[1], i, g[1])

    lat_f, lat_b = prep(cl)
    ctx_f, ctx_b = prep(cc)
    b = cl['hg_q'].shape[0]
    s0 = jnp.zeros((b, HG_HEADS, HG_HEAD_DIM, HG_HEAD_DIM), f32)
    o_lat, o_ctx = _bidirectional(_hgrn2_chunk, ctx_f, lat_f, ctx_b, lat_b, s0)
    y_lat = _gated_out(o_lat, cl['hg_g'], norm_w, HG_HEADS)
    y_ctx = _gated_out(o_ctx, cc['hg_g'], norm_w, HG_HEADS) if ctx_out else None
    return y_lat, y_ctx


def _token_mixing(h_lat, h_ctx, w_in, w_out, q_norm, k_norm, dn_conv, dn_a_log, dn_dt_bias,
                  dn_norm, hg_lb, hg_norm, rope, ctx_out):
    cl = _split_cols(h_lat @ w_in)
    cc = _split_cols(h_ctx @ w_in)
    a_lat, a_ctx = _attention_group(cl, cc, q_norm, k_norm, rope, ctx_out)
    d_lat, d_ctx = _deltanet_group(cl, cc, dn_conv, dn_a_log, dn_dt_bias, dn_norm, ctx_out)
    g_lat, g_ctx = _hgrn2_group(cl, cc, hg_lb, hg_norm, ctx_out)
    y_lat = jnp.concatenate([a_lat, d_lat, g_lat], axis=-1) @ w_out
    y_ctx = None
    if ctx_out:
        y_ctx = jnp.concatenate([a_ctx, d_ctx, g_ctx], axis=-1) @ w_out
    return y_lat, y_ctx


def _expert_choice_ffn(h, router, w_gate, w_up, w_down):
    b, t, _ = h.shape
    cap = CAPACITY_FACTOR * t // N_EXPERTS
    aff = jax.nn.softmax(jnp.einsum('btd,de->bte', h, router).astype(jnp.float32), axis=-1)
    top_w, top_i = lax.top_k(jnp.swapaxes(aff, 1, 2), cap)
    bidx = jnp.arange(b)[:, None, None]
    xe = h[bidx, top_i]
    a = jnp.einsum('becd,edf->becf', xe, w_gate)
    u = jnp.einsum('becd,edf->becf', xe, w_up)
    y = jnp.einsum('becf,efd->becd', jax.nn.silu(a) * u, w_down) * top_w[..., None].astype(h.dtype)
    return jnp.zeros_like(h).at[bidx, top_i].add(y)


def setup_inputs(seed: int = 0) -> dict:
    key = jax.random.key(seed)
    ks = jax.random.split(key, 20)
    f32 = jnp.float32

    def nrm(k, shape, scale):
        return jax.random.normal(k, shape, f32) * scale

    dt = jnp.exp(jax.random.uniform(ks[12], (DEPTH, 2, DN_HEADS), f32, math.log(1e-3), math.log(1e-1)))
    return {
        'x': nrm(ks[0], (BATCH, SEQ, D_MODEL), 1.0),
        'c': nrm(ks[1], (BATCH, D_MODEL), 1.0),
        'ctx': nrm(ks[2], (BATCH, CTX_LEN, D_MODEL), 1.0),
        'c_ctx': nrm(ks[3], (D_MODEL,), 1.0),
        'w_mod': nrm(ks[4], (DEPTH, D_MODEL, 6 * D_MODEL), MOD_INIT * D_MODEL ** -0.5),
        'b_mod': nrm(ks[5], (DEPTH, 6 * D_MODEL), 0.01),
        'w_in': nrm(ks[6], (DEPTH, D_MODEL, IN_COLS), D_MODEL ** -0.5),
        'w_out': nrm(ks[7], (DEPTH, D_MODEL, D_MODEL), D_MODEL ** -0.5),
        'attn_q_norm': 1.0 + nrm(ks[8], (DEPTH, HEAD_DIM), 0.1),
        'attn_k_norm': 1.0 + nrm(ks[9], (DEPTH, HEAD_DIM), 0.1),
        'dn_conv': nrm(ks[10], (DEPTH, CONV_K, 3 * DN_W), CONV_K ** -0.5),
        'dn_a_log': jnp.log(jax.random.uniform(ks[11], (DEPTH, 2, DN_HEADS), f32, 1.0, 16.0)),
        'dn_dt_bias': dt + jnp.log(-jnp.expm1(-dt)),
        'dn_norm': 1.0 + nrm(ks[13], (DEPTH, DN_HEAD_DIM), 0.1),
        'hg_lower_bounds': nrm(ks[14], (DEPTH, 2, HG_W), 0.1),
        'hg_norm': 1.0 + nrm(ks[15], (DEPTH, HG_HEAD_DIM), 0.1),
        'moe_router': nrm(ks[16], (DEPTH, D_MODEL, N_EXPERTS), D_MODEL ** -0.5),
        'moe_w_gate': nrm(ks[17], (DEPTH, N_EXPERTS, D_MODEL, D_EXPERT), D_MODEL ** -0.5),
        'moe_w_up': nrm(ks[18], (DEPTH, N_EXPERTS, D_MODEL, D_EXPERT), D_MODEL ** -0.5),
        'moe_w_down': nrm(ks[19], (DEPTH, N_EXPERTS, D_EXPERT, D_MODEL), D_EXPERT ** -0.5),
    }


def reference(x, c, ctx, c_ctx, w_mod, b_mod, w_in, w_out, attn_q_norm, attn_k_norm, dn_conv,
              dn_a_log, dn_dt_bias, dn_norm, hg_lower_bounds, hg_norm, moe_router, moe_w_gate,
              moe_w_up, moe_w_down):
    rope = _rope_tables(x.shape[1], x.dtype)
    lb_w = jax.nn.softmax(hg_lower_bounds.astype(jnp.float32), axis=0)
    hg_lb = jnp.cumsum(lb_w, axis=0) - lb_w[0]
    cond_lat = jax.nn.silu(c)[:, None, :]
    cond_ctx = jax.nn.silu(c_ctx)[None, None, :]
    x_lat, x_ctx = x, ctx
    for l in range(DEPTH):
        ctx_out = l < DEPTH - 1
        m_lat = jnp.split(cond_lat @ w_mod[l] + b_mod[l], 6, axis=-1)
        m_ctx = jnp.split(cond_ctx @ w_mod[l] + b_mod[l], 6, axis=-1)
        y_lat, y_ctx = _token_mixing(
            _modulate(x_lat, m_lat[0], m_lat[1]), _modulate(x_ctx, m_ctx[0], m_ctx[1]),
            w_in[l], w_out[l], attn_q_norm[l], attn_k_norm[l], dn_conv[l], dn_a_log[l],
            dn_dt_bias[l], dn_norm[l], hg_lb[l], hg_norm[l], rope, ctx_out)
        x_lat = x_lat + m_lat[2] * y_lat
        x_lat = x_lat + m_lat[5] * _expert_choice_ffn(
            _modulate(x_lat, m_lat[3], m_lat[4]), moe_router[l], moe_w_gate[l], moe_w_up[l], moe_w_down[l])
        if ctx_out:
            x_ctx = x_ctx + m_ctx[2] * y_ctx
            x_ctx = x_ctx + m_ctx[5] * _expert_choice_ffn(
                _modulate(x_ctx, m_ctx[3], m_ctx[4]), moe_router[l], moe_w_gate[l], moe_w_up[l], moe_w_down[l])
    return x_lat
```

```python
import functools

import numpy as np
import jax
import jax.numpy as jnp
from jax import lax
from jax.experimental import pallas as pl
from jax.experimental.pallas import tpu as pltpu

F32 = jnp.float32
BF16 = jnp.bfloat16
HIGHEST = lax.Precision.HIGHEST

HEAD_DIM = 64
N_Q_HEADS = 8
N_KV_HEADS = 2
GQA_GROUP = N_Q_HEADS // N_KV_HEADS
DN_HEADS = 4
HG_HEADS = 4
GRID_W = 64
ROPE_THETA = 10000.0
CONV_K = 5
CHUNK = 64
N_EXPERTS = 16
CAPACITY_FACTOR = 2
EPS = 1e-6

ATTN_Q_W = N_Q_HEADS * HEAD_DIM
ATTN_KV_W = N_KV_HEADS * HEAD_DIM
REC_W = DN_HEADS * HEAD_DIM
LANES = 128
ROUTE_BLK = LANES
ROUTE_ALIGN = 16
ROUTE_WIN = ROUTE_BLK + ROUTE_ALIGN
N_LEVELS = 6
VMEM_LIMIT = 56 * 1024 * 1024

C_Q, C_K, C_V, C_DN, C_HG, C_BA, C_END = 0, 512, 640, 768, 1792, 3072, 3200


def _mm(a, b):
    return jnp.dot(a.astype(BF16), b.astype(BF16), preferred_element_type=F32)


def _mm_nt(a, b):
    return lax.dot_general(a.astype(BF16), b.astype(BF16), (((1,), (1,)), ((), ())),
                           preferred_element_type=F32)


def _mm_tn(a, b):
    return lax.dot_general(a.astype(BF16), b.astype(BF16), (((0,), (0,)), ((), ())),
                           preferred_element_type=F32)


def _mm_hi(a, b):
    return jnp.dot(a, b, precision=HIGHEST, preferred_element_type=F32)


def _sigmoid(x):
    return 1.0 / (1.0 + jnp.exp(-x))


def _silu(x):
    return x * _sigmoid(x)


def _cparams(sem):
    return pltpu.CompilerParams(dimension_semantics=sem, vmem_limit_bytes=VMEM_LIMIT)


def _tile(n, pref):
    return pref if n % pref == 0 else n


def _mod_kernel(c_ref, w_ref, b_ref, o_ref):
    o_ref[0] = _mm_hi(_silu(c_ref[...]), w_ref[0]) + b_ref[0]


def _modulation(cond, w_mod, b_mod):
    depth, d, n = w_mod.shape
    rows = cond.shape[0]
    tn = _tile(n, 1536)
    return pl.pallas_call(
        _mod_kernel,
        grid=(depth, n // tn),
        in_specs=[pl.BlockSpec((rows, d), lambda l, j: (0, 0)),
                  pl.BlockSpec((1, d, tn), lambda l, j: (l, 0, j)),
                  pl.BlockSpec((1, 1, tn), lambda l, j: (l, 0, j))],
        out_specs=pl.BlockSpec((1, rows, tn), lambda l, j: (l, 0, j)),
        out_shape=jax.ShapeDtypeStruct((depth, rows, n), F32),
        compiler_params=_cparams(("parallel", "parallel")),
        name="modulation",
    )(cond, w_mod, b_mod.reshape(depth, 1, n))


def _inproj_kernel(x_ref, sh_ref, sc_ref, w_ref, qn_ref, kn_ref, cos_ref, sin_ref, g_ref,
                   q_ref, kt_ref, v_ref, dn_ref, hg_ref, ba_ref, *, rotate):
    x = x_ref[0]
    ms = jnp.mean(x * x, axis=-1, keepdims=True)
    h = x * lax.rsqrt(ms + EPS) * (1.0 + sc_ref[0]) + sh_ref[0]
    hb = h.astype(BF16)
    gmat = g_ref[...]

    def proj(lo, hi):
        return jnp.dot(hb, w_ref[:, lo:hi], preferred_element_type=F32)

    def head_norm_rope(t, nw):
        t = t * lax.rsqrt(_mm_hi(t * t, gmat) + EPS) * nw
        if rotate:
            lane = lax.broadcasted_iota(jnp.int32, t.shape, 1)
            first = (lane % 32) < 16
            rot = jnp.where(first, -pltpu.roll(t, LANES - 16, 1), pltpu.roll(t, 16, 1))
            t = t * cos_ref[...] + rot * sin_ref[...]
        return t

    for j in range(ATTN_Q_W // LANES):
        qj = head_norm_rope(proj(C_Q + j * LANES, C_Q + (j + 1) * LANES), qn_ref[...])
        q_ref[0, :, j * LANES:(j + 1) * LANES] = (qj * HEAD_DIM ** -0.5).astype(BF16)
    kt = head_norm_rope(proj(C_K, C_V), kn_ref[...]).T
    kt_ref[0, 0] = kt[:HEAD_DIM].astype(BF16)
    kt_ref[0, 1] = kt[HEAD_DIM:].astype(BF16)
    v = proj(C_V, C_DN)
    v_ref[0, 0] = v[:, :HEAD_DIM].astype(BF16)
    v_ref[0, 1] = v[:, HEAD_DIM:].astype(BF16)
    dn_ref[0] = proj(C_DN, C_HG)
    hg_ref[0] = proj(C_HG, C_BA)
    ba_ref[0] = proj(C_BA, C_END)


def _input_projection(x, shift, scale, w, qn, kn, cos, sin, gmat, rotate):
    b, t, d = x.shape
    tm = _tile(t, 512)
    per_b = shift.shape[0] > 1
    mod_map = (lambda bi, i: (bi, 0, 0)) if per_b else (lambda bi, i: (0, 0, 0))
    const = lambda bi, i: (0, 0)
    out_shape = (
        jax.ShapeDtypeStruct((b, t, ATTN_Q_W), BF16),
        jax.ShapeDtypeStruct((b, N_KV_HEADS, HEAD_DIM, t), BF16),
        jax.ShapeDtypeStruct((b, N_KV_HEADS, t, HEAD_DIM), BF16),
        jax.ShapeDtypeStruct((b, t, C_HG - C_DN), F32),
        jax.ShapeDtypeStruct((b, t, C_BA - C_HG), F32),
        jax.ShapeDtypeStruct((b, t, C_END - C_BA), F32),
    )
    return pl.pallas_call(
        functools.partial(_inproj_kernel, rotate=rotate),
        grid=(b, t // tm),
        in_specs=[pl.BlockSpec((1, tm, d), lambda bi, i: (bi, i, 0)),
                  pl.BlockSpec((1, 1, d), mod_map),
                  pl.BlockSpec((1, 1, d), mod_map),
                  pl.BlockSpec((d, C_END), const),
                  pl.BlockSpec((1, LANES), const),
                  pl.BlockSpec((1, LANES), const),
                  pl.BlockSpec((tm, LANES), lambda bi, i: (i, 0)),
                  pl.BlockSpec((tm, LANES), lambda bi, i: (i, 0)),
                  pl.BlockSpec((LANES, LANES), const)],
        out_specs=(pl.BlockSpec((1, tm, ATTN_Q_W), lambda bi, i: (bi, i, 0)),
                   pl.BlockSpec((1, N_KV_HEADS, HEAD_DIM, tm), lambda bi, i: (bi, 0, 0, i)),
                   pl.BlockSpec((1, N_KV_HEADS, tm, HEAD_DIM), lambda bi, i: (bi, 0, i, 0)),
                   pl.BlockSpec((1, tm, C_HG - C_DN), lambda bi, i: (bi, i, 0)),
                   pl.BlockSpec((1, tm, C_BA - C_HG), lambda bi, i: (bi, i, 0)),
                   pl.BlockSpec((1, tm, C_END - C_BA), lambda bi, i: (bi, i, 0))),
        out_shape=out_shape,
        compiler_params=_cparams(("parallel", "parallel")),
        name="input_projection",
    )(x, shift, scale, w, qn, kn, cos, sin, gmat)


def _attn_kernel(*refs, n_src):
    q_ref, o_ref = refs[0], refs[-1]
    kts = [refs[1 + 2 * i] for i in range(n_src)]
    vs = [refs[2 + 2 * i] for i in range(n_src)]
    q = q_ref[0]
    outs = []
    for h in range(GQA_GROUP):
        qh = q[:, h * HEAD_DIM:(h + 1) * HEAD_DIM]
        ss = [jnp.dot(qh, kt[0, 0], preferred_element_type=F32) for kt in kts]
        m = functools.reduce(jnp.maximum, [jnp.max(s, axis=-1, keepdims=True) for s in ss])
        ps = [jnp.exp(s - m) for s in ss]
        denom = functools.reduce(jnp.add, [jnp.sum(p, axis=-1, keepdims=True) for p in ps])
        o = functools.reduce(jnp.add, [jnp.dot(p.astype(BF16), v[0, 0], preferred_element_type=F32)
                                       for p, v in zip(ps, vs)])
        outs.append(o / denom)
    o_ref[0] = jnp.concatenate(outs, axis=-1).astype(BF16)


def _attention(q, sources):
    b, t, _ = q.shape
    tq = _tile(t, 256)
    gw = GQA_GROUP * HEAD_DIM
    in_specs = [pl.BlockSpec((1, tq, gw), lambda bi, g, i: (bi, i, g))]
    args = [q]
    for kt, v in sources:
        tk = kt.shape[-1]
        in_specs.append(pl.BlockSpec((1, 1, HEAD_DIM, tk), lambda bi, g, i: (bi, g, 0, 0)))
        in_specs.append(pl.BlockSpec((1, 1, tk, HEAD_DIM), lambda bi, g, i: (bi, g, 0, 0)))
        args += [kt, v]
    return pl.pallas_call(
        functools.partial(_attn_kernel, n_src=len(sources)),
        grid=(b, N_KV_HEADS, t // tq),
        in_specs=in_specs,
        out_specs=pl.BlockSpec((1, tq, gw), lambda bi, g, i: (bi, i, g)),
        out_shape=jax.ShapeDtypeStruct((b, t, ATTN_Q_W), BF16),
        compiler_params=_cparams(("parallel", "parallel", "parallel")),
        name="attention",
    )(*args)


def _dn_prep_kernel(x_ref, xp_ref, xn_ref, ba_ref, cw_ref, na_ref, dtb_ref, g_ref, o_ref, gb_ref, buf):
    i = pl.program_id(1)
    n = pl.num_programs(1)
    tm = x_ref.shape[1]
    buf[0:8] = jnp.where(i > 0, xp_ref[0], 0.0)
    buf[8:8 + tm] = x_ref[0]
    buf[8 + tm:16 + tm] = jnp.where(i < n - 1, xn_ref[0], 0.0)
    half = CONV_K // 2
    y = cw_ref[0:1, :] * buf[8 - half:8 - half + tm]
    for j in range(1, CONV_K):
        y = y + cw_ref[j:j + 1, :] * buf[8 - half + j:8 - half + j + tm]
    y = _silu(y)
    gmat = g_ref[...]
    q = y[:, :REC_W]
    k = y[:, REC_W:2 * REC_W]
    o_ref[0, :, 0:REC_W] = q * lax.rsqrt(_mm_hi(q * q, gmat) + EPS) * HEAD_DIM ** -0.5
    o_ref[0, :, REC_W:2 * REC_W] = k * lax.rsqrt(_mm_hi(k * k, gmat) + EPS)
    o_ref[0, :, 2 * REC_W:] = y[:, 2 * REC_W:]
    ba = ba_ref[0]
    z = ba + dtb_ref[...]
    softplus = jnp.maximum(z, 0.0) + jnp.log1p(jnp.exp(-jnp.abs(z)))
    lane = lax.broadcasted_iota(jnp.int32, ba.shape, 1)
    gb_ref[0] = jnp.where(lane < 2 * DN_HEADS, _sigmoid(ba), na_ref[...] * softplus)


def _dn_prep(dn, ba, conv_w, neg_a, dt_bias, gsum):
    b, t, _ = dn.shape
    tm = _tile(t, 512)
    w3 = 3 * REC_W
    nb8 = t // 8
    return pl.pallas_call(
        _dn_prep_kernel,
        grid=(b, t // tm),
        in_specs=[pl.BlockSpec((1, tm, w3), lambda bi, i: (bi, i, 0)),
                  pl.BlockSpec((1, 8, w3), lambda bi, i: (bi, jnp.maximum(i * (tm // 8) - 1, 0), 0)),
                  pl.BlockSpec((1, 8, w3), lambda bi, i: (bi, jnp.minimum((i + 1) * (tm // 8), nb8 - 1), 0)),
                  pl.BlockSpec((1, tm, LANES), lambda bi, i: (bi, i, 0)),
                  pl.BlockSpec((8, w3), lambda bi, i: (0, 0)),
                  pl.BlockSpec((1, LANES), lambda bi, i: (0, 0)),
                  pl.BlockSpec((1, LANES), lambda bi, i: (0, 0)),
                  pl.BlockSpec((REC_W, REC_W), lambda bi, i: (0, 0))],
        out_specs=(pl.BlockSpec((1, tm, w3), lambda bi, i: (bi, i, 0)),
                   pl.BlockSpec((1, tm, LANES), lambda bi, i: (bi, i, 0))),
        out_shape=(jax.ShapeDtypeStruct((b, t, w3), F32), jax.ShapeDtypeStruct((b, t, LANES), F32)),
        scratch_shapes=[pltpu.VMEM((tm + 16, w3), F32)],
        compiler_params=_cparams(("parallel", "parallel")),
        name="deltanet_prep",
    )(dn, dn, dn, ba, conv_w, neg_a, dt_bias, gsum)


def _stack(a, hm):
    return jnp.concatenate([a] * DN_HEADS, axis=0) * hm


def _unstack(a):
    return a[0:CHUNK] + a[CHUNK:2 * CHUNK] + a[2 * CHUNK:3 * CHUNK] + a[3 * CHUNK:4 * CHUNK]


def _dn_scan_kernel(x_ref, gb_ref, s0_ref, l_ref, lt_ref, hm_ref, o_ref, sf_ref, s_scr, *, rev, nchunk):
    @pl.when(pl.program_id(1) == 0)
    def _():
        s_scr[...] = s0_ref[0]

    lmat = l_ref[...]
    ltmat = lt_ref[...]
    hm = hm_ref[...]
    n = REC_W
    eye = (lax.broadcasted_iota(jnp.int32, (n, n), 0) == lax.broadcasted_iota(jnp.int32, (n, n), 1)).astype(F32)
    ones = jnp.ones((n, n), F32)
    d = 1 if rev else 0

    def body(c, carry):
        cc = (nchunk - 1 - c) if rev else c
        r0 = pl.multiple_of(cc * CHUNK, CHUNK)
        x = x_ref[0, pl.ds(r0, CHUNK), :]
        gb = gb_ref[0, pl.ds(r0, CHUNK), :]
        qs = _stack(x[:, :n], hm)
        ks = _stack(x[:, n:2 * n], hm)
        vs = _stack(x[:, 2 * n:], hm)

        def head_cols(off):
            return jnp.concatenate(
                [jnp.broadcast_to(gb[:, off + h:off + h + 1], (CHUNK, n)) for h in range(DN_HEADS)], axis=0)

        bcol = head_cols(d * DN_HEADS)
        gcol = head_cols(2 * DN_HEADS + d * DN_HEADS)
        cumc = _mm_hi(lmat, gcol)
        cumr = _mm_hi(ones, gcol * ltmat)
        clast = _mm_hi(hm, gcol)
        decay = jnp.where(lmat > 0, jnp.exp(jnp.where(lmat > 0, cumc - cumr, 0.0)), 0.0)
        ecum = jnp.exp(cumc)
        kbs = ks * bcol
        m = _mm_nt(kbs, ks) * decay * (1.0 - eye)
        xinv = eye - m
        p = m
        for _ in range(N_LEVELS - 1):
            p = _mm_hi(p, p)
            xinv = xinv + _mm_hi(xinv, p)
        u = _mm_hi(xinv, vs * bcol)
        w = _mm_hi(xinv, kbs * ecum)
        s = s_scr[...]
        v_new = u - _mm(w, s)
        attn = _mm_nt(qs, ks) * decay
        out = _mm(qs * ecum, s) + _mm(attn, v_new)
        o_ref[0, pl.ds(r0, CHUNK), :] = _unstack(out)
        s_scr[...] = s * jnp.exp(clast) + _mm_tn(ks * jnp.exp(clast - cumc), v_new)
        return carry

    lax.fori_loop(0, nchunk, body, 0)
    sf_ref[0] = s_scr[...]


def _hg_scan_kernel(q_ref, f_ref, i_ref, lb_ref, s0_ref, a_ref, hm_ref, lm_ref, o_ref, sf_ref, s_scr,
                    *, rev, nchunk):
    @pl.when(pl.program_id(1) == 0)
    def _():
        s_scr[...] = s0_ref[0]

    amat = a_ref[...]
    hm = hm_ref[...]
    lb = lb_ref[...]
    n = REC_W
    ones = jnp.ones((CHUNK, n), F32)

    def body(c, carry):
        cc = (nchunk - 1 - c) if rev else c
        r0 = pl.multiple_of(cc * CHUNK, CHUNK)
        q = _silu(q_ref[0, pl.ds(r0, CHUNK), :])
        f = lb + (1.0 - lb) * _sigmoid(f_ref[0, pl.ds(r0, CHUNK), :])
        k = 1.0 - f
        g = jnp.log(f)
        v = i_ref[0, pl.ds(r0, CHUNK), :]
        ex = jnp.exp(_mm_hi(amat, g))
        attn = _mm_nt(_stack(q, hm), k) * lm_ref[N_LEVELS * n:(N_LEVELS + 1) * n, :]
        for lev in range(N_LEVELS):
            xl = ex[(2 + lev) * CHUNK:(3 + lev) * CHUNK]
            attn = attn + _mm_nt(_stack(q * xl, hm), k * xl) * lm_ref[lev * n:(lev + 1) * n, :]
        s = s_scr[...]
        out = _unstack(_mm(attn, v) * hm) + _mm(q * ex[0:CHUNK], s)
        o_ref[0, pl.ds(r0, CHUNK), :] = out
        clast = lax.dot_general(g, ones, (((0,), (0,)), ((), ())), precision=HIGHEST,
                                preferred_element_type=F32)
        s_scr[...] = s * jnp.exp(clast) + _mm_tn(k * ex[CHUNK:2 * CHUNK], v) * hm
        return carry

    lax.fori_loop(0, nchunk, body, 0)
    sf_ref[0] = s_scr[...]


def _scan_order(rev, nblk):
    return (lambda i: nblk - 1 - i) if rev else (lambda i: i)


def _dn_scan(qkv, gb, s0, consts, rev):
    b, t, w3 = qkv.shape
    tb = _tile(t, 512)
    nblk = t // tb
    order = _scan_order(rev, nblk)
    n = REC_W
    const = lambda bi, i: (0, 0)
    return pl.pallas_call(
        functools.partial(_dn_scan_kernel, rev=rev, nchunk=tb // CHUNK),
        grid=(b, nblk),
        in_specs=[pl.BlockSpec((1, tb, w3), lambda bi, i: (bi, order(i), 0)),
                  pl.BlockSpec((1, tb, LANES), lambda bi, i: (bi, order(i), 0)),
                  pl.BlockSpec((1, n, n), lambda bi, i: (bi, 0, 0)),
                  pl.BlockSpec((n, n), const), pl.BlockSpec((n, n), const), pl.BlockSpec((n, n), const)],
        out_specs=(pl.BlockSpec((1, tb, n), lambda bi, i: (bi, order(i), 0)),
                   pl.BlockSpec((1, n, n), lambda bi, i: (bi, 0, 0))),
        out_shape=(jax.ShapeDtypeStruct((b, t, n), F32), jax.ShapeDtypeStruct((b, n, n), F32)),
        scratch_shapes=[pltpu.VMEM((n, n), F32)],
        compiler_params=_cparams(("parallel", "arbitrary")),
        name="deltanet_scan_bwd" if rev else "deltanet_scan_fwd",
    )(qkv, gb, s0, consts["l"], consts["lt"], consts["hm"])


def _hg_scan(hg, lb, s0, consts, rev):
    b, t, _ = hg.shape
    tb = _tile(t, 512)
    nblk = t // tb
    order = _scan_order(rev, nblk)
    n = REC_W
    fcol = 2 if rev else 1
    const = lambda bi, i: (0, 0)
    return pl.pallas_call(
        functools.partial(_hg_scan_kernel, rev=rev, nchunk=tb // CHUNK),
        grid=(b, nblk),
        in_specs=[pl.BlockSpec((1, tb, n), lambda bi, i: (bi, order(i), 0)),
                  pl.BlockSpec((1, tb, n), lambda bi, i: (bi, order(i), fcol)),
                  pl.BlockSpec((1, tb, n), lambda bi, i: (bi, order(i), 3)),
                  pl.BlockSpec((1, n), const),
                  pl.BlockSpec((1, n, n), lambda bi, i: (bi, 0, 0)),
                  pl.BlockSpec(((2 + N_LEVELS) * CHUNK, CHUNK), const),
                  pl.BlockSpec((n, n), const),
                  pl.BlockSpec(((N_LEVELS + 1) * n, CHUNK), const)],
        out_specs=(pl.BlockSpec((1, tb, n), lambda bi, i: (bi, order(i), 0)),
                   pl.BlockSpec((1, n, n), lambda bi, i: (bi, 0, 0))),
        out_shape=(jax.ShapeDtypeStruct((b, t, n), F32), jax.ShapeDtypeStruct((b, n, n), F32)),
        scratch_shapes=[pltpu.VMEM((n, n), F32)],
        compiler_params=_cparams(("parallel", "arbitrary")),
        name="hgrn2_scan_bwd" if rev else "hgrn2_scan_fwd",
    )(hg, hg, hg, lb, s0, consts["a"], consts["hm"], consts["lm"])


def _scan_constants(rev):
    t = np.arange(CHUNK)
    p = (CHUNK - 1 - t) if rev else t
    pt, pu = p[:, None], p[None, :]
    causal = (pu <= pt).astype(np.float32)
    head = np.arange(REC_W) // CHUNK
    hm = (head[:, None] == head[None, :]).astype(np.float32)
    l_bd = np.kron(np.eye(DN_HEADS, dtype=np.float32), causal)
    rows = [causal, (pu > pt).astype(np.float32)]
    masks = []
    for lev in range(N_LEVELS):
        blk = CHUNK >> (lev + 1)
        bound = (pt // (2 * blk)) * (2 * blk) + blk - 1
        right = (pt % (2 * blk)) >= blk
        rows.append(np.where(right, (pu > bound) & (pu <= pt), (pu > pt) & (pu <= bound)).astype(np.float32))
        masks.append(((pt // (2 * blk) == pu // (2 * blk)) & right & ((pu % (2 * blk)) < blk)).astype(np.float32))
    masks.append(np.eye(CHUNK, dtype=np.float32))
    return {
        "l": jnp.asarray(l_bd), "lt": jnp.asarray(l_bd.T), "hm": jnp.asarray(hm),
        "a": jnp.asarray(np.concatenate(rows, axis=0)),
        "lm": jnp.asarray(np.concatenate([np.tile(mk, (HG_HEADS, 1)) for mk in masks], axis=0)),
    }


def _outproj_kernel(a_ref, dof_ref, dob_ref, z_ref, gof_ref, gob_ref, gg_ref, dnw_ref, hgw_ref, g_ref,
                    w_ref, x_ref, gate_ref, sh_ref, sc_ref, rt_ref, x1_ref, h2_ref, aff_ref):
    gmat = g_ref[...]

    def gated(o, z, nw):
        return o * lax.rsqrt(_mm_hi(o * o, gmat) + EPS) * nw * _silu(z)

    dmix = gated(dof_ref[0] + dob_ref[0], z_ref[0], dnw_ref[...])
    gmix = gated(gof_ref[0] + gob_ref[0], gg_ref[0], hgw_ref[...])
    y = (jnp.dot(a_ref[0], w_ref[0:ATTN_Q_W, :], preferred_element_type=F32)
         + jnp.dot(dmix.astype(BF16), w_ref[ATTN_Q_W:ATTN_Q_W + REC_W, :], preferred_element_type=F32)
         + jnp.dot(gmix.astype(BF16), w_ref[ATTN_Q_W + REC_W:, :], preferred_element_type=F32))
    x1 = x_ref[0] + gate_ref[0] * y
    x1_ref[0] = x1
    ms = jnp.mean(x1 * x1, axis=-1, keepdims=True)
    h2 = x1 * lax.rsqrt(ms + EPS) * (1.0 + sc_ref[0]) + sh_ref[0]
    h2_ref[0] = h2.astype(BF16)
    logits = lax.dot_general(rt_ref[...], h2, (((1,), (1,)), ((), ())), precision=HIGHEST,
                             preferred_element_type=F32)
    e = jnp.exp(logits - jnp.max(logits, axis=0, keepdims=True))
    aff_ref[0] = e / jnp.sum(e, axis=0, keepdims=True)


def _output_projection(a, dof, dob, dn, gof, gob, hg, dnw, hgw, gavg, w_out, x, gate, shift, scale, router_t):
    b, t, d = x.shape
    tm = _tile(t, 512)
    n = REC_W
    per_b = gate.shape[0] > 1
    mod_map = (lambda bi, i: (bi, 0, 0)) if per_b else (lambda bi, i: (0, 0, 0))
    const = lambda bi, i: (0, 0)
    tok = lambda bi, i: (bi, i, 0)
    return pl.pallas_call(
        _outproj_kernel,
        grid=(b, t // tm),
        in_specs=[pl.BlockSpec((1, tm, ATTN_Q_W), tok),
                  pl.BlockSpec((1, tm, n), tok), pl.BlockSpec((1, tm, n), tok),
                  pl.BlockSpec((1, tm, n), lambda bi, i: (bi, i, 3)),
                  pl.BlockSpec((1, tm, n), tok), pl.BlockSpec((1, tm, n), tok),
                  pl.BlockSpec((1, tm, n), lambda bi, i: (bi, i, 4)),
                  pl.BlockSpec((1, n), const), pl.BlockSpec((1, n), const),
                  pl.BlockSpec((n, n), const),
                  pl.BlockSpec((d, d), const),
                  pl.BlockSpec((1, tm, d), tok),
                  pl.BlockSpec((1, 1, d), mod_map), pl.BlockSpec((1, 1, d), mod_map),
                  pl.BlockSpec((1, 1, d), mod_map),
                  pl.BlockSpec((N_EXPERTS, d), const)],
        out_specs=(pl.BlockSpec((1, tm, d), tok), pl.BlockSpec((1, tm, d), tok),
                   pl.BlockSpec((1, N_EXPERTS, tm), lambda bi, i: (bi, 0, i))),
        out_shape=(jax.ShapeDtypeStruct((b, t, d), F32), jax.ShapeDtypeStruct((b, t, d), BF16),
                   jax.ShapeDtypeStruct((b, N_EXPERTS, t), F32)),
        compiler_params=_cparams(("parallel", "parallel")),
        name="output_projection_router",
    )(a, dof, dob, dn, gof, gob, hg, dnw, hgw, gavg, w_out, x, gate, shift, scale, router_t)


def _select_kernel(aff_ref, u_ref, bs_ref, bst_ref, su_ref, slot_ref, base_ref, inc_scr, *, cap, nblk):
    x = aff_ref[0]
    bits = pltpu.bitcast(x, jnp.int32)
    lo = jnp.zeros((N_EXPERTS, 1), jnp.int32)
    for bit in range(30, -1, -1):
        cand = lo | (1 << bit)
        cnt = jnp.sum((bits >= cand).astype(jnp.int32), axis=1, keepdims=True)
        lo = jnp.where(cnt >= cap, cand, lo)
    gt = bits > lo
    eq = bits == lo
    umat = u_ref[...]

    def prefix(mask):
        mb = mask.astype(BF16)
        for j in range(nblk):
            inc_scr[:, j * LANES:(j + 1) * LANES] = jnp.dot(mb[:, j * LANES:(j + 1) * LANES], umat,
                                                            preferred_element_type=F32)
        totals = jnp.dot(mb, bs_ref[...], preferred_element_type=F32)
        offs = _mm_hi(totals, su_ref[...])
        return inc_scr[...] + _mm_hi(offs, bst_ref[...]), offs

    eqf = eq.astype(F32)
    n_gt = jnp.sum(gt.astype(F32), axis=1, keepdims=True)
    eq_before, _ = prefix(eqf)
    sel = gt | (eq & ((eq_before - eqf) < (cap - n_gt)))
    self_ = sel.astype(F32)
    pos, offs = prefix(self_)
    slot_ref[0] = jnp.where(sel, pos - 1.0, -1.0).astype(jnp.int32)
    base_ref[0] = offs.astype(jnp.int32)


def _moe_select(aff_t, cap):
    b, e, t = aff_t.shape
    nblk = t // ROUTE_BLK
    u = jnp.asarray(np.triu(np.ones((LANES, LANES), np.float32)), BF16)
    blk = np.arange(t) // ROUTE_BLK
    bs = (blk[:, None] == np.arange(nblk)[None, :]).astype(np.float32)
    su = np.triu(np.ones((nblk, nblk), np.float32), 1)
    const = lambda bi: (0, 0)
    return pl.pallas_call(
        functools.partial(_select_kernel, cap=cap, nblk=nblk),
        grid=(b,),
        in_specs=[pl.BlockSpec((1, e, t), lambda bi: (bi, 0, 0)),
                  pl.BlockSpec((LANES, LANES), const),
                  pl.BlockSpec((t, nblk), const),
                  pl.BlockSpec((nblk, t), const),
                  pl.BlockSpec((nblk, nblk), const)],
        out_specs=(pl.BlockSpec((1, e, t), lambda bi: (bi, 0, 0)),
                   pl.BlockSpec((1, e, nblk), lambda bi: (bi, 0, 0))),
        out_shape=(jax.ShapeDtypeStruct((b, e, t), jnp.int32), jax.ShapeDtypeStruct((b, e, nblk), jnp.int32)),
        scratch_shapes=[pltpu.VMEM((e, t), F32)],
        compiler_params=_cparams(("parallel",)),
        name="moe_select",
    )(aff_t, u, jnp.asarray(bs, BF16), jnp.asarray(bs.T), jnp.asarray(su))


def _gather_kernel(base_sm, slot_ref, h_ref, xe_ref, *, nblk):
    bi, ei = pl.program_id(0), pl.program_id(1)
    xe_ref[0, 0] = jnp.zeros(xe_ref.shape[2:], BF16)
    row = lax.broadcasted_iota(jnp.int32, (ROUTE_WIN, ROUTE_BLK), 0)

    def body(k, carry):
        base = base_sm[(bi * N_EXPERTS + ei) * nblk + k]
        bal = pl.multiple_of((base // ROUTE_ALIGN) * ROUTE_ALIGN, ROUTE_ALIGN)
        t0 = pl.multiple_of(k * ROUTE_BLK, ROUTE_BLK)
        srow = slot_ref[0, 0, :, pl.ds(t0, ROUTE_BLK)]
        onehot = jnp.where(row == srow - bal, 1.0, 0.0).astype(BF16)
        rows = jnp.dot(onehot, h_ref[0, pl.ds(t0, ROUTE_BLK), :], preferred_element_type=F32)
        xe_ref[0, 0, pl.ds(bal, ROUTE_WIN), :] = xe_ref[0, 0, pl.ds(bal, ROUTE_WIN), :] + rows.astype(BF16)
        return carry

    lax.fori_loop(0, nblk, body, 0)


def _moe_gather(h2, slot, base_flat, cap):
    b, t, d = h2.shape
    nblk = t // ROUTE_BLK
    cp = cap + ROUTE_WIN
    grid_spec = pltpu.PrefetchScalarGridSpec(
        num_scalar_prefetch=1,
        grid=(b, N_EXPERTS),
        in_specs=[pl.BlockSpec((1, 1, 1, t), lambda bi, ei, sm: (bi, ei, 0, 0)),
                  pl.BlockSpec((1, t, d), lambda bi, ei, sm: (bi, 0, 0))],
        out_specs=pl.BlockSpec((1, 1, cp, d), lambda bi, ei, sm: (bi, ei, 0, 0)),
    )
    return pl.pallas_call(
        functools.partial(_gather_kernel, nblk=nblk),
        grid_spec=grid_spec,
        out_shape=jax.ShapeDtypeStruct((b, N_EXPERTS, cp, d), BF16),
        compiler_params=_cparams(("parallel", "arbitrary")),
        name="moe_gather",
    )(base_flat, slot.reshape(b, N_EXPERTS, 1, t), h2)


def _ffn_kernel(x_ref, wg_ref, wu_ref, wd_ref, y_ref):
    x = x_ref[0, 0]
    a = jnp.dot(x, wg_ref[0], preferred_element_type=F32)
    u = jnp.dot(x, wu_ref[0], preferred_element_type=F32)
    y_ref[0, 0] = jnp.dot((_silu(a) * u).astype(BF16), wd_ref[0], preferred_element_type=F32).astype(BF16)


def _moe_ffn(xe, wg, wu, wd):
    b, e, cp, d = xe.shape
    f = wg.shape[-1]
    return pl.pallas_call(
        _ffn_kernel,
        grid=(e, b),
        in_specs=[pl.BlockSpec((1, 1, cp, d), lambda ei, bi: (bi, ei, 0, 0)),
                  pl.BlockSpec((1, d, f), lambda ei, bi: (ei, 0, 0)),
                  pl.BlockSpec((1, d, f), lambda ei, bi: (ei, 0, 0)),
                  pl.BlockSpec((1, f, d), lambda ei, bi: (ei, 0, 0))],
        out_specs=pl.BlockSpec((1, 1, cp, d), lambda ei, bi: (bi, ei, 0, 0)),
        out_shape=jax.ShapeDtypeStruct((b, e, cp, d), BF16),
        compiler_params=_cparams(("parallel", "parallel")),
        name="moe_ffn",
    )(xe, wg, wu, wd)


def _combine_kernel(base_sm, ye_ref, x_ref, gate_ref, slot_ref, aff_ref, o_ref, *, nblk):
    bi, k = pl.program_id(0), pl.program_id(2)
    lane = lax.broadcasted_iota(jnp.int32, (ROUTE_BLK, ROUTE_WIN), 1)
    slot = slot_ref[0]
    aff = aff_ref[0]
    acc = jnp.zeros(x_ref.shape[1:], F32)
    for e in range(N_EXPERTS):
        base = base_sm[(bi * N_EXPERTS + e) * nblk + k]
        bal = pl.multiple_of((base // ROUTE_ALIGN) * ROUTE_ALIGN, ROUTE_ALIGN)
        onehot = jnp.where(lane == slot[:, e:e + 1] - bal, 1.0, 0.0).astype(BF16)
        rows = jnp.dot(onehot, ye_ref[0, e, pl.ds(bal, ROUTE_WIN), :], preferred_element_type=F32)
        acc = acc + rows * aff[:, e:e + 1]
    o_ref[0] = x_ref[0] + gate_ref[0] * acc


def _moe_combine(ye, x1, gate, slot_tm, aff_tm, base_flat):
    b, t, d = x1.shape
    cp = ye.shape[2]
    nblk = t // ROUTE_BLK
    dh = d // 2
    per_b = gate.shape[0] > 1
    grid_spec = pltpu.PrefetchScalarGridSpec(
        num_scalar_prefetch=1,
        grid=(b, 2, nblk),
        in_specs=[pl.BlockSpec((1, N_EXPERTS, cp, dh), lambda bi, j, k, sm: (bi, 0, 0, j)),
                  pl.BlockSpec((1, ROUTE_BLK, dh), lambda bi, j, k, sm: (bi, k, j)),
                  pl.BlockSpec((1, 1, dh), (lambda bi, j, k, sm: (bi, 0, j)) if per_b
                               else (lambda bi, j, k, sm: (0, 0, j))),
                  pl.BlockSpec((1, ROUTE_BLK, N_EXPERTS), lambda bi, j, k, sm: (bi, k, 0)),
                  pl.BlockSpec((1, ROUTE_BLK, N_EXPERTS), lambda bi, j, k, sm: (bi, k, 0))],
        out_specs=pl.BlockSpec((1, ROUTE_BLK, dh), lambda bi, j, k, sm: (bi, k, j)),
    )
    return pl.pallas_call(
        functools.partial(_combine_kernel, nblk=nblk),
        grid_spec=grid_spec,
        out_shape=jax.ShapeDtypeStruct((b, t, d), F32),
        compiler_params=_cparams(("parallel", "parallel", "arbitrary")),
        name="moe_combine",
    )(base_flat, ye, x1, gate, slot_tm, aff_tm)


def _expert_choice_ffn(x1, h2, aff_t, gate, wg, wu, wd):
    b, t, _ = x1.shape
    cap = CAPACITY_FACTOR * t // N_EXPERTS
    slot, base = _moe_select(aff_t, cap)
    base_flat = base.reshape(-1)
    xe = _moe_gather(h2, slot, base_flat, cap)
    ye = _moe_ffn(xe, wg, wu, wd)
    return _moe_combine(ye, x1, gate, jnp.swapaxes(slot, 1, 2), jnp.swapaxes(aff_t, 1, 2), base_flat)


def _rope_tables(n_tokens):
    rows = n_tokens // GRID_W
    row = jnp.repeat(jnp.arange(rows, dtype=F32), GRID_W)
    col = jnp.tile(jnp.arange(GRID_W, dtype=F32), rows)
    n_freq = HEAD_DIM // 4
    inv_freq = ROPE_THETA ** (-jnp.arange(n_freq, dtype=F32) / n_freq)
    ang_r = row[:, None] * inv_freq
    ang_c = col[:, None] * inv_freq
    cos = jnp.concatenate([jnp.cos(ang_r)] * 2 + [jnp.cos(ang_c)] * 2, axis=-1)
    sin = jnp.concatenate([jnp.sin(ang_r)] * 2 + [jnp.sin(ang_c)] * 2, axis=-1)
    return jnp.tile(cos, (1, LANES // HEAD_DIM)), jnp.tile(sin, (1, LANES // HEAD_DIM))


def _block_diag(width, block, value):
    idx = np.arange(width) // block
    return jnp.asarray((idx[:, None] == idx[None, :]).astype(np.float32) * value)


def kernel(x, c, ctx, c_ctx, w_mod, b_mod, w_in, w_out, attn_q_norm, attn_k_norm, dn_conv, dn_a_log,
           dn_dt_bias, dn_norm, hg_lower_bounds, hg_norm, moe_router, moe_w_gate, moe_w_up, moe_w_down):
    depth = w_mod.shape[0]
    b, t_lat, d = x.shape
    cos, sin = _rope_tables(t_lat)
    g_head = _block_diag(LANES, HEAD_DIM, 1.0 / HEAD_DIM)
    g_mean = _block_diag(REC_W, HEAD_DIM, 1.0 / HEAD_DIM)
    g_sum = _block_diag(REC_W, HEAD_DIM, 1.0)
    consts = {False: _scan_constants(False), True: _scan_constants(True)}
    s_zero = jnp.zeros((b, REC_W, REC_W), F32)

    lb_w = jax.nn.softmax(hg_lower_bounds.astype(F32), axis=0)
    hg_lb = jnp.cumsum(lb_w, axis=0) - lb_w[0]

    rows = ((b + 1 + 7) // 8) * 8
    cond = jnp.zeros((rows, d), F32).at[:b].set(c).at[b].set(c_ctx)
    mod = _modulation(cond, w_mod, b_mod)

    n_small = 4 * DN_HEADS
    w_in_r = jnp.concatenate(
        [w_in[:, :, :C_HG], w_in[:, :, C_HG + n_small:], w_in[:, :, C_HG:C_HG + n_small],
         jnp.zeros((depth, d, C_END - C_BA - n_small), w_in.dtype)], axis=-1).astype(BF16)
    w_out_b = w_out.astype(BF16)
    wg_b, wu_b, wd_b = moe_w_gate.astype(BF16), moe_w_up.astype(BF16), moe_w_down.astype(BF16)

    x_lat, x_ctx = x, ctx
    for l in range(depth):
        ctx_out = l < depth - 1
        m_lat = [mod[l, :b, j * d:(j + 1) * d][:, None, :] for j in range(6)]
        m_ctx = [mod[l, b:b + 1, j * d:(j + 1) * d][:, None, :] for j in range(6)]
        qn = jnp.tile(attn_q_norm[l], LANES // HEAD_DIM)[None, :]
        kn = jnp.tile(attn_k_norm[l], LANES // HEAD_DIM)[None, :]
        conv_w = jnp.zeros((8, 3 * REC_W), F32).at[:CONV_K].set(dn_conv[l])
        pad = jnp.zeros((LANES - 4 * DN_HEADS,), F32)
        neg_a = jnp.concatenate([jnp.zeros((2 * DN_HEADS,), F32), -jnp.exp(dn_a_log[l].reshape(-1)), pad])[None, :]
        dt_b = jnp.concatenate([jnp.zeros((2 * DN_HEADS,), F32), dn_dt_bias[l].reshape(-1), pad])[None, :]
        dnw = jnp.tile(dn_norm[l], DN_HEADS)[None, :]
        hgw = jnp.tile(hg_norm[l], HG_HEADS)[None, :]
        router_t = moe_router[l].T

        streams = {}
        for name, xs, ms, rotate in (("ctx", x_ctx, m_ctx, False), ("lat", x_lat, m_lat, True)):
            t = xs.shape[1]
            q, kt, v, dn, hg, ba = _input_projection(xs, ms[0], ms[1], w_in_r[l], qn, kn,
                                                     cos[:t], sin[:t], g_head, rotate)
            qkv, gb = _dn_prep(dn, ba, conv_w, neg_a, dt_b, g_sum)
            streams[name] = dict(q=q, kt=kt, v=v, dn=dn, hg=hg, qkv=qkv, gb=gb)

        sc, sl = streams["ctx"], streams["lat"]
        rec = {}
        for rev in (False, True):
            o_c, s_c = _dn_scan(sc["qkv"], sc["gb"], s_zero, consts[rev], rev)
            o_l, _ = _dn_scan(sl["qkv"], sl["gb"], s_c, consts[rev], rev)
            lb = hg_lb[l, 1 if rev else 0][None, :]
            g_c, t_c = _hg_scan(sc["hg"], lb, s_zero, consts[rev], rev)
            g_l, _ = _hg_scan(sl["hg"], lb, t_c, consts[rev], rev)
            rec[rev] = dict(dn_ctx=o_c, dn_lat=o_l, hg_ctx=g_c, hg_lat=g_l)

        a_lat = _attention(sl["q"], [(sc["kt"], sc["v"]), (sl["kt"], sl["v"])])
        x1, h2, aff_t = _output_projection(
            a_lat, rec[False]["dn_lat"], rec[True]["dn_lat"], sl["dn"], rec[False]["hg_lat"],
            rec[True]["hg_lat"], sl["hg"], dnw, hgw, g_mean, w_out_b[l], x_lat, m_lat[2], m_lat[3], m_lat[4],
            router_t)
        x_lat = _expert_choice_ffn(x1, h2, aff_t, m_lat[5], wg_b[l], wu_b[l], wd_b[l])
        if ctx_out:
            a_ctx = _attention(sc["q"], [(sc["kt"], sc["v"])])
            x1, h2, aff_t = _output_projection(
                a_ctx, rec[False]["dn_ctx"], rec[True]["dn_ctx"], sc["dn"], rec[False]["hg_ctx"],
                rec[True]["hg_ctx"], sc["hg"], dnw, hgw, g_mean, w_out_b[l], x_ctx, m_ctx[2], m_ctx[3],
                m_ctx[4], router_t)
            x_ctx = _expert_choice_ffn(x1, h2, aff_t, m_ctx[5], wg_b[l], wu_b[l], wd_b[l])
    return x_lat
```

```python
import functools

import numpy as np
import jax
import jax.numpy as jnp
from jax import lax
from jax.experimental import pallas as pl
from jax.experimental.pallas import tpu as pltpu

F32 = jnp.float32
BF16 = jnp.bfloat16
HIGHEST = lax.Precision.HIGHEST

HEAD_DIM = 64
N_Q_HEADS = 8
N_KV_HEADS = 2
GQA_GROUP = N_Q_HEADS // N_KV_HEADS
DN_HEADS = 4
HG_HEADS = 4
GRID_W = 64
ROPE_THETA = 10000.0
CONV_K = 5
CHUNK = 64
N_EXPERTS = 16
CAPACITY_FACTOR = 2
EPS = 1e-6
LOG2E = 1.4426950408889634

ATTN_Q_W = N_Q_HEADS * HEAD_DIM
ATTN_KV_W = N_KV_HEADS * HEAD_DIM
REC_W = DN_HEADS * HEAD_DIM
LANES = 128
ROUTE_BLK = LANES
ROUTE_ALIGN = 16
ROUTE_WIN = ROUTE_BLK + ROUTE_ALIGN
N_LEVELS = 6
SCAN_BATCH = 2
VMEM_LIMIT = 56 * 1024 * 1024

C_Q, C_K, C_V, C_DN, C_HG, C_BA, C_END = 0, 512, 640, 768, 1792, 3072, 3200


def _mm(a, b):
    return jnp.dot(a.astype(BF16), b.astype(BF16), preferred_element_type=F32)


def _mm_tn(a, b):
    return lax.dot_general(a.astype(BF16), b.astype(BF16), (((0,), (0,)), ((), ())),
                           preferred_element_type=F32)


def _mm_hi(a, b):
    return jnp.dot(a, b, precision=HIGHEST, preferred_element_type=F32)


def _sigmoid(x):
    return 1.0 / (1.0 + jnp.exp(-x))


def _silu(x):
    return x * _sigmoid(x)


def _cparams(sem):
    return pltpu.CompilerParams(dimension_semantics=sem, vmem_limit_bytes=VMEM_LIMIT)


def _tile(n, pref):
    return pref if n % pref == 0 else n


def _mod_kernel(c_ref, w_ref, b_ref, o_ref):
    o_ref[0] = _mm_hi(_silu(c_ref[...]), w_ref[0]) + b_ref[0]


def _modulation(cond, w_mod, b_mod):
    depth, d, n = w_mod.shape
    rows = cond.shape[0]
    tn = _tile(n, 1536)
    return pl.pallas_call(
        _mod_kernel,
        grid=(depth, n // tn),
        in_specs=[pl.BlockSpec((rows, d), lambda l, j: (0, 0)),
                  pl.BlockSpec((1, d, tn), lambda l, j: (l, 0, j)),
                  pl.BlockSpec((1, 1, tn), lambda l, j: (l, 0, j))],
        out_specs=pl.BlockSpec((1, rows, tn), lambda l, j: (l, 0, j)),
        out_shape=jax.ShapeDtypeStruct((depth, rows, n), F32),
        compiler_params=_cparams(("parallel", "parallel")),
        name="modulation",
    )(cond, w_mod, b_mod.reshape(depth, 1, n))


def _inproj_kernel(x_ref, sh_ref, sc_ref, w_ref, qn_ref, kn_ref, cos_ref, sin_ref, g_ref,
                   q_ref, kt_ref, v_ref, dn_ref, hg_ref, ba_ref, *, rotate):
    x = x_ref[0]
    ms = jnp.mean(x * x, axis=-1, keepdims=True)
    h = x * lax.rsqrt(ms + EPS) * (1.0 + sc_ref[0]) + sh_ref[0]
    hb = h.astype(BF16)
    gmat = g_ref[...]

    def proj(lo, hi):
        return jnp.dot(hb, w_ref[:, lo:hi], preferred_element_type=F32)

    def head_norm_rope(t, nw):
        t = t * lax.rsqrt(_mm_hi(t * t, gmat) + EPS) * nw
        if rotate:
            lane = lax.broadcasted_iota(jnp.int32, t.shape, 1)
            first = (lane % 32) < 16
            rot = jnp.where(first, -pltpu.roll(t, LANES - 16, 1), pltpu.roll(t, 16, 1))
            t = t * cos_ref[...] + rot * sin_ref[...]
        return t

    for j in range(ATTN_Q_W // LANES):
        qj = head_norm_rope(proj(C_Q + j * LANES, C_Q + (j + 1) * LANES), qn_ref[...])
        q_ref[0, :, j * LANES:(j + 1) * LANES] = (qj * (HEAD_DIM ** -0.5 * LOG2E)).astype(BF16)
    kt = head_norm_rope(proj(C_K, C_V), kn_ref[...]).T
    kt_ref[0, 0] = kt[:HEAD_DIM].astype(BF16)
    kt_ref[0, 1] = kt[HEAD_DIM:].astype(BF16)
    v = proj(C_V, C_DN)
    ones = jnp.ones((v.shape[0], LANES - HEAD_DIM), F32)
    v_ref[0, 0] = jnp.concatenate([v[:, :HEAD_DIM], ones], axis=1).astype(BF16)
    v_ref[0, 1] = jnp.concatenate([v[:, HEAD_DIM:], ones], axis=1).astype(BF16)
    dn_ref[0] = proj(C_DN, C_HG)
    hg_ref[0] = proj(C_HG, C_BA)
    ba_ref[0] = proj(C_BA, C_END)


def _input_projection(x, shift, scale, w, qn, kn, cos, sin, gmat, rotate):
    b, t, d = x.shape
    tm = _tile(t, 512)
    per_b = shift.shape[0] > 1
    mod_map = (lambda bi, i: (bi, 0, 0)) if per_b else (lambda bi, i: (0, 0, 0))
    const = lambda bi, i: (0, 0)
    out_shape = (
        jax.ShapeDtypeStruct((b, t, ATTN_Q_W), BF16),
        jax.ShapeDtypeStruct((b, N_KV_HEADS, HEAD_DIM, t), BF16),
        jax.ShapeDtypeStruct((b, N_KV_HEADS, t, LANES), BF16),
        jax.ShapeDtypeStruct((b, t, C_HG - C_DN), F32),
        jax.ShapeDtypeStruct((b, t, C_BA - C_HG), F32),
        jax.ShapeDtypeStruct((b, t, C_END - C_BA), F32),
    )
    return pl.pallas_call(
        functools.partial(_inproj_kernel, rotate=rotate),
        grid=(b, t // tm),
        in_specs=[pl.BlockSpec((1, tm, d), lambda bi, i: (bi, i, 0)),
                  pl.BlockSpec((1, 1, d), mod_map),
                  pl.BlockSpec((1, 1, d), mod_map),
                  pl.BlockSpec((d, C_END), const),
                  pl.BlockSpec((1, LANES), const),
                  pl.BlockSpec((1, LANES), const),
                  pl.BlockSpec((tm, LANES), lambda bi, i: (i, 0)),
                  pl.BlockSpec((tm, LANES), lambda bi, i: (i, 0)),
                  pl.BlockSpec((LANES, LANES), const)],
        out_specs=(pl.BlockSpec((1, tm, ATTN_Q_W), lambda bi, i: (bi, i, 0)),
                   pl.BlockSpec((1, N_KV_HEADS, HEAD_DIM, tm), lambda bi, i: (bi, 0, 0, i)),
                   pl.BlockSpec((1, N_KV_HEADS, tm, LANES), lambda bi, i: (bi, 0, i, 0)),
                   pl.BlockSpec((1, tm, C_HG - C_DN), lambda bi, i: (bi, i, 0)),
                   pl.BlockSpec((1, tm, C_BA - C_HG), lambda bi, i: (bi, i, 0)),
                   pl.BlockSpec((1, tm, C_END - C_BA), lambda bi, i: (bi, i, 0))),
        out_shape=out_shape,
        compiler_params=_cparams(("parallel", "parallel")),
        name="input_projection",
    )(x, shift, scale, w, qn, kn, cos, sin, gmat)


def _lockstep(gens, stagger=False):
    results = [None] * len(gens)
    live, started = [], 0
    while live or started < len(gens):
        fresh = 1 if stagger else len(gens)
        live += list(range(started, min(started + fresh, len(gens))))
        started = min(started + fresh, len(gens))
        for i in reversed(list(live)):
            try:
                next(gens[i])
            except StopIteration as stop:
                results[i] = stop.value
                live.remove(i)
    return results


def _attn_head(qh, kts, vs):
    ss = [jnp.dot(qh, kt, preferred_element_type=F32) for kt in kts]
    yield
    m = functools.reduce(jnp.maximum, [jnp.max(s, axis=-1, keepdims=True) for s in ss])
    ps = [jnp.exp2((s - m).astype(BF16)) for s in ss]
    yield
    o = functools.reduce(jnp.add, [jnp.dot(p, v[0, 0], preferred_element_type=F32) for p, v in zip(ps, vs)])
    return o[:, :HEAD_DIM] / o[:, HEAD_DIM:HEAD_DIM + 1]


def _attn_kernel(*refs, n_src):
    q_ref, o_ref = refs[0], refs[-1]
    kts = [refs[1 + 2 * i][0, 0] for i in range(n_src)]
    vs = [refs[2 + 2 * i] for i in range(n_src)]
    q = q_ref[0]
    outs = _lockstep([_attn_head(q[:, h * HEAD_DIM:(h + 1) * HEAD_DIM], kts, vs) for h in range(GQA_GROUP)],
                     stagger=True)
    o_ref[0] = jnp.concatenate(outs, axis=-1).astype(BF16)


def _attention(q, sources):
    b, t, _ = q.shape
    tq = _tile(t, 256)
    gw = GQA_GROUP * HEAD_DIM
    in_specs = [pl.BlockSpec((1, tq, gw), lambda bi, g, i: (bi, i, g))]
    args = [q]
    for kt, v in sources:
        tk = kt.shape[-1]
        in_specs.append(pl.BlockSpec((1, 1, HEAD_DIM, tk), lambda bi, g, i: (bi, g, 0, 0)))
        in_specs.append(pl.BlockSpec((1, 1, tk, LANES), lambda bi, g, i: (bi, g, 0, 0)))
        args += [kt, v]
    return pl.pallas_call(
        functools.partial(_attn_kernel, n_src=len(sources)),
        grid=(b, N_KV_HEADS, t // tq),
        in_specs=in_specs,
        out_specs=pl.BlockSpec((1, tq, gw), lambda bi, g, i: (bi, i, g)),
        out_shape=jax.ShapeDtypeStruct((b, t, ATTN_Q_W), BF16),
        compiler_params=_cparams(("parallel", "parallel", "parallel")),
        name="attention",
    )(*args)


def _dn_prep_kernel(x_ref, xp_ref, xn_ref, ba_ref, cw_ref, na_ref, dtb_ref, g_ref, o_ref, gb_ref, buf):
    i = pl.program_id(1)
    n = pl.num_programs(1)
    tm = x_ref.shape[1]
    buf[0:8] = jnp.where(i > 0, xp_ref[0], 0.0)
    buf[8:8 + tm] = x_ref[0]
    buf[8 + tm:16 + tm] = jnp.where(i < n - 1, xn_ref[0], 0.0)
    half = CONV_K // 2
    y = cw_ref[0:1, :] * buf[8 - half:8 - half + tm]
    for j in range(1, CONV_K):
        y = y + cw_ref[j:j + 1, :] * buf[8 - half + j:8 - half + j + tm]
    y = _silu(y)
    gmat = g_ref[...]
    q = y[:, :REC_W]
    k = y[:, REC_W:2 * REC_W]
    o_ref[0, :, 0:REC_W] = q * lax.rsqrt(_mm_hi(q * q, gmat) + EPS) * HEAD_DIM ** -0.5
    o_ref[0, :, REC_W:2 * REC_W] = k * lax.rsqrt(_mm_hi(k * k, gmat) + EPS)
    o_ref[0, :, 2 * REC_W:] = y[:, 2 * REC_W:]
    ba = ba_ref[0]
    z = ba + dtb_ref[...]
    softplus = jnp.maximum(z, 0.0) + jnp.log1p(jnp.exp(-jnp.abs(z)))
    lane = lax.broadcasted_iota(jnp.int32, ba.shape, 1)
    gb_ref[0] = jnp.where(lane < 2 * DN_HEADS, _sigmoid(ba), na_ref[...] * softplus)


def _dn_prep(dn, ba, conv_w, neg_a, dt_bias, gsum):
    b, t, _ = dn.shape
    tm = _tile(t, 512)
    w3 = 3 * REC_W
    nb8 = t // 8
    return pl.pallas_call(
        _dn_prep_kernel,
        grid=(b, t // tm),
        in_specs=[pl.BlockSpec((1, tm, w3), lambda bi, i: (bi, i, 0)),
                  pl.BlockSpec((1, 8, w3), lambda bi, i: (bi, jnp.maximum(i * (tm // 8) - 1, 0), 0)),
                  pl.BlockSpec((1, 8, w3), lambda bi, i: (bi, jnp.minimum((i + 1) * (tm // 8), nb8 - 1), 0)),
                  pl.BlockSpec((1, tm, LANES), lambda bi, i: (bi, i, 0)),
                  pl.BlockSpec((8, w3), lambda bi, i: (0, 0)),
                  pl.BlockSpec((1, LANES), lambda bi, i: (0, 0)),
                  pl.BlockSpec((1, LANES), lambda bi, i: (0, 0)),
                  pl.BlockSpec((REC_W, REC_W), lambda bi, i: (0, 0))],
        out_specs=(pl.BlockSpec((1, tm, w3), lambda bi, i: (bi, i, 0)),
                   pl.BlockSpec((1, tm, LANES), lambda bi, i: (bi, i, 0))),
        out_shape=(jax.ShapeDtypeStruct((b, t, w3), F32), jax.ShapeDtypeStruct((b, t, LANES), F32)),
        scratch_shapes=[pltpu.VMEM((tm + 16, w3), F32)],
        compiler_params=_cparams(("parallel", "parallel")),
        name="deltanet_prep",
    )(dn, dn, dn, ba, conv_w, neg_a, dt_bias, gsum)


def _split2(a):
    hi = a.astype(BF16)
    return hi, (a - hi.astype(F32)).astype(BF16)


def _split3(a):
    hi = a.astype(BF16)
    r = a - hi.astype(F32)
    mid = r.astype(BF16)
    return hi, mid, (r - mid.astype(F32)).astype(BF16)


def _bdiag(a, hm):
    return jnp.concatenate([a] * DN_HEADS, axis=0) * hm


def _mm_bd(a, b, hm):
    return jnp.dot(a.astype(BF16), _bdiag(b.astype(BF16), hm), preferred_element_type=F32)


def _mm_bd3(a, b, hm):
    ah, al = _split2(a)
    bh, bl = _split2(b)
    t = jnp.dot(jnp.concatenate([ah, al], axis=0), _bdiag(bh, hm), preferred_element_type=F32)
    m = a.shape[0]
    return t[:m] + t[m:] + jnp.dot(ah, _bdiag(bl, hm), preferred_element_type=F32)


def _mm_sel_l(sel, b):
    n = b.shape[1]
    t = jnp.dot(sel.astype(BF16), jnp.concatenate(_split3(b), axis=1), preferred_element_type=F32)
    return t[:, :n] + t[:, n:2 * n] + t[:, 2 * n:]


def _mm_sel_r(a, sel):
    m = a.shape[0]
    t = jnp.dot(jnp.concatenate(_split3(a), axis=0), sel.astype(BF16), preferred_element_type=F32)
    return t[:m] + t[m:2 * m] + t[2 * m:]


def _dn_chunk(x, gb, s, ex, lmat, causal, lvl_ref, d, hm):
    n = REC_W
    lvl = lambda j: lvl_ref[d, j * CHUNK:(j + 1) * CHUNK, :]
    eye = lvl(N_LEVELS)
    ones8 = jnp.ones((8, CHUNK), F32)
    q, k, v = x[:, :n], x[:, n:2 * n], x[:, 2 * n:]
    gbx = _mm_sel_r(gb, ex)
    yield
    beta, g = gbx[:, :n], gbx[:, n:]
    cum = _mm_sel_l(lmat, g)
    clast = _mm_sel_l(ones8, g)[0:1]
    clast_rows = lax.dot_general(g, jnp.ones((CHUNK, n), F32), (((0,), (0,)), ((), ())), precision=HIGHEST,
                                 preferred_element_type=F32)
    kb = k * beta
    ks = _bdiag(k.astype(BF16), hm)
    kk = lax.dot_general(kb.astype(BF16), ks, (((1,), (1,)), ((), ())), preferred_element_type=F32)
    qk = lax.dot_general(q.astype(BF16), ks, (((1,), (1,)), ((), ())), preferred_element_type=F32)
    yield
    cum_s = _mm_sel_l(ones8, cum * eye)[0:1]
    ecum = jnp.exp(cum)
    kdec = k * jnp.exp(clast - cum)
    yield
    decay = jnp.where(causal, jnp.exp(jnp.where(causal, cum - cum_s, 0.0)), 0.0)
    m = kk * decay
    attn = qk * decay
    inv = eye - lvl(N_LEVELS - 1) * m
    for lev in range(N_LEVELS - 2, -1, -1):
        half = _mm_bd3(inv, lvl(lev) * m, hm)
        yield
        inv = inv - _mm_bd3(half, inv, hm)
        yield
    rhs = v * beta - _mm(kb * ecum, s)
    qs = _mm(q * ecum, s)
    yield
    v_new = _mm_bd3(inv, rhs, hm)
    yield
    out = qs + _mm_bd(attn, v_new, hm)
    s_new = s * jnp.exp(clast_rows) + _mm_tn(kdec, v_new) * hm.astype(F32)
    return out, s_new


def _hg_chunk(qr, fr, v, lb, s, amat, lvl_ref, d, hm):
    n = REC_W
    lvl = lambda j: lvl_ref[d, j * CHUNK:(j + 1) * CHUNK, :]
    q = _silu(qr)
    f = lb + (1.0 - lb) * _sigmoid(fr)
    k = 1.0 - f
    g = jnp.log(f)
    ex = jnp.exp(_mm_sel_l(amat, g))
    clast_rows = lax.dot_general(g, jnp.ones((CHUNK, n), F32), (((0,), (0,)), ((), ())), precision=HIGHEST,
                                 preferred_element_type=F32)
    yield

    def level(xl, mask):
        kl = _bdiag((k * xl).astype(BF16), hm)
        return lax.dot_general((q * xl).astype(BF16), kl, (((1,), (1,)), ((), ())),
                               preferred_element_type=F32) * mask

    attn = level(1.0, lvl(N_LEVELS))
    for lev in range(N_LEVELS):
        attn = attn + level(ex[(2 + lev) * CHUNK:(3 + lev) * CHUNK], lvl(lev))
    qs = _mm(q * ex[0:CHUNK], s)
    kv = _mm_tn(k * ex[CHUNK:2 * CHUNK], v)
    yield
    out = _mm_bd(attn, v, hm) + qs
    s_new = s * jnp.exp(clast_rows) + kv * hm.astype(F32)
    return out, s_new


def _dn_scan_kernel(xf_ref, xb_ref, gf_ref, gb_ref, s0_ref, ex_ref, l_ref, cz_ref, lvl_ref, hm_ref,
                    of_ref, ob_ref, sf_ref, s_scr, *, nchunk):
    @pl.when(pl.program_id(1) == 0)
    def _():
        s_scr[...] = s0_ref[...]

    hm = hm_ref[...]
    refs = ((xf_ref, gf_ref, of_ref), (xb_ref, gb_ref, ob_ref))

    def body(c, carry):
        rows = [pl.ds(pl.multiple_of(cc * CHUNK, CHUNK), CHUNK) for cc in (c, nchunk - 1 - c)]
        insts = [(bi, d) for bi in range(s_scr.shape[0]) for d in range(2)]
        res = _lockstep([_dn_chunk(refs[d][0][bi, rows[d], :], refs[d][1][bi, rows[d], :], s_scr[bi, d],
                                   ex_ref[d], l_ref[d], cz_ref[d] > 0, lvl_ref, d, hm) for bi, d in insts])
        for (bi, d), (out, s_new) in zip(insts, res):
            refs[d][2][bi, rows[d], :] = out
            s_scr[bi, d] = s_new
        return carry

    lax.fori_loop(0, nchunk, body, 0)
    sf_ref[...] = s_scr[...]


def _hg_scan_kernel(qf_ref, qb_ref, ff_ref, fb_ref, if_ref, ib_ref, lb_ref, s0_ref, a_ref, lvl_ref, hm_ref,
                    of_ref, ob_ref, sf_ref, s_scr, *, nchunk):
    @pl.when(pl.program_id(1) == 0)
    def _():
        s_scr[...] = s0_ref[...]

    hm = hm_ref[...]
    refs = ((qf_ref, ff_ref, if_ref, of_ref), (qb_ref, fb_ref, ib_ref, ob_ref))

    def body(c, carry):
        rows = [pl.ds(pl.multiple_of(cc * CHUNK, CHUNK), CHUNK) for cc in (c, nchunk - 1 - c)]
        insts = [(bi, d) for bi in range(s_scr.shape[0]) for d in range(2)]
        res = _lockstep([_hg_chunk(refs[d][0][bi, rows[d], :], refs[d][1][bi, rows[d], :],
                                   refs[d][2][bi, rows[d], :], lb_ref[d:d + 1, :], s_scr[bi, d], a_ref[d],
                                   lvl_ref, d, hm) for bi, d in insts])
        for (bi, d), (out, s_new) in zip(insts, res):
            refs[d][3][bi, rows[d], :] = out
            s_scr[bi, d] = s_new
        return carry

    lax.fori_loop(0, nchunk, body, 0)
    sf_ref[...] = s_scr[...]


def _dn_scan(qkv, gb, s0, consts):
    b, t, w3 = qkv.shape
    tb = _tile(t, 512)
    nblk = t // tb
    n = REC_W
    nb = _tile(b, SCAN_BATCH)
    fwd = lambda bi, i: (bi, i, 0)
    bwd = lambda bi, i: (bi, nblk - 1 - i, 0)
    c3 = lambda bi, i: (0, 0, 0)
    state = pl.BlockSpec((nb, 2, n, n), lambda bi, i: (bi, 0, 0, 0))
    return pl.pallas_call(
        functools.partial(_dn_scan_kernel, nchunk=tb // CHUNK),
        grid=(b // nb, nblk),
        in_specs=[pl.BlockSpec((nb, tb, w3), fwd), pl.BlockSpec((nb, tb, w3), bwd),
                  pl.BlockSpec((nb, tb, LANES), fwd), pl.BlockSpec((nb, tb, LANES), bwd),
                  state,
                  pl.BlockSpec((2, LANES, 2 * n), c3),
                  pl.BlockSpec((2, CHUNK, CHUNK), c3),
                  pl.BlockSpec((2, CHUNK, n), c3),
                  pl.BlockSpec((2, (N_LEVELS + 1) * CHUNK, n), c3),
                  pl.BlockSpec((n, n), lambda bi, i: (0, 0))],
        out_specs=(pl.BlockSpec((nb, tb, n), fwd), pl.BlockSpec((nb, tb, n), bwd), state),
        out_shape=(jax.ShapeDtypeStruct((b, t, n), F32), jax.ShapeDtypeStruct((b, t, n), F32),
                   jax.ShapeDtypeStruct((b, 2, n, n), F32)),
        scratch_shapes=[pltpu.VMEM((nb, 2, n, n), F32)],
        compiler_params=_cparams(("parallel", "arbitrary")),
        name="deltanet_scan",
    )(qkv, qkv, gb, gb, s0, consts["ex"], consts["l"], consts["cz"], consts["lvl"], consts["hm"])


def _hg_scan(hg, lb, s0, consts):
    b, t, _ = hg.shape
    tb = _tile(t, 512)
    nblk = t // tb
    n = REC_W
    nb = _tile(b, SCAN_BATCH)
    fwd = lambda col: (lambda bi, i: (bi, i, col))
    bwd = lambda col: (lambda bi, i: (bi, nblk - 1 - i, col))
    c3 = lambda bi, i: (0, 0, 0)
    blk = lambda m: pl.BlockSpec((nb, tb, n), m)
    state = pl.BlockSpec((nb, 2, n, n), lambda bi, i: (bi, 0, 0, 0))
    return pl.pallas_call(
        functools.partial(_hg_scan_kernel, nchunk=tb // CHUNK),
        grid=(b // nb, nblk),
        in_specs=[blk(fwd(0)), blk(bwd(0)), blk(fwd(1)), blk(bwd(2)), blk(fwd(3)), blk(bwd(3)),
                  pl.BlockSpec((2, n), lambda bi, i: (0, 0)),
                  state,
                  pl.BlockSpec((2, (2 + N_LEVELS) * CHUNK, CHUNK), c3),
                  pl.BlockSpec((2, (N_LEVELS + 1) * CHUNK, n), c3),
                  pl.BlockSpec((n, n), lambda bi, i: (0, 0))],
        out_specs=(blk(fwd(0)), blk(bwd(0)), state),
        out_shape=(jax.ShapeDtypeStruct((b, t, n), F32), jax.ShapeDtypeStruct((b, t, n), F32),
                   jax.ShapeDtypeStruct((b, 2, n, n), F32)),
        scratch_shapes=[pltpu.VMEM((nb, 2, n, n), F32)],
        compiler_params=_cparams(("parallel", "arbitrary")),
        name="hgrn2_scan",
    )(hg, hg, hg, hg, hg, hg, lb, s0, consts["a"], consts["lvl"], consts["hm"])


def _scan_constants_dir(rev):
    t = np.arange(CHUNK)
    p = (CHUNK - 1 - t) if rev else t
    pt, pu = p[:, None], p[None, :]
    causal = (pu <= pt).astype(np.float32)
    head = np.arange(REC_W) // CHUNK
    hm = (head[:, None] == head[None, :]).astype(np.float32)
    rows = [causal, (pu > pt).astype(np.float32)]
    masks = []
    for lev in range(N_LEVELS):
        blk = CHUNK >> (lev + 1)
        bound = (pt // (2 * blk)) * (2 * blk) + blk - 1
        right = (pt % (2 * blk)) >= blk
        rows.append(np.where(right, (pu > bound) & (pu <= pt), (pu > pt) & (pu <= bound)).astype(np.float32))
        masks.append(((pt // (2 * blk) == pu // (2 * blk)) & right & ((pu % (2 * blk)) < blk)).astype(np.float32))
    masks.append(np.eye(CHUNK, dtype=np.float32))
    d = 1 if rev else 0
    ex = np.zeros((LANES, 2 * REC_W), np.float32)
    for h in range(DN_HEADS):
        ex[d * DN_HEADS + h, h * CHUNK:(h + 1) * CHUNK] = 1.0
        ex[2 * DN_HEADS + d * DN_HEADS + h, REC_W + h * CHUNK:REC_W + (h + 1) * CHUNK] = 1.0
    return {
        "l": causal, "cz": np.tile(causal, (1, DN_HEADS)), "hm": hm, "ex": ex,
        "a": np.concatenate(rows, axis=0),
        "lvl": np.concatenate([np.tile(mk, (1, DN_HEADS)) for mk in masks], axis=0),
    }


def _scan_constants():
    fwd, bwd = _scan_constants_dir(False), _scan_constants_dir(True)
    out = {key: jnp.asarray(np.stack([fwd[key], bwd[key]])) for key in ("l", "cz", "ex", "a", "lvl")}
    out["hm"] = jnp.asarray(fwd["hm"], BF16)
    return out


def _outproj_kernel(a_ref, dof_ref, dob_ref, z_ref, gof_ref, gob_ref, gg_ref, dnw_ref, hgw_ref, g_ref,
                    w_ref, x_ref, gate_ref, sh_ref, sc_ref, rt_ref, x1_ref, h2_ref, aff_ref):
    gmat = g_ref[...]

    def gated(o, z, nw):
        return o * lax.rsqrt(_mm_hi(o * o, gmat) + EPS) * nw * _silu(z)

    dmix = gated(dof_ref[0] + dob_ref[0], z_ref[0], dnw_ref[...])
    gmix = gated(gof_ref[0] + gob_ref[0], gg_ref[0], hgw_ref[...])
    y = (jnp.dot(a_ref[0], w_ref[0:ATTN_Q_W, :], preferred_element_type=F32)
         + jnp.dot(dmix.astype(BF16), w_ref[ATTN_Q_W:ATTN_Q_W + REC_W, :], preferred_element_type=F32)
         + jnp.dot(gmix.astype(BF16), w_ref[ATTN_Q_W + REC_W:, :], preferred_element_type=F32))
    x1 = x_ref[0] + gate_ref[0] * y
    x1_ref[0] = x1
    ms = jnp.mean(x1 * x1, axis=-1, keepdims=True)
    h2 = x1 * lax.rsqrt(ms + EPS) * (1.0 + sc_ref[0]) + sh_ref[0]
    h2_ref[0] = h2.astype(BF16)
    logits = lax.dot_general(rt_ref[...], h2, (((1,), (1,)), ((), ())), precision=HIGHEST,
                             preferred_element_type=F32)
    e = jnp.exp(logits - jnp.max(logits, axis=0, keepdims=True))
    aff_ref[0] = e / jnp.sum(e, axis=0, keepdims=True)


def _output_projection(a, dof, dob, dn, gof, gob, hg, dnw, hgw, gavg, w_out, x, gate, shift, scale, router_t):
    b, t, d = x.shape
    tm = _tile(t, 512)
    n = REC_W
    per_b = gate.shape[0] > 1
    mod_map = (lambda bi, i: (bi, 0, 0)) if per_b else (lambda bi, i: (0, 0, 0))
    const = lambda bi, i: (0, 0)
    tok = lambda bi, i: (bi, i, 0)
    return pl.pallas_call(
        _outproj_kernel,
        grid=(b, t // tm),
        in_specs=[pl.BlockSpec((1, tm, ATTN_Q_W), tok),
                  pl.BlockSpec((1, tm, n), tok), pl.BlockSpec((1, tm, n), tok),
                  pl.BlockSpec((1, tm, n), lambda bi, i: (bi, i, 3)),
                  pl.BlockSpec((1, tm, n), tok), pl.BlockSpec((1, tm, n), tok),
                  pl.BlockSpec((1, tm, n), lambda bi, i: (bi, i, 4)),
                  pl.BlockSpec((1, n), const), pl.BlockSpec((1, n), const),
                  pl.BlockSpec((n, n), const),
                  pl.BlockSpec((d, d), const),
                  pl.BlockSpec((1, tm, d), tok),
                  pl.BlockSpec((1, 1, d), mod_map), pl.BlockSpec((1, 1, d), mod_map),
                  pl.BlockSpec((1, 1, d), mod_map),
                  pl.BlockSpec((N_EXPERTS, d), const)],
        out_specs=(pl.BlockSpec((1, tm, d), tok), pl.BlockSpec((1, tm, d), tok),
                   pl.BlockSpec((1, N_EXPERTS, tm), lambda bi, i: (bi, 0, i))),
        out_shape=(jax.ShapeDtypeStruct((b, t, d), F32), jax.ShapeDtypeStruct((b, t, d), BF16),
                   jax.ShapeDtypeStruct((b, N_EXPERTS, t), F32)),
        compiler_params=_cparams(("parallel", "parallel")),
        name="output_projection_router",
    )(a, dof, dob, dn, gof, gob, hg, dnw, hgw, gavg, w_out, x, gate, shift, scale, router_t)


def _select_kernel(aff_ref, u_ref, bs_ref, bst_ref, su_ref, slot_ref, base_ref, inc_scr, *, cap, nblk):
    x = aff_ref[0]
    bits = pltpu.bitcast(x, jnp.int32)
    lo = jnp.zeros((N_EXPERTS, 1), jnp.int32)
    for bit in range(30, -1, -1):
        cand = lo | (1 << bit)
        cnt = jnp.sum((bits >= cand).astype(jnp.int32), axis=1, keepdims=True)
        lo = jnp.where(cnt >= cap, cand, lo)
    gt = bits > lo
    eq = bits == lo
    umat = u_ref[...]

    def prefix(mask):
        mb = mask.astype(BF16)
        for j in range(nblk):
            inc_scr[:, j * LANES:(j + 1) * LANES] = jnp.dot(mb[:, j * LANES:(j + 1) * LANES], umat,
                                                            preferred_element_type=F32)
        totals = jnp.dot(mb, bs_ref[...], preferred_element_type=F32)
        offs = _mm_hi(totals, su_ref[...])
        return inc_scr[...] + _mm_hi(offs, bst_ref[...]), offs

    eqf = eq.astype(F32)
    n_gt = jnp.sum(gt.astype(F32), axis=1, keepdims=True)
    eq_before, _ = prefix(eqf)
    sel = gt | (eq & ((eq_before - eqf) < (cap - n_gt)))
    self_ = sel.astype(F32)
    pos, offs = prefix(self_)
    slot_ref[0] = jnp.where(sel, pos - 1.0, -1.0).astype(jnp.int32)
    base_ref[0] = offs.astype(jnp.int32)


def _moe_select(aff_t, cap):
    b, e, t = aff_t.shape
    nblk = t // ROUTE_BLK
    u = jnp.asarray(np.triu(np.ones((LANES, LANES), np.float32)), BF16)
    blk = np.arange(t) // ROUTE_BLK
    bs = (blk[:, None] == np.arange(nblk)[None, :]).astype(np.float32)
    su = np.triu(np.ones((nblk, nblk), np.float32), 1)
    const = lambda bi: (0, 0)
    return pl.pallas_call(
        functools.partial(_select_kernel, cap=cap, nblk=nblk),
        grid=(b,),
        in_specs=[pl.BlockSpec((1, e, t), lambda bi: (bi, 0, 0)),
                  pl.BlockSpec((LANES, LANES), const),
                  pl.BlockSpec((t, nblk), const),
                  pl.BlockSpec((nblk, t), const),
                  pl.BlockSpec((nblk, nblk), const)],
        out_specs=(pl.BlockSpec((1, e, t), lambda bi: (bi, 0, 0)),
                   pl.BlockSpec((1, e, nblk), lambda bi: (bi, 0, 0))),
        out_shape=(jax.ShapeDtypeStruct((b, e, t), jnp.int32), jax.ShapeDtypeStruct((b, e, nblk), jnp.int32)),
        scratch_shapes=[pltpu.VMEM((e, t), F32)],
        compiler_params=_cparams(("parallel",)),
        name="moe_select",
    )(aff_t, u, jnp.asarray(bs, BF16), jnp.asarray(bs.T), jnp.asarray(su))


def _gather_kernel(base_sm, slot_ref, h_ref, xe_ref, *, nblk):
    bi, ei = pl.program_id(0), pl.program_id(1)
    xe_ref[0, 0] = jnp.zeros(xe_ref.shape[2:], BF16)
    row = lax.broadcasted_iota(jnp.int32, (ROUTE_WIN, ROUTE_BLK), 0)

    def body(k, carry):
        base = base_sm[(bi * N_EXPERTS + ei) * nblk + k]
        bal = pl.multiple_of((base // ROUTE_ALIGN) * ROUTE_ALIGN, ROUTE_ALIGN)
        t0 = pl.multiple_of(k * ROUTE_BLK, ROUTE_BLK)
        srow = slot_ref[0, 0, :, pl.ds(t0, ROUTE_BLK)]
        onehot = jnp.where(row == srow - bal, 1.0, 0.0).astype(BF16)
        rows = jnp.dot(onehot, h_ref[0, pl.ds(t0, ROUTE_BLK), :], preferred_element_type=F32)
        xe_ref[0, 0, pl.ds(bal, ROUTE_WIN), :] = xe_ref[0, 0, pl.ds(bal, ROUTE_WIN), :] + rows.astype(BF16)
        return carry

    lax.fori_loop(0, nblk, body, 0)


def _moe_gather(h2, slot, base_flat, cap):
    b, t, d = h2.shape
    nblk = t // ROUTE_BLK
    cp = cap + ROUTE_WIN
    grid_spec = pltpu.PrefetchScalarGridSpec(
        num_scalar_prefetch=1,
        grid=(b, N_EXPERTS),
        in_specs=[pl.BlockSpec((1, 1, 1, t), lambda bi, ei, sm: (bi, ei, 0, 0)),
                  pl.BlockSpec((1, t, d), lambda bi, ei, sm: (bi, 0, 0))],
        out_specs=pl.BlockSpec((1, 1, cp, d), lambda bi, ei, sm: (bi, ei, 0, 0)),
    )
    return pl.pallas_call(
        functools.partial(_gather_kernel, nblk=nblk),
        grid_spec=grid_spec,
        out_shape=jax.ShapeDtypeStruct((b, N_EXPERTS, cp, d), BF16),
        compiler_params=_cparams(("parallel", "arbitrary")),
        name="moe_gather",
    )(base_flat, slot.reshape(b, N_EXPERTS, 1, t), h2)


def _ffn_kernel(x_ref, wg_ref, wu_ref, wd_ref, y_ref):
    x = x_ref[0, 0]
    a = jnp.dot(x, wg_ref[0], preferred_element_type=F32)
    u = jnp.dot(x, wu_ref[0], preferred_element_type=F32)
    y_ref[0, 0] = jnp.dot((_silu(a) * u).astype(BF16), wd_ref[0], preferred_element_type=F32).astype(BF16)


def _moe_ffn(xe, wg, wu, wd):
    b, e, cp, d = xe.shape
    f = wg.shape[-1]
    return pl.pallas_call(
        _ffn_kernel,
        grid=(e, b),
        in_specs=[pl.BlockSpec((1, 1, cp, d), lambda ei, bi: (bi, ei, 0, 0)),
                  pl.BlockSpec((1, d, f), lambda ei, bi: (ei, 0, 0)),
                  pl.BlockSpec((1, d, f), lambda ei, bi: (ei, 0, 0)),
                  pl.BlockSpec((1, f, d), lambda ei, bi: (ei, 0, 0))],
        out_specs=pl.BlockSpec((1, 1, cp, d), lambda ei, bi: (bi, ei, 0, 0)),
        out_shape=jax.ShapeDtypeStruct((b, e, cp, d), BF16),
        compiler_params=_cparams(("parallel", "parallel")),
        name="moe_ffn",
    )(xe, wg, wu, wd)


def _combine_kernel(base_sm, ye_ref, x_ref, gate_ref, slot_ref, aff_ref, o_ref, *, nblk):
    bi, k = pl.program_id(0), pl.program_id(2)
    lane = lax.broadcasted_iota(jnp.int32, (ROUTE_BLK, ROUTE_WIN), 1)
    slot = slot_ref[0]
    aff = aff_ref[0]
    acc = jnp.zeros(x_ref.shape[1:], F32)
    for e in range(N_EXPERTS):
        base = base_sm[(bi * N_EXPERTS + e) * nblk + k]
        bal = pl.multiple_of((base // ROUTE_ALIGN) * ROUTE_ALIGN, ROUTE_ALIGN)
        onehot = jnp.where(lane == slot[:, e:e + 1] - bal, 1.0, 0.0).astype(BF16)
        rows = jnp.dot(onehot, ye_ref[0, e, pl.ds(bal, ROUTE_WIN), :], preferred_element_type=F32)
        acc = acc + rows * aff[:, e:e + 1]
    o_ref[0] = x_ref[0] + gate_ref[0] * acc


def _moe_combine(ye, x1, gate, slot_tm, aff_tm, base_flat):
    b, t, d = x1.shape
    cp = ye.shape[2]
    nblk = t // ROUTE_BLK
    dh = d // 2
    per_b = gate.shape[0] > 1
    grid_spec = pltpu.PrefetchScalarGridSpec(
        num_scalar_prefetch=1,
        grid=(b, 2, nblk),
        in_specs=[pl.BlockSpec((1, N_EXPERTS, cp, dh), lambda bi, j, k, sm: (bi, 0, 0, j)),
                  pl.BlockSpec((1, ROUTE_BLK, dh), lambda bi, j, k, sm: (bi, k, j)),
                  pl.BlockSpec((1, 1, dh), (lambda bi, j, k, sm: (bi, 0, j)) if per_b
                               else (lambda bi, j, k, sm: (0, 0, j))),
                  pl.BlockSpec((1, ROUTE_BLK, N_EXPERTS), lambda bi, j, k, sm: (bi, k, 0)),
                  pl.BlockSpec((1, ROUTE_BLK, N_EXPERTS), lambda bi, j, k, sm: (bi, k, 0))],
        out_specs=pl.BlockSpec((1, ROUTE_BLK, dh), lambda bi, j, k, sm: (bi, k, j)),
    )
    return pl.pallas_call(
        functools.partial(_combine_kernel, nblk=nblk),
        grid_spec=grid_spec,
        out_shape=jax.ShapeDtypeStruct((b, t, d), F32),
        compiler_params=_cparams(("parallel", "parallel", "arbitrary")),
        name="moe_combine",
    )(base_flat, ye, x1, gate, slot_tm, aff_tm)


def _expert_choice_ffn(x1, h2, aff_t, gate, wg, wu, wd):
    b, t, _ = x1.shape
    cap = CAPACITY_FACTOR * t // N_EXPERTS
    slot, base = _moe_select(aff_t, cap)
    base_flat = base.reshape(-1)
    xe = _moe_gather(h2, slot, base_flat, cap)
    ye = _moe_ffn(xe, wg, wu, wd)
    return _moe_combine(ye, x1, gate, jnp.swapaxes(slot, 1, 2), jnp.swapaxes(aff_t, 1, 2), base_flat)


def _rope_tables(n_tokens):
    rows = n_tokens // GRID_W
    row = jnp.repeat(jnp.arange(rows, dtype=F32), GRID_W)
    col = jnp.tile(jnp.arange(GRID_W, dtype=F32), rows)
    n_freq = HEAD_DIM // 4
    inv_freq = ROPE_THETA ** (-jnp.arange(n_freq, dtype=F32) / n_freq)
    ang_r = row[:, None] * inv_freq
    ang_c = col[:, None] * inv_freq
    cos = jnp.concatenate([jnp.cos(ang_r)] * 2 + [jnp.cos(ang_c)] * 2, axis=-1)
    sin = jnp.concatenate([jnp.sin(ang_r)] * 2 + [jnp.sin(ang_c)] * 2, axis=-1)
    return jnp.tile(cos, (1, LANES // HEAD_DIM)), jnp.tile(sin, (1, LANES // HEAD_DIM))


def _block_diag(width, block, value):
    idx = np.arange(width) // block
    return jnp.asarray((idx[:, None] == idx[None, :]).astype(np.float32) * value)


def kernel(x, c, ctx, c_ctx, w_mod, b_mod, w_in, w_out, attn_q_norm, attn_k_norm, dn_conv, dn_a_log,
           dn_dt_bias, dn_norm, hg_lower_bounds, hg_norm, moe_router, moe_w_gate, moe_w_up, moe_w_down):
    depth = w_mod.shape[0]
    b, t_lat, d = x.shape
    cos, sin = _rope_tables(t_lat)
    g_head = _block_diag(LANES, HEAD_DIM, 1.0 / HEAD_DIM)
    g_mean = _block_diag(REC_W, HEAD_DIM, 1.0 / HEAD_DIM)
    g_sum = _block_diag(REC_W, HEAD_DIM, 1.0)
    consts = _scan_constants()
    s_zero = jnp.zeros((b, 2, REC_W, REC_W), F32)

    lb_w = jax.nn.softmax(hg_lower_bounds.astype(F32), axis=0)
    hg_lb = jnp.cumsum(lb_w, axis=0) - lb_w[0]

    rows = ((b + 1 + 7) // 8) * 8
    cond = jnp.zeros((rows, d), F32).at[:b].set(c).at[b].set(c_ctx)
    mod = _modulation(cond, w_mod, b_mod)

    n_small = 4 * DN_HEADS
    w_in_r = jnp.concatenate(
        [w_in[:, :, :C_HG], w_in[:, :, C_HG + n_small:], w_in[:, :, C_HG:C_HG + n_small],
         jnp.zeros((depth, d, C_END - C_BA - n_small), w_in.dtype)], axis=-1).astype(BF16)
    w_out_b = w_out.astype(BF16)
    wg_b, wu_b, wd_b = moe_w_gate.astype(BF16), moe_w_up.astype(BF16), moe_w_down.astype(BF16)

    x_lat, x_ctx = x, ctx
    for l in range(depth):
        ctx_out = l < depth - 1
        m_lat = [mod[l, :b, j * d:(j + 1) * d][:, None, :] for j in range(6)]
        m_ctx = [mod[l, b:b + 1, j * d:(j + 1) * d][:, None, :] for j in range(6)]
        qn = jnp.tile(attn_q_norm[l], LANES // HEAD_DIM)[None, :]
        kn = jnp.tile(attn_k_norm[l], LANES // HEAD_DIM)[None, :]
        conv_w = jnp.zeros((8, 3 * REC_W), F32).at[:CONV_K].set(dn_conv[l])
        pad = jnp.zeros((LANES - 4 * DN_HEADS,), F32)
        neg_a = jnp.concatenate([jnp.zeros((2 * DN_HEADS,), F32), -jnp.exp(dn_a_log[l].reshape(-1)), pad])[None, :]
        dt_b = jnp.concatenate([jnp.zeros((2 * DN_HEADS,), F32), dn_dt_bias[l].reshape(-1), pad])[None, :]
        dnw = jnp.tile(dn_norm[l], DN_HEADS)[None, :]
        hgw = jnp.tile(hg_norm[l], HG_HEADS)[None, :]
        router_t = moe_router[l].T

        streams = {}
        for name, xs, ms, rotate in (("ctx", x_ctx, m_ctx, False), ("lat", x_lat, m_lat, True)):
            t = xs.shape[1]
            q, kt, v, dn, hg, ba = _input_projection(xs, ms[0], ms[1], w_in_r[l], qn, kn,
                                                     cos[:t], sin[:t], g_head, rotate)
            qkv, gb = _dn_prep(dn, ba, conv_w, neg_a, dt_b, g_sum)
            streams[name] = dict(q=q, kt=kt, v=v, dn=dn, hg=hg, qkv=qkv, gb=gb)

        sc, sl = streams["ctx"], streams["lat"]
        dcf, dcb, dn_state = _dn_scan(sc["qkv"], sc["gb"], s_zero, consts)
        dlf, dlb, _ = _dn_scan(sl["qkv"], sl["gb"], dn_state, consts)
        gcf, gcb, hg_state = _hg_scan(sc["hg"], hg_lb[l], s_zero, consts)
        glf, glb, _ = _hg_scan(sl["hg"], hg_lb[l], hg_state, consts)

        a_lat = _attention(sl["q"], [(sc["kt"], sc["v"]), (sl["kt"], sl["v"])])
        x1, h2, aff_t = _output_projection(
            a_lat, dlf, dlb, sl["dn"], glf, glb, sl["hg"], dnw, hgw, g_mean, w_out_b[l], x_lat,
            m_lat[2], m_lat[3], m_lat[4], router_t)
        x_lat = _expert_choice_ffn(x1, h2, aff_t, m_lat[5], wg_b[l], wu_b[l], wd_b[l])
        if ctx_out:
            a_ctx = _attention(sc["q"], [(sc["kt"], sc["v"])])
            x1, h2, aff_t = _output_projection(
                a_ctx, dcf, dcb, sc["dn"], gcf, gcb, sc["hg"], dnw, hgw, g_mean, w_out_b[l], x_ctx,
                m_ctx[2], m_ctx[3], m_ctx[4], router_t)
            x_ctx = _expert_choice_ffn(x1, h2, aff_t, m_ctx[5], wg_b[l], wu_b[l], wd_b[l])
    return x_lat
```

```python
import functools

import numpy as np
import jax
import jax.numpy as jnp
from jax import lax
from jax.experimental import pallas as pl
from jax.experimental.pallas import tpu as pltpu

F32 = jnp.float32
BF16 = jnp.bfloat16
HIGHEST = lax.Precision.HIGHEST

HEAD_DIM = 64
N_Q_HEADS = 8
N_KV_HEADS = 2
GQA_GROUP = N_Q_HEADS // N_KV_HEADS
DN_HEADS = 4
HG_HEADS = 4
GRID_W = 64
ROPE_THETA = 10000.0
CONV_K = 5
CHUNK = 64
N_EXPERTS = 16
CAPACITY_FACTOR = 2
EPS = 1e-6
LOG2E = 1.4426950408889634

ATTN_Q_W = N_Q_HEADS * HEAD_DIM
ATTN_KV_W = N_KV_HEADS * HEAD_DIM
REC_W = DN_HEADS * HEAD_DIM
LANES = 128
ROUTE_BLK = LANES
ROUTE_ALIGN = 16
ROUTE_WIN = ROUTE_BLK + ROUTE_ALIGN
ROUTE_SUB = ROUTE_WIN // 3
N_LEVELS = 6
SCAN_BATCH = 4
VMEM_LIMIT = 56 * 1024 * 1024

C_Q, C_K, C_V, C_DN, C_HG, C_BA, C_END = 0, 512, 640, 768, 1792, 3072, 3200


def _mm(a, b):
    return jnp.dot(a.astype(BF16), b.astype(BF16), preferred_element_type=F32)


def _mm_tn(a, b):
    return lax.dot_general(a.astype(BF16), b.astype(BF16), (((0,), (0,)), ((), ())),
                           preferred_element_type=F32)


def _mm_hi(a, b):
    return jnp.dot(a, b, precision=HIGHEST, preferred_element_type=F32)


def _sigmoid(x):
    return 1.0 / (1.0 + jnp.exp(-x))


def _silu(x):
    return x * _sigmoid(x)


def _cparams(sem):
    return pltpu.CompilerParams(dimension_semantics=sem, vmem_limit_bytes=VMEM_LIMIT)


def _tile(n, pref):
    return pref if n % pref == 0 else n


def _mod_kernel(c_ref, w_ref, b_ref, o_ref):
    o_ref[0] = _mm_hi(_silu(c_ref[...]), w_ref[0]) + b_ref[0]


def _modulation(cond, w_mod, b_mod):
    depth, d, n = w_mod.shape
    rows = cond.shape[0]
    tn = _tile(n, 1536)
    return pl.pallas_call(
        _mod_kernel,
        grid=(depth, n // tn),
        in_specs=[pl.BlockSpec((rows, d), lambda l, j: (0, 0)),
                  pl.BlockSpec((1, d, tn), lambda l, j: (l, 0, j)),
                  pl.BlockSpec((1, 1, tn), lambda l, j: (l, 0, j))],
        out_specs=pl.BlockSpec((1, rows, tn), lambda l, j: (l, 0, j)),
        out_shape=jax.ShapeDtypeStruct((depth, rows, n), F32),
        compiler_params=_cparams(("parallel", "parallel")),
        name="modulation",
    )(cond, w_mod, b_mod.reshape(depth, 1, n))


def _inproj_kernel(x_ref, sh_ref, sc_ref, w_ref, qn_ref, kn_ref, cos_ref, sin_ref, g_ref,
                   q_ref, k_ref, v_ref, dn_ref, hg_ref, ba_ref, *, rotate):
    x = x_ref[0]
    ms = jnp.mean(x * x, axis=-1, keepdims=True)
    h = x * lax.rsqrt(ms + EPS) * (1.0 + sc_ref[0]) + sh_ref[0]
    hb = h.astype(BF16)
    gmat = g_ref[...]

    def proj(lo, hi):
        return jnp.dot(hb, w_ref[:, lo:hi], preferred_element_type=F32)

    def head_norm_rope(t, nw):
        t = t * lax.rsqrt(_mm_hi(t * t, gmat) + EPS) * nw
        if rotate:
            lane = lax.broadcasted_iota(jnp.int32, t.shape, 1)
            first = (lane % 32) < 16
            rot = jnp.where(first, -pltpu.roll(t, LANES - 16, 1), pltpu.roll(t, 16, 1))
            t = t * cos_ref[...] + rot * sin_ref[...]
        return t

    for j in range(ATTN_Q_W // LANES):
        qj = head_norm_rope(proj(C_Q + j * LANES, C_Q + (j + 1) * LANES), qn_ref[...])
        q_ref[0, :, j * LANES:(j + 1) * LANES] = (qj * (HEAD_DIM ** -0.5 * LOG2E)).astype(BF16)
    kt = head_norm_rope(proj(C_K, C_V), kn_ref[...]).T
    k_ref[0, 0] = kt[:HEAD_DIM].astype(BF16)
    k_ref[0, 1] = kt[HEAD_DIM:].astype(BF16)
    v = proj(C_V, C_DN)
    ones = jnp.ones((v.shape[0], LANES - HEAD_DIM), F32)
    v_ref[0, 0] = jnp.concatenate([v[:, :HEAD_DIM], ones], axis=1).astype(BF16)
    v_ref[0, 1] = jnp.concatenate([v[:, HEAD_DIM:], ones], axis=1).astype(BF16)
    dn_ref[0] = proj(C_DN, C_HG)
    hg_ref[0] = proj(C_HG, C_BA)
    ba_ref[0] = proj(C_BA, C_END)


def _input_projection(x, shift, scale, w, qn, kn, cos, sin, gmat, rotate):
    b, t, d = x.shape
    tm = _tile(t, 512)
    per_b = shift.shape[0] > 1
    mod_map = (lambda bi, i: (bi, 0, 0)) if per_b else (lambda bi, i: (0, 0, 0))
    const = lambda bi, i: (0, 0)
    out_shape = (
        jax.ShapeDtypeStruct((b, t, ATTN_Q_W), BF16),
        jax.ShapeDtypeStruct((b, N_KV_HEADS, HEAD_DIM, t), BF16),
        jax.ShapeDtypeStruct((b, N_KV_HEADS, t, LANES), BF16),
        jax.ShapeDtypeStruct((b, t, C_HG - C_DN), F32),
        jax.ShapeDtypeStruct((b, t, C_BA - C_HG), F32),
        jax.ShapeDtypeStruct((b, t, C_END - C_BA), F32),
    )
    return pl.pallas_call(
        functools.partial(_inproj_kernel, rotate=rotate),
        grid=(b, t // tm),
        in_specs=[pl.BlockSpec((1, tm, d), lambda bi, i: (bi, i, 0)),
                  pl.BlockSpec((1, 1, d), mod_map),
                  pl.BlockSpec((1, 1, d), mod_map),
                  pl.BlockSpec((d, C_END), const),
                  pl.BlockSpec((1, LANES), const),
                  pl.BlockSpec((1, LANES), const),
                  pl.BlockSpec((tm, LANES), lambda bi, i: (i, 0)),
                  pl.BlockSpec((tm, LANES), lambda bi, i: (i, 0)),
                  pl.BlockSpec((LANES, LANES), const)],
        out_specs=(pl.BlockSpec((1, tm, ATTN_Q_W), lambda bi, i: (bi, i, 0)),
                   pl.BlockSpec((1, N_KV_HEADS, HEAD_DIM, tm), lambda bi, i: (bi, 0, 0, i)),
                   pl.BlockSpec((1, N_KV_HEADS, tm, LANES), lambda bi, i: (bi, 0, i, 0)),
                   pl.BlockSpec((1, tm, C_HG - C_DN), lambda bi, i: (bi, i, 0)),
                   pl.BlockSpec((1, tm, C_BA - C_HG), lambda bi, i: (bi, i, 0)),
                   pl.BlockSpec((1, tm, C_END - C_BA), lambda bi, i: (bi, i, 0))),
        out_shape=out_shape,
        compiler_params=_cparams(("parallel", "parallel")),
        name="input_projection",
    )(x, shift, scale, w, qn, kn, cos, sin, gmat)


def _lockstep(gens, stagger=False):
    results = [None] * len(gens)
    live, started = [], 0
    while live or started < len(gens):
        fresh = 1 if stagger else len(gens)
        live += list(range(started, min(started + fresh, len(gens))))
        started = min(started + fresh, len(gens))
        for i in reversed(list(live)):
            try:
                next(gens[i])
            except StopIteration as stop:
                results[i] = stop.value
                live.remove(i)
    return results


def _attn_head(qh, kts, vs):
    ss = [jnp.dot(qh, kt, preferred_element_type=F32) for kt in kts]
    yield
    m = functools.reduce(jnp.maximum, [jnp.max(s, axis=-1, keepdims=True) for s in ss])
    ps = [jnp.exp2((s - m).astype(BF16)) for s in ss]
    yield
    o = functools.reduce(jnp.add, [jnp.dot(p, v[0, 0], preferred_element_type=F32) for p, v in zip(ps, vs)])
    return o[:, :HEAD_DIM] / o[:, HEAD_DIM:HEAD_DIM + 1]


def _attn_kernel(*refs, n_src):
    q_ref, o_ref = refs[0], refs[-1]
    kts = [refs[1 + 2 * i][0, 0] for i in range(n_src)]
    vs = [refs[2 + 2 * i] for i in range(n_src)]
    q = q_ref[0]
    outs = _lockstep([_attn_head(q[:, h * HEAD_DIM:(h + 1) * HEAD_DIM], kts, vs) for h in range(GQA_GROUP)],
                     stagger=True)
    o_ref[0] = jnp.concatenate(outs, axis=-1).astype(BF16)


def _attention(q, sources):
    b, t, _ = q.shape
    tq = _tile(t, 256)
    gw = GQA_GROUP * HEAD_DIM
    in_specs = [pl.BlockSpec((1, tq, gw), lambda bi, g, i: (bi, i, g))]
    args = [q]
    for kt, v in sources:
        tk = kt.shape[-1]
        in_specs.append(pl.BlockSpec((1, 1, HEAD_DIM, tk), lambda bi, g, i: (bi, g, 0, 0)))
        in_specs.append(pl.BlockSpec((1, 1, tk, LANES), lambda bi, g, i: (bi, g, 0, 0)))
        args += [kt, v]
    return pl.pallas_call(
        functools.partial(_attn_kernel, n_src=len(sources)),
        grid=(b, N_KV_HEADS, t // tq),
        in_specs=in_specs,
        out_specs=pl.BlockSpec((1, tq, gw), lambda bi, g, i: (bi, i, g)),
        out_shape=jax.ShapeDtypeStruct((b, t, ATTN_Q_W), BF16),
        compiler_params=_cparams(("parallel", "parallel", "parallel")),
        name="attention",
    )(*args)


def _dn_prep_kernel(x_ref, xp_ref, xn_ref, ba_ref, cw_ref, na_ref, dtb_ref, g_ref, o_ref, gb_ref, buf):
    i = pl.program_id(1)
    n = pl.num_programs(1)
    tm = x_ref.shape[1]
    buf[0:8] = jnp.where(i > 0, xp_ref[0], 0.0)
    buf[8:8 + tm] = x_ref[0]
    buf[8 + tm:16 + tm] = jnp.where(i < n - 1, xn_ref[0], 0.0)
    half = CONV_K // 2
    y = cw_ref[0:1, :] * buf[8 - half:8 - half + tm]
    for j in range(1, CONV_K):
        y = y + cw_ref[j:j + 1, :] * buf[8 - half + j:8 - half + j + tm]
    y = _silu(y)
    gmat = g_ref[...]
    q = y[:, :REC_W]
    k = y[:, REC_W:2 * REC_W]
    o_ref[0, :, 0:REC_W] = q * lax.rsqrt(_mm_hi(q * q, gmat) + EPS) * HEAD_DIM ** -0.5
    o_ref[0, :, REC_W:2 * REC_W] = k * lax.rsqrt(_mm_hi(k * k, gmat) + EPS)
    o_ref[0, :, 2 * REC_W:] = y[:, 2 * REC_W:]
    ba = ba_ref[0]
    z = ba + dtb_ref[...]
    softplus = jnp.maximum(z, 0.0) + jnp.log1p(jnp.exp(-jnp.abs(z)))
    lane = lax.broadcasted_iota(jnp.int32, ba.shape, 1)
    gb_ref[0] = jnp.where(lane < 2 * DN_HEADS, _sigmoid(ba), na_ref[...] * softplus)


def _dn_prep(dn, ba, conv_w, neg_a, dt_bias, gsum):
    b, t, _ = dn.shape
    tm = _tile(t, 512)
    w3 = 3 * REC_W
    nb8 = t // 8
    return pl.pallas_call(
        _dn_prep_kernel,
        grid=(b, t // tm),
        in_specs=[pl.BlockSpec((1, tm, w3), lambda bi, i: (bi, i, 0)),
                  pl.BlockSpec((1, 8, w3), lambda bi, i: (bi, jnp.maximum(i * (tm // 8) - 1, 0), 0)),
                  pl.BlockSpec((1, 8, w3), lambda bi, i: (bi, jnp.minimum((i + 1) * (tm // 8), nb8 - 1), 0)),
                  pl.BlockSpec((1, tm, LANES), lambda bi, i: (bi, i, 0)),
                  pl.BlockSpec((8, w3), lambda bi, i: (0, 0)),
                  pl.BlockSpec((1, LANES), lambda bi, i: (0, 0)),
                  pl.BlockSpec((1, LANES), lambda bi, i: (0, 0)),
                  pl.BlockSpec((REC_W, REC_W), lambda bi, i: (0, 0))],
        out_specs=(pl.BlockSpec((1, tm, w3), lambda bi, i: (bi, i, 0)),
                   pl.BlockSpec((1, tm, LANES), lambda bi, i: (bi, i, 0))),
        out_shape=(jax.ShapeDtypeStruct((b, t, w3), F32), jax.ShapeDtypeStruct((b, t, LANES), F32)),
        scratch_shapes=[pltpu.VMEM((tm + 16, w3), F32)],
        compiler_params=_cparams(("parallel", "parallel")),
        name="deltanet_prep",
    )(dn, dn, dn, ba, conv_w, neg_a, dt_bias, gsum)


def _split2(a):
    hi = a.astype(BF16)
    return hi, (a - hi.astype(F32)).astype(BF16)


def _split3(a):
    hi = a.astype(BF16)
    r = a - hi.astype(F32)
    mid = r.astype(BF16)
    return hi, mid, (r - mid.astype(F32)).astype(BF16)


def _bdiag(a, hm):
    return jnp.concatenate([a] * DN_HEADS, axis=0) * hm


def _mm_bd(a, b, hm):
    return jnp.dot(a.astype(BF16), _bdiag(b.astype(BF16), hm), preferred_element_type=F32)


def _mm_bd3(a, b, hm):
    ah, al = _split2(a)
    bh, bl = _split2(b)
    t = jnp.dot(jnp.concatenate([ah, al], axis=0), _bdiag(bh, hm), preferred_element_type=F32)
    m = a.shape[0]
    return t[:m] + t[m:] + jnp.dot(ah, _bdiag(bl, hm), preferred_element_type=F32)


def _mm_sel_l(sel, b):
    n = b.shape[1]
    t = jnp.dot(sel.astype(BF16), jnp.concatenate(_split3(b), axis=1), preferred_element_type=F32)
    return t[:, :n] + t[:, n:2 * n] + t[:, 2 * n:]


def _mm_sel_r(a, sel):
    m = a.shape[0]
    t = jnp.dot(jnp.concatenate(_split3(a), axis=0), sel.astype(BF16), preferred_element_type=F32)
    return t[:m] + t[m:2 * m] + t[2 * m:]


def _dn_chunk(x, gb, s, ex, lmat, causal, lvl_ref, d, hm):
    n = REC_W
    lvl = lambda j: lvl_ref[d, j * CHUNK:(j + 1) * CHUNK, :]
    eye = lvl(N_LEVELS)
    ones8 = jnp.ones((8, CHUNK), F32)
    q, k, v = x[:, :n], x[:, n:2 * n], x[:, 2 * n:]
    gbx = _mm_sel_r(gb, ex)
    yield
    beta, g = gbx[:, :n], gbx[:, n:]
    cum = _mm_sel_l(lmat, g)
    clast = _mm_sel_l(ones8, g)[0:1]
    clast_rows = lax.dot_general(g, jnp.ones((CHUNK, n), F32), (((0,), (0,)), ((), ())), precision=HIGHEST,
                                 preferred_element_type=F32)
    kb = k * beta
    ks = _bdiag(k.astype(BF16), hm)
    kk = lax.dot_general(kb.astype(BF16), ks, (((1,), (1,)), ((), ())), preferred_element_type=F32)
    qk = lax.dot_general(q.astype(BF16), ks, (((1,), (1,)), ((), ())), preferred_element_type=F32)
    yield
    cum_s = _mm_sel_l(ones8, cum * eye)[0:1]
    ecum = jnp.exp(cum)
    kdec = k * jnp.exp(clast - cum)
    yield
    decay = jnp.where(causal, jnp.exp(jnp.where(causal, cum - cum_s, 0.0)), 0.0)
    m = kk * decay
    attn = qk * decay
    inv = eye - lvl(N_LEVELS - 1) * m
    for lev in range(N_LEVELS - 2, -1, -1):
        half = _mm_bd3(inv, lvl(lev) * m, hm)
        yield
        inv = inv - _mm_bd3(half, inv, hm)
        yield
    rhs = v * beta - _mm(kb * ecum, s)
    qs = _mm(q * ecum, s)
    yield
    v_new = _mm_bd3(inv, rhs, hm)
    yield
    out = qs + _mm_bd(attn, v_new, hm)
    s_new = s * jnp.exp(clast_rows) + _mm_tn(kdec, v_new) * hm.astype(F32)
    return out, s_new


def _hg_chunk(qr, fr, v, lb, s, amat, lvl_ref, d, hm):
    n = REC_W
    lvl = lambda j: lvl_ref[d, j * CHUNK:(j + 1) * CHUNK, :]
    q = _silu(qr)
    f = lb + (1.0 - lb) * _sigmoid(fr)
    k = 1.0 - f
    g = jnp.log(f)
    ex = jnp.exp(_mm_sel_l(amat, g))
    clast_rows = lax.dot_general(g, jnp.ones((CHUNK, n), F32), (((0,), (0,)), ((), ())), precision=HIGHEST,
                                 preferred_element_type=F32)
    yield

    def level(xl, mask):
        kl = _bdiag((k * xl).astype(BF16), hm)
        return lax.dot_general((q * xl).astype(BF16), kl, (((1,), (1,)), ((), ())),
                               preferred_element_type=F32) * mask

    attn = level(1.0, lvl(N_LEVELS))
    for lev in range(N_LEVELS):
        attn = attn + level(ex[(2 + lev) * CHUNK:(3 + lev) * CHUNK], lvl(lev))
    qs = _mm(q * ex[0:CHUNK], s)
    kv = _mm_tn(k * ex[CHUNK:2 * CHUNK], v)
    yield
    out = _mm_bd(attn, v, hm) + qs
    s_new = s * jnp.exp(clast_rows) + kv * hm.astype(F32)
    return out, s_new


def _dn_scan_kernel(xf_ref, xb_ref, gf_ref, gb_ref, s0_ref, ex_ref, l_ref, cz_ref, lvl_ref, hm_ref,
                    of_ref, ob_ref, sf_ref, s_scr, *, nchunk):
    @pl.when(pl.program_id(1) == 0)
    def _():
        s_scr[...] = s0_ref[...]

    hm = hm_ref[...]
    refs = ((xf_ref, gf_ref, of_ref), (xb_ref, gb_ref, ob_ref))

    def body(c, carry):
        rows = [pl.ds(pl.multiple_of(cc * CHUNK, CHUNK), CHUNK) for cc in (c, nchunk - 1 - c)]
        insts = [(bi, d) for bi in range(s_scr.shape[0]) for d in range(2)]
        res = _lockstep([_dn_chunk(refs[d][0][bi, rows[d], :], refs[d][1][bi, rows[d], :], s_scr[bi, d],
                                   ex_ref[d], l_ref[d], cz_ref[d] > 0, lvl_ref, d, hm) for bi, d in insts])
        for (bi, d), (out, s_new) in zip(insts, res):
            refs[d][2][bi, rows[d], :] = out
            s_scr[bi, d] = s_new
        return carry

    lax.fori_loop(0, nchunk, body, 0)
    sf_ref[...] = s_scr[...]


def _hg_scan_kernel(qf_ref, qb_ref, ff_ref, fb_ref, if_ref, ib_ref, lb_ref, s0_ref, a_ref, lvl_ref, hm_ref,
                    of_ref, ob_ref, sf_ref, s_scr, *, nchunk):
    @pl.when(pl.program_id(1) == 0)
    def _():
        s_scr[...] = s0_ref[...]

    hm = hm_ref[...]
    refs = ((qf_ref, ff_ref, if_ref, of_ref), (qb_ref, fb_ref, ib_ref, ob_ref))

    def body(c, carry):
        rows = [pl.ds(pl.multiple_of(cc * CHUNK, CHUNK), CHUNK) for cc in (c, nchunk - 1 - c)]
        insts = [(bi, d) for bi in range(s_scr.shape[0]) for d in range(2)]
        res = _lockstep([_hg_chunk(refs[d][0][bi, rows[d], :], refs[d][1][bi, rows[d], :],
                                   refs[d][2][bi, rows[d], :], lb_ref[d:d + 1, :], s_scr[bi, d], a_ref[d],
                                   lvl_ref, d, hm) for bi, d in insts])
        for (bi, d), (out, s_new) in zip(insts, res):
            refs[d][3][bi, rows[d], :] = out
            s_scr[bi, d] = s_new
        return carry

    lax.fori_loop(0, nchunk, body, 0)
    sf_ref[...] = s_scr[...]


def _dn_scan(qkv, gb, s0, consts):
    b, t, w3 = qkv.shape
    tb = _tile(t, 512)
    nblk = t // tb
    n = REC_W
    nb = _tile(b, SCAN_BATCH)
    fwd = lambda bi, i: (bi, i, 0)
    bwd = lambda bi, i: (bi, nblk - 1 - i, 0)
    c3 = lambda bi, i: (0, 0, 0)
    state = pl.BlockSpec((nb, 2, n, n), lambda bi, i: (bi, 0, 0, 0))
    return pl.pallas_call(
        functools.partial(_dn_scan_kernel, nchunk=tb // CHUNK),
        grid=(b // nb, nblk),
        in_specs=[pl.BlockSpec((nb, tb, w3), fwd), pl.BlockSpec((nb, tb, w3), bwd),
                  pl.BlockSpec((nb, tb, LANES), fwd), pl.BlockSpec((nb, tb, LANES), bwd),
                  state,
                  pl.BlockSpec((2, LANES, 2 * n), c3),
                  pl.BlockSpec((2, CHUNK, CHUNK), c3),
                  pl.BlockSpec((2, CHUNK, n), c3),
                  pl.BlockSpec((2, (N_LEVELS + 1) * CHUNK, n), c3),
                  pl.BlockSpec((n, n), lambda bi, i: (0, 0))],
        out_specs=(pl.BlockSpec((nb, tb, n), fwd), pl.BlockSpec((nb, tb, n), bwd), state),
        out_shape=(jax.ShapeDtypeStruct((b, t, n), F32), jax.ShapeDtypeStruct((b, t, n), F32),
                   jax.ShapeDtypeStruct((b, 2, n, n), F32)),
        scratch_shapes=[pltpu.VMEM((nb, 2, n, n), F32)],
        compiler_params=_cparams(("parallel", "arbitrary")),
        name="deltanet_scan",
    )(qkv, qkv, gb, gb, s0, consts["ex"], consts["l"], consts["cz"], consts["lvl"], consts["hm"])


def _hg_scan(hg, lb, s0, consts):
    b, t, _ = hg.shape
    tb = _tile(t, 512)
    nblk = t // tb
    n = REC_W
    nb = _tile(b, SCAN_BATCH)
    fwd = lambda col: (lambda bi, i: (bi, i, col))
    bwd = lambda col: (lambda bi, i: (bi, nblk - 1 - i, col))
    c3 = lambda bi, i: (0, 0, 0)
    blk = lambda m: pl.BlockSpec((nb, tb, n), m)
    state = pl.BlockSpec((nb, 2, n, n), lambda bi, i: (bi, 0, 0, 0))
    return pl.pallas_call(
        functools.partial(_hg_scan_kernel, nchunk=tb // CHUNK),
        grid=(b // nb, nblk),
        in_specs=[blk(fwd(0)), blk(bwd(0)), blk(fwd(1)), blk(bwd(2)), blk(fwd(3)), blk(bwd(3)),
                  pl.BlockSpec((2, n), lambda bi, i: (0, 0)),
                  state,
                  pl.BlockSpec((2, (2 + N_LEVELS) * CHUNK, CHUNK), c3),
                  pl.BlockSpec((2, (N_LEVELS + 1) * CHUNK, n), c3),
                  pl.BlockSpec((n, n), lambda bi, i: (0, 0))],
        out_specs=(blk(fwd(0)), blk(bwd(0)), state),
        out_shape=(jax.ShapeDtypeStruct((b, t, n), F32), jax.ShapeDtypeStruct((b, t, n), F32),
                   jax.ShapeDtypeStruct((b, 2, n, n), F32)),
        scratch_shapes=[pltpu.VMEM((nb, 2, n, n), F32)],
        compiler_params=_cparams(("parallel", "arbitrary")),
        name="hgrn2_scan",
    )(hg, hg, hg, hg, hg, hg, lb, s0, consts["a"], consts["lvl"], consts["hm"])


def _scan_constants_dir(rev):
    t = np.arange(CHUNK)
    p = (CHUNK - 1 - t) if rev else t
    pt, pu = p[:, None], p[None, :]
    causal = (pu <= pt).astype(np.float32)
    head = np.arange(REC_W) // CHUNK
    hm = (head[:, None] == head[None, :]).astype(np.float32)
    rows = [causal, (pu > pt).astype(np.float32)]
    masks = []
    for lev in range(N_LEVELS):
        blk = CHUNK >> (lev + 1)
        bound = (pt // (2 * blk)) * (2 * blk) + blk - 1
        right = (pt % (2 * blk)) >= blk
        rows.append(np.where(right, (pu > bound) & (pu <= pt), (pu > pt) & (pu <= bound)).astype(np.float32))
        masks.append(((pt // (2 * blk) == pu // (2 * blk)) & right & ((pu % (2 * blk)) < blk)).astype(np.float32))
    masks.append(np.eye(CHUNK, dtype=np.float32))
    d = 1 if rev else 0
    ex = np.zeros((LANES, 2 * REC_W), np.float32)
    for h in range(DN_HEADS):
        ex[d * DN_HEADS + h, h * CHUNK:(h + 1) * CHUNK] = 1.0
        ex[2 * DN_HEADS + d * DN_HEADS + h, REC_W + h * CHUNK:REC_W + (h + 1) * CHUNK] = 1.0
    return {
        "l": causal, "cz": np.tile(causal, (1, DN_HEADS)), "hm": hm, "ex": ex,
        "a": np.concatenate(rows, axis=0),
        "lvl": np.concatenate([np.tile(mk, (1, DN_HEADS)) for mk in masks], axis=0),
    }


def _scan_constants():
    fwd, bwd = _scan_constants_dir(False), _scan_constants_dir(True)
    out = {key: jnp.asarray(np.stack([fwd[key], bwd[key]])) for key in ("l", "cz", "ex", "a", "lvl")}
    out["hm"] = jnp.asarray(fwd["hm"], BF16)
    return out


def _outproj_kernel(a_ref, dof_ref, dob_ref, z_ref, gof_ref, gob_ref, gg_ref, dnw_ref, hgw_ref, g_ref,
                    w_ref, x_ref, gate_ref, sh_ref, sc_ref, rt_ref, x1_ref, h2_ref, aff_ref):
    gmat = g_ref[...]

    def gated(o, z, nw):
        return o * lax.rsqrt(_mm_hi(o * o, gmat) + EPS) * nw * _silu(z)

    dmix = gated(dof_ref[0] + dob_ref[0], z_ref[0], dnw_ref[...])
    gmix = gated(gof_ref[0] + gob_ref[0], gg_ref[0], hgw_ref[...])
    y = (jnp.dot(a_ref[0], w_ref[0:ATTN_Q_W, :], preferred_element_type=F32)
         + jnp.dot(dmix.astype(BF16), w_ref[ATTN_Q_W:ATTN_Q_W + REC_W, :], preferred_element_type=F32)
         + jnp.dot(gmix.astype(BF16), w_ref[ATTN_Q_W + REC_W:, :], preferred_element_type=F32))
    x1 = x_ref[0] + gate_ref[0] * y
    x1_ref[0] = x1
    ms = jnp.mean(x1 * x1, axis=-1, keepdims=True)
    h2 = x1 * lax.rsqrt(ms + EPS) * (1.0 + sc_ref[0]) + sh_ref[0]
    h2_ref[0] = h2.astype(BF16)
    logits = lax.dot_general(rt_ref[...], h2, (((1,), (1,)), ((), ())), precision=HIGHEST,
                             preferred_element_type=F32)
    e = jnp.exp(logits - jnp.max(logits, axis=0, keepdims=True))
    aff_ref[0] = e / jnp.sum(e, axis=0, keepdims=True)


def _output_projection(a, dof, dob, dn, gof, gob, hg, dnw, hgw, gavg, w_out, x, gate, shift, scale, router_t):
    b, t, d = x.shape
    tm = _tile(t, 512)
    n = REC_W
    per_b = gate.shape[0] > 1
    mod_map = (lambda bi, i: (bi, 0, 0)) if per_b else (lambda bi, i: (0, 0, 0))
    const = lambda bi, i: (0, 0)
    tok = lambda bi, i: (bi, i, 0)
    return pl.pallas_call(
        _outproj_kernel,
        grid=(b, t // tm),
        in_specs=[pl.BlockSpec((1, tm, ATTN_Q_W), tok),
                  pl.BlockSpec((1, tm, n), tok), pl.BlockSpec((1, tm, n), tok),
                  pl.BlockSpec((1, tm, n), lambda bi, i: (bi, i, 3)),
                  pl.BlockSpec((1, tm, n), tok), pl.BlockSpec((1, tm, n), tok),
                  pl.BlockSpec((1, tm, n), lambda bi, i: (bi, i, 4)),
                  pl.BlockSpec((1, n), const), pl.BlockSpec((1, n), const),
                  pl.BlockSpec((n, n), const),
                  pl.BlockSpec((d, d), const),
                  pl.BlockSpec((1, tm, d), tok),
                  pl.BlockSpec((1, 1, d), mod_map), pl.BlockSpec((1, 1, d), mod_map),
                  pl.BlockSpec((1, 1, d), mod_map),
                  pl.BlockSpec((N_EXPERTS, d), const)],
        out_specs=(pl.BlockSpec((1, tm, d), tok), pl.BlockSpec((1, tm, d), tok),
                   pl.BlockSpec((1, N_EXPERTS, tm), lambda bi, i: (bi, 0, i))),
        out_shape=(jax.ShapeDtypeStruct((b, t, d), F32), jax.ShapeDtypeStruct((b, t, d), BF16),
                   jax.ShapeDtypeStruct((b, N_EXPERTS, t), F32)),
        compiler_params=_cparams(("parallel", "parallel")),
        name="output_projection_router",
    )(a, dof, dob, dn, gof, gob, hg, dnw, hgw, gavg, w_out, x, gate, shift, scale, router_t)


def _select_kernel(aff_ref, u_ref, bs_ref, bst_ref, su_ref, slot_ref, base_ref, nsub_ref, inc_scr, *, cap, nblk):
    x = aff_ref[0]
    bits = pltpu.bitcast(x, jnp.int32)
    lo = jnp.zeros((N_EXPERTS, 1), jnp.int32)
    for bit in range(30, -1, -1):
        cand = lo | (1 << bit)
        cnt = jnp.sum((bits >= cand).astype(jnp.int32), axis=1, keepdims=True)
        lo = jnp.where(cnt >= cap, cand, lo)
    gt = bits > lo
    eq = bits == lo
    umat = u_ref[...]

    def prefix(mask):
        mb = mask.astype(BF16)
        for j in range(nblk):
            inc_scr[:, j * LANES:(j + 1) * LANES] = jnp.dot(mb[:, j * LANES:(j + 1) * LANES], umat,
                                                            preferred_element_type=F32)
        totals = jnp.dot(mb, bs_ref[...], preferred_element_type=F32)
        offs = _mm_hi(totals, su_ref[...])
        return inc_scr[...] + _mm_hi(offs, bst_ref[...]), offs, totals

    eqf = eq.astype(F32)
    n_gt = jnp.sum(gt.astype(F32), axis=1, keepdims=True)
    eq_before, _, _ = prefix(eqf)
    sel = gt | (eq & ((eq_before - eqf) < (cap - n_gt)))
    self_ = sel.astype(F32)
    pos, offs, totals = prefix(self_)
    slot_ref[0] = jnp.where(sel, pos - 1.0, -1.0).astype(jnp.int32)
    start = jnp.floor(offs * (1.0 / ROUTE_ALIGN)) * ROUTE_ALIGN
    span = offs + totals - start
    n_sub = functools.reduce(jnp.add, [jnp.where(span > j * ROUTE_SUB, 1.0, 0.0)
                                       for j in range(ROUTE_WIN // ROUTE_SUB)])
    base_ref[0] = start.astype(jnp.int32)
    nsub_ref[0] = jnp.max(n_sub, axis=0, keepdims=True).astype(jnp.int32)


def _moe_select(aff_t, cap):
    b, e, t = aff_t.shape
    nblk = t // ROUTE_BLK
    u = jnp.asarray(np.triu(np.ones((LANES, LANES), np.float32)), BF16)
    blk = np.arange(t) // ROUTE_BLK
    bs = (blk[:, None] == np.arange(nblk)[None, :]).astype(np.float32)
    su = np.triu(np.ones((nblk, nblk), np.float32), 1)
    const = lambda bi: (0, 0)
    return pl.pallas_call(
        functools.partial(_select_kernel, cap=cap, nblk=nblk),
        grid=(b,),
        in_specs=[pl.BlockSpec((1, e, t), lambda bi: (bi, 0, 0)),
                  pl.BlockSpec((LANES, LANES), const),
                  pl.BlockSpec((t, nblk), const),
                  pl.BlockSpec((nblk, t), const),
                  pl.BlockSpec((nblk, nblk), const)],
        out_specs=(pl.BlockSpec((1, e, t), lambda bi: (bi, 0, 0)),
                   pl.BlockSpec((1, e, nblk), lambda bi: (bi, 0, 0)),
                   pl.BlockSpec((1, 1, nblk), lambda bi: (bi, 0, 0))),
        out_shape=(jax.ShapeDtypeStruct((b, e, t), jnp.int32), jax.ShapeDtypeStruct((b, e, nblk), jnp.int32),
                   jax.ShapeDtypeStruct((b, 1, nblk), jnp.int32)),
        scratch_shapes=[pltpu.VMEM((e, t), F32)],
        compiler_params=_cparams(("parallel",)),
        name="moe_select",
    )(aff_t, u, jnp.asarray(bs, BF16), jnp.asarray(bs.T), jnp.asarray(su))


def _gather_kernel(start_sm, nsub_sm, slot_ref, h_ref, xe_ref, *, nblk):
    bi = pl.program_id(0)
    xe_ref[0] = jnp.zeros(xe_ref.shape[1:], BF16)
    row = lax.broadcasted_iota(jnp.int32, (ROUTE_SUB, ROUTE_BLK), 0)

    def body(k, carry):
        t0 = pl.multiple_of(k * ROUTE_BLK, ROUTE_BLK)
        hblk = h_ref[0, pl.ds(t0, ROUTE_BLK), :]
        slots = slot_ref[0, :, pl.ds(t0, ROUTE_BLK)]
        firsts = [start_sm[(bi * N_EXPERTS + e) * nblk + k] for e in range(N_EXPERTS)]

        def sub(j, inner):
            starts = [pl.multiple_of(first + j * ROUTE_SUB, ROUTE_ALIGN) for first in firsts]
            onehot = jnp.concatenate(
                [jnp.where(row == slots[e:e + 1, :] - starts[e], 1.0, 0.0).astype(BF16)
                 for e in range(N_EXPERTS)], axis=0)
            rows = jnp.dot(onehot, hblk, preferred_element_type=F32).astype(BF16)
            for e in range(N_EXPERTS):
                win = pl.ds(starts[e], ROUTE_SUB)
                xe_ref[0, e, win, :] = xe_ref[0, e, win, :] + rows[e * ROUTE_SUB:(e + 1) * ROUTE_SUB]
            return inner

        lax.fori_loop(0, nsub_sm[bi * nblk + k], sub, 0)
        return carry

    lax.fori_loop(0, nblk, body, 0)


def _moe_gather(h2, slot, start_flat, nsub_flat, cap):
    b, t, d = h2.shape
    nblk = t // ROUTE_BLK
    cp = cap + ROUTE_WIN
    dq = _tile(d, 256)
    grid_spec = pltpu.PrefetchScalarGridSpec(
        num_scalar_prefetch=2,
        grid=(b, d // dq),
        in_specs=[pl.BlockSpec((1, N_EXPERTS, t), lambda bi, j, s0, s1: (bi, 0, 0)),
                  pl.BlockSpec((1, t, dq), lambda bi, j, s0, s1: (bi, 0, j))],
        out_specs=pl.BlockSpec((1, N_EXPERTS, cp, dq), lambda bi, j, s0, s1: (bi, 0, 0, j)),
    )
    return pl.pallas_call(
        functools.partial(_gather_kernel, nblk=nblk),
        grid_spec=grid_spec,
        out_shape=jax.ShapeDtypeStruct((b, N_EXPERTS, cp, d), BF16),
        compiler_params=_cparams(("parallel", "parallel")),
        name="moe_gather",
    )(start_flat, nsub_flat, slot, h2)


def _ffn_kernel(x_ref, wg_ref, wu_ref, wd_ref, y_ref, *, cap):
    x = x_ref[0, 0, 0:cap, :]
    a = jnp.dot(x, wg_ref[0], preferred_element_type=F32)
    u = jnp.dot(x, wu_ref[0], preferred_element_type=F32)
    y_ref[0, 0, 0:cap, :] = jnp.dot((_silu(a) * u).astype(BF16), wd_ref[0],
                                    preferred_element_type=F32).astype(BF16)
    y_ref[0, 0, cap:, :] = jnp.zeros((y_ref.shape[2] - cap, y_ref.shape[3]), BF16)


def _moe_ffn(xe, wg, wu, wd, cap):
    b, e, cp, d = xe.shape
    f = wg.shape[-1]
    return pl.pallas_call(
        functools.partial(_ffn_kernel, cap=cap),
        grid=(e, b),
        in_specs=[pl.BlockSpec((1, 1, cp, d), lambda ei, bi: (bi, ei, 0, 0)),
                  pl.BlockSpec((1, d, f), lambda ei, bi: (ei, 0, 0)),
                  pl.BlockSpec((1, d, f), lambda ei, bi: (ei, 0, 0)),
                  pl.BlockSpec((1, f, d), lambda ei, bi: (ei, 0, 0))],
        out_specs=pl.BlockSpec((1, 1, cp, d), lambda ei, bi: (bi, ei, 0, 0)),
        out_shape=jax.ShapeDtypeStruct((b, e, cp, d), BF16),
        compiler_params=_cparams(("parallel", "parallel")),
        name="moe_ffn",
    )(xe, wg, wu, wd)


def _combine_kernel(start_sm, nsub_sm, ye_ref, x_ref, gate_ref, slot_ref, aff_ref, o_ref, *, nblk):
    bi, k = pl.program_id(0), pl.program_id(2)
    row = lax.broadcasted_iota(jnp.int32, (ROUTE_SUB, ROUTE_BLK), 0)
    slots = slot_ref[0]
    aff = aff_ref[0]
    firsts = [start_sm[(bi * N_EXPERTS + e) * nblk + k] for e in range(N_EXPERTS)]

    def sub(j, acc):
        starts = [pl.multiple_of(first + j * ROUTE_SUB, ROUTE_ALIGN) for first in firsts]
        wsel = jnp.concatenate([jnp.where(row == slots[e:e + 1, :] - starts[e], aff[e:e + 1, :], 0.0)
                                for e in range(N_EXPERTS)], axis=0)
        ys = jnp.concatenate([ye_ref[0, e, pl.ds(starts[e], ROUTE_SUB), :] for e in range(N_EXPERTS)], axis=0)
        hi, lo = _split2(wsel)
        tn = (((0,), (0,)), ((), ()))
        return (acc + lax.dot_general(hi, ys, tn, preferred_element_type=F32)
                + lax.dot_general(lo, ys, tn, preferred_element_type=F32))

    acc = lax.fori_loop(0, nsub_sm[bi * nblk + k], sub, jnp.zeros(x_ref.shape[1:], F32))
    o_ref[0] = x_ref[0] + gate_ref[0] * acc


def _moe_combine(ye, x1, gate, slot, aff_t, start_flat, nsub_flat):
    b, t, d = x1.shape
    cp = ye.shape[2]
    nblk = t // ROUTE_BLK
    dh = d // 2
    per_b = gate.shape[0] > 1
    grid_spec = pltpu.PrefetchScalarGridSpec(
        num_scalar_prefetch=2,
        grid=(b, 2, nblk),
        in_specs=[pl.BlockSpec((1, N_EXPERTS, cp, dh), lambda bi, j, k, s0, s1: (bi, 0, 0, j)),
                  pl.BlockSpec((1, ROUTE_BLK, dh), lambda bi, j, k, s0, s1: (bi, k, j)),
                  pl.BlockSpec((1, 1, dh), (lambda bi, j, k, s0, s1: (bi, 0, j)) if per_b
                               else (lambda bi, j, k, s0, s1: (0, 0, j))),
                  pl.BlockSpec((1, N_EXPERTS, ROUTE_BLK), lambda bi, j, k, s0, s1: (bi, 0, k)),
                  pl.BlockSpec((1, N_EXPERTS, ROUTE_BLK), lambda bi, j, k, s0, s1: (bi, 0, k))],
        out_specs=pl.BlockSpec((1, ROUTE_BLK, dh), lambda bi, j, k, s0, s1: (bi, k, j)),
    )
    return pl.pallas_call(
        functools.partial(_combine_kernel, nblk=nblk),
        grid_spec=grid_spec,
        out_shape=jax.ShapeDtypeStruct((b, t, d), F32),
        compiler_params=_cparams(("parallel", "parallel", "arbitrary")),
        name="moe_combine",
    )(start_flat, nsub_flat, ye, x1, gate, slot, aff_t)


def _expert_choice_ffn(x1, h2, aff_t, gate, wg, wu, wd):
    b, t, _ = x1.shape
    cap = CAPACITY_FACTOR * t // N_EXPERTS
    slot, start, nsub = _moe_select(aff_t, cap)
    start_flat, nsub_flat = start.reshape(-1), nsub.reshape(-1)
    xe = _moe_gather(h2, slot, start_flat, nsub_flat, cap)
    ye = _moe_ffn(xe, wg, wu, wd, cap)
    return _moe_combine(ye, x1, gate, slot, aff_t, start_flat, nsub_flat)


def _rope_tables(n_tokens):
    rows = n_tokens // GRID_W
    row = jnp.repeat(jnp.arange(rows, dtype=F32), GRID_W)
    col = jnp.tile(jnp.arange(GRID_W, dtype=F32), rows)
    n_freq = HEAD_DIM // 4
    inv_freq = ROPE_THETA ** (-jnp.arange(n_freq, dtype=F32) / n_freq)
    ang_r = row[:, None] * inv_freq
    ang_c = col[:, None] * inv_freq
    cos = jnp.concatenate([jnp.cos(ang_r)] * 2 + [jnp.cos(ang_c)] * 2, axis=-1)
    sin = jnp.concatenate([jnp.sin(ang_r)] * 2 + [jnp.sin(ang_c)] * 2, axis=-1)
    return jnp.tile(cos, (1, LANES // HEAD_DIM)), jnp.tile(sin, (1, LANES // HEAD_DIM))


def _block_diag(width, block, value):
    idx = np.arange(width) // block
    return jnp.asarray((idx[:, None] == idx[None, :]).astype(np.float32) * value)


def kernel(x, c, ctx, c_ctx, w_mod, b_mod, w_in, w_out, attn_q_norm, attn_k_norm, dn_conv, dn_a_log,
           dn_dt_bias, dn_norm, hg_lower_bounds, hg_norm, moe_router, moe_w_gate, moe_w_up, moe_w_down):
    depth = w_mod.shape[0]
    b, t_lat, d = x.shape
    cos, sin = _rope_tables(t_lat)
    g_head = _block_diag(LANES, HEAD_DIM, 1.0 / HEAD_DIM)
    g_mean = _block_diag(REC_W, HEAD_DIM, 1.0 / HEAD_DIM)
    g_sum = _block_diag(REC_W, HEAD_DIM, 1.0)
    consts = _scan_constants()
    s_zero = jnp.zeros((b, 2, REC_W, REC_W), F32)

    lb_w = jax.nn.softmax(hg_lower_bounds.astype(F32), axis=0)
    hg_lb = jnp.cumsum(lb_w, axis=0) - lb_w[0]

    rows = ((b + 1 + 7) // 8) * 8
    cond = jnp.zeros((rows, d), F32).at[:b].set(c).at[b].set(c_ctx)
    mod = _modulation(cond, w_mod, b_mod)

    n_small = 4 * DN_HEADS
    w_in_r = jnp.concatenate(
        [w_in[:, :, :C_HG], w_in[:, :, C_HG + n_small:], w_in[:, :, C_HG:C_HG + n_small],
         jnp.zeros((depth, d, C_END - C_BA - n_small), w_in.dtype)], axis=-1).astype(BF16)
    w_out_b = w_out.astype(BF16)
    wg_b, wu_b, wd_b = moe_w_gate.astype(BF16), moe_w_up.astype(BF16), moe_w_down.astype(BF16)

    x_lat, x_ctx = x, ctx
    for l in range(depth):
        ctx_out = l < depth - 1
        m_lat = [mod[l, :b, j * d:(j + 1) * d][:, None, :] for j in range(6)]
        m_ctx = [mod[l, b:b + 1, j * d:(j + 1) * d][:, None, :] for j in range(6)]
        qn = jnp.tile(attn_q_norm[l], LANES // HEAD_DIM)[None, :]
        kn = jnp.tile(attn_k_norm[l], LANES // HEAD_DIM)[None, :]
        conv_w = jnp.zeros((8, 3 * REC_W), F32).at[:CONV_K].set(dn_conv[l])
        pad = jnp.zeros((LANES - 4 * DN_HEADS,), F32)
        neg_a = jnp.concatenate([jnp.zeros((2 * DN_HEADS,), F32), -jnp.exp(dn_a_log[l].reshape(-1)), pad])[None, :]
        dt_b = jnp.concatenate([jnp.zeros((2 * DN_HEADS,), F32), dn_dt_bias[l].reshape(-1), pad])[None, :]
        dnw = jnp.tile(dn_norm[l], DN_HEADS)[None, :]
        hgw = jnp.tile(hg_norm[l], HG_HEADS)[None, :]
        router_t = moe_router[l].T

        streams = {}
        for name, xs, ms, rotate in (("ctx", x_ctx, m_ctx, False), ("lat", x_lat, m_lat, True)):
            t = xs.shape[1]
            q, kt, v, dn, hg, ba = _input_projection(xs, ms[0], ms[1], w_in_r[l], qn, kn,
                                                     cos[:t], sin[:t], g_head, rotate)
            qkv, gb = _dn_prep(dn, ba, conv_w, neg_a, dt_b, g_sum)
            streams[name] = dict(q=q, kv=(kt, v), dn=dn, hg=hg, qkv=qkv, gb=gb)

        sc, sl = streams["ctx"], streams["lat"]
        dcf, dcb, dn_state = _dn_scan(sc["qkv"], sc["gb"], s_zero, consts)
        dlf, dlb, _ = _dn_scan(sl["qkv"], sl["gb"], dn_state, consts)
        gcf, gcb, hg_state = _hg_scan(sc["hg"], hg_lb[l], s_zero, consts)
        glf, glb, _ = _hg_scan(sl["hg"], hg_lb[l], hg_state, consts)

        a_lat = _attention(sl["q"], [sc["kv"], sl["kv"]])
        x1, h2, aff_t = _output_projection(
            a_lat, dlf, dlb, sl["dn"], glf, glb, sl["hg"], dnw, hgw, g_mean, w_out_b[l], x_lat,
            m_lat[2], m_lat[3], m_lat[4], router_t)
        x_lat = _expert_choice_ffn(x1, h2, aff_t, m_lat[5], wg_b[l], wu_b[l], wd_b[l])
        if ctx_out:
            a_ctx = _attention(sc["q"], [sc["kv"]])
            x1, h2, aff_t = _output_projection(
                a_ctx, dcf, dcb, sc["dn"], gcf, gcb, sc["hg"], dnw, hgw, g_mean, w_out_b[l], x_ctx,
                m_ctx[2], m_ctx[3], m_ctx[4], router_t)
            x_ctx = _expert_choice_ffn(x1, h2, aff_t, m_ctx[5], wg_b[l], wu_b[l], wd_b[l])
    return x_lat
```

```python
import functools

import numpy as np
import jax
import jax.numpy as jnp
from jax import lax
from jax.experimental import pallas as pl
from jax.experimental.pallas import tpu as pltpu

F32 = jnp.float32
BF16 = jnp.bfloat16
HIGHEST = lax.Precision.HIGHEST

HEAD_DIM = 64
N_Q_HEADS = 8
N_KV_HEADS = 2
GQA_GROUP = N_Q_HEADS // N_KV_HEADS
DN_HEADS = 4
HG_HEADS = 4
GRID_W = 64
ROPE_THETA = 10000.0
CONV_K = 5
CHUNK = 64
N_EXPERTS = 16
CAPACITY_FACTOR = 2
EPS = 1e-6
LOG2E = 1.4426950408889634

ATTN_Q_W = N_Q_HEADS * HEAD_DIM
ATTN_KV_W = N_KV_HEADS * HEAD_DIM
REC_W = DN_HEADS * HEAD_DIM
LANES = 128
ROUTE_BLK = LANES
ROUTE_ALIGN = 16
ROUTE_WIN = ROUTE_BLK + ROUTE_ALIGN
ROUTE_SUB = ROUTE_WIN // 3
N_LEVELS = 6
SCAN_BATCH = 4
VMEM_LIMIT = 56 * 1024 * 1024

C_Q, C_K, C_V, C_DN, C_HG, C_BA, C_END = 0, 512, 640, 768, 1792, 3072, 3200


def _mm(a, b):
    return jnp.dot(a.astype(BF16), b.astype(BF16), preferred_element_type=F32)


def _mm_tn(a, b):
    return lax.dot_general(a.astype(BF16), b.astype(BF16), (((0,), (0,)), ((), ())),
                           preferred_element_type=F32)


def _mm_hi(a, b):
    return jnp.dot(a, b, precision=HIGHEST, preferred_element_type=F32)


def _sigmoid(x):
    return 1.0 / (1.0 + jnp.exp(-x))


def _silu(x):
    return x * _sigmoid(x)


def _cparams(sem):
    return pltpu.CompilerParams(dimension_semantics=sem, vmem_limit_bytes=VMEM_LIMIT)


def _tile(n, pref):
    return pref if n % pref == 0 else n


def _mod_kernel(c_ref, w_ref, b_ref, o_ref):
    o_ref[0] = _mm_hi(_silu(c_ref[...]), w_ref[0]) + b_ref[0]


def _modulation(cond, w_mod, b_mod):
    depth, d, n = w_mod.shape
    rows = cond.shape[0]
    tn = _tile(n, 1536)
    return pl.pallas_call(
        _mod_kernel,
        grid=(depth, n // tn),
        in_specs=[pl.BlockSpec((rows, d), lambda l, j: (0, 0)),
                  pl.BlockSpec((1, d, tn), lambda l, j: (l, 0, j)),
                  pl.BlockSpec((1, 1, tn), lambda l, j: (l, 0, j))],
        out_specs=pl.BlockSpec((1, rows, tn), lambda l, j: (l, 0, j)),
        out_shape=jax.ShapeDtypeStruct((depth, rows, n), F32),
        compiler_params=_cparams(("parallel", "parallel")),
        name="modulation",
    )(cond, w_mod, b_mod.reshape(depth, 1, n))


def _inproj_kernel(x_ref, sh_ref, sc_ref, w_ref, qn_ref, kn_ref, cos_ref, sin_ref, g_ref,
                   q_ref, k_ref, v_ref, dn_ref, hg_ref, ba_ref, *, rotate):
    x = x_ref[0]
    ms = jnp.mean(x * x, axis=-1, keepdims=True)
    h = x * lax.rsqrt(ms + EPS) * (1.0 + sc_ref[0]) + sh_ref[0]
    hb = h.astype(BF16)
    gmat = g_ref[...]

    def proj(lo, hi):
        return jnp.dot(hb, w_ref[:, lo:hi], preferred_element_type=F32)

    def head_norm_rope(t, nw):
        t = t * lax.rsqrt(_mm_hi(t * t, gmat) + EPS) * nw
        if rotate:
            lane = lax.broadcasted_iota(jnp.int32, t.shape, 1)
            first = (lane % 32) < 16
            rot = jnp.where(first, -pltpu.roll(t, LANES - 16, 1), pltpu.roll(t, 16, 1))
            t = t * cos_ref[...] + rot * sin_ref[...]
        return t

    def q_group(j):
        t = proj(C_Q + j * LANES, C_Q + (j + 1) * LANES)
        yield
        qj = head_norm_rope(t, qn_ref[...])
        q_ref[0, :, j * LANES:(j + 1) * LANES] = (qj * (HEAD_DIM ** -0.5 * LOG2E)).astype(BF16)

    def k_group():
        t = proj(C_K, C_V)
        yield
        kt = head_norm_rope(t, kn_ref[...]).T
        k_ref[0, 0] = kt[:HEAD_DIM].astype(BF16)
        k_ref[0, 1] = kt[HEAD_DIM:].astype(BF16)

    def v_group():
        v = proj(C_V, C_DN)
        yield
        ones = jnp.ones((v.shape[0], LANES - HEAD_DIM), F32)
        v_ref[0, 0] = jnp.concatenate([v[:, :HEAD_DIM], ones], axis=1).astype(BF16)
        v_ref[0, 1] = jnp.concatenate([v[:, HEAD_DIM:], ones], axis=1).astype(BF16)

    def plain_group(o_ref, base, lo, hi):
        t = proj(base + lo, base + hi)
        yield
        o_ref[0, :, lo:hi] = t

    step = 4 * LANES
    plain = [plain_group(o_ref, base, lo, min(lo + step, width))
             for o_ref, base, width in ((dn_ref, C_DN, C_HG - C_DN), (hg_ref, C_HG, C_BA - C_HG),
                                        (ba_ref, C_BA, C_END - C_BA))
             for lo in range(0, width, step)]
    _lockstep([q_group(j) for j in range(ATTN_Q_W // LANES)] + [k_group(), v_group()] + plain, stagger=True)


def _input_projection(x, shift, scale, w, qn, kn, cos, sin, gmat, rotate):
    b, t, d = x.shape
    tm = _tile(t, 512)
    per_b = shift.shape[0] > 1
    mod_map = (lambda bi, i: (bi, 0, 0)) if per_b else (lambda bi, i: (0, 0, 0))
    const = lambda bi, i: (0, 0)
    out_shape = (
        jax.ShapeDtypeStruct((b, t, ATTN_Q_W), BF16),
        jax.ShapeDtypeStruct((b, N_KV_HEADS, HEAD_DIM, t), BF16),
        jax.ShapeDtypeStruct((b, N_KV_HEADS, t, LANES), BF16),
        jax.ShapeDtypeStruct((b, t, C_HG - C_DN), F32),
        jax.ShapeDtypeStruct((b, t, C_BA - C_HG), F32),
        jax.ShapeDtypeStruct((b, t, C_END - C_BA), F32),
    )
    return pl.pallas_call(
        functools.partial(_inproj_kernel, rotate=rotate),
        grid=(b, t // tm),
        in_specs=[pl.BlockSpec((1, tm, d), lambda bi, i: (bi, i, 0)),
                  pl.BlockSpec((1, 1, d), mod_map),
                  pl.BlockSpec((1, 1, d), mod_map),
                  pl.BlockSpec((d, C_END), const),
                  pl.BlockSpec((1, LANES), const),
                  pl.BlockSpec((1, LANES), const),
                  pl.BlockSpec((tm, LANES), lambda bi, i: (i, 0)),
                  pl.BlockSpec((tm, LANES), lambda bi, i: (i, 0)),
                  pl.BlockSpec((LANES, LANES), const)],
        out_specs=(pl.BlockSpec((1, tm, ATTN_Q_W), lambda bi, i: (bi, i, 0)),
                   pl.BlockSpec((1, N_KV_HEADS, HEAD_DIM, tm), lambda bi, i: (bi, 0, 0, i)),
                   pl.BlockSpec((1, N_KV_HEADS, tm, LANES), lambda bi, i: (bi, 0, i, 0)),
                   pl.BlockSpec((1, tm, C_HG - C_DN), lambda bi, i: (bi, i, 0)),
                   pl.BlockSpec((1, tm, C_BA - C_HG), lambda bi, i: (bi, i, 0)),
                   pl.BlockSpec((1, tm, C_END - C_BA), lambda bi, i: (bi, i, 0))),
        out_shape=out_shape,
        compiler_params=_cparams(("parallel", "parallel")),
        name="input_projection",
    )(x, shift, scale, w, qn, kn, cos, sin, gmat)


def _lockstep(gens, stagger=False):
    results = [None] * len(gens)
    live, started = [], 0
    while live or started < len(gens):
        fresh = 1 if stagger else len(gens)
        live += list(range(started, min(started + fresh, len(gens))))
        started = min(started + fresh, len(gens))
        for i in reversed(list(live)):
            try:
                next(gens[i])
            except StopIteration as stop:
                results[i] = stop.value
                live.remove(i)
    return results


def _attn_head(qh, kts, vs):
    ss = [jnp.dot(qh, kt, preferred_element_type=F32).astype(BF16) for kt in kts]
    yield
    m = functools.reduce(jnp.maximum, [jnp.max(s, axis=-1, keepdims=True) for s in ss])
    ps = [jnp.exp2(s - m) for s in ss]
    yield
    o = functools.reduce(jnp.add, [jnp.dot(p, v[0, 0], preferred_element_type=F32) for p, v in zip(ps, vs)])
    return o[:, :HEAD_DIM] / o[:, HEAD_DIM:HEAD_DIM + 1]


def _attn_kernel(*refs, n_src):
    q_ref, o_ref = refs[0], refs[-1]
    kts = [refs[1 + 2 * i][0, 0] for i in range(n_src)]
    vs = [refs[2 + 2 * i] for i in range(n_src)]
    q = q_ref[0]
    outs = _lockstep([_attn_head(q[:, h * HEAD_DIM:(h + 1) * HEAD_DIM], kts, vs) for h in range(GQA_GROUP)],
                     stagger=True)
    o_ref[0] = jnp.concatenate(outs, axis=-1).astype(BF16)


def _attention(q, sources):
    b, t, _ = q.shape
    tq = _tile(t, 256)
    gw = GQA_GROUP * HEAD_DIM
    in_specs = [pl.BlockSpec((1, tq, gw), lambda bi, g, i: (bi, i, g))]
    args = [q]
    for kt, v in sources:
        tk = kt.shape[-1]
        in_specs.append(pl.BlockSpec((1, 1, HEAD_DIM, tk), lambda bi, g, i: (bi, g, 0, 0)))
        in_specs.append(pl.BlockSpec((1, 1, tk, LANES), lambda bi, g, i: (bi, g, 0, 0)))
        args += [kt, v]
    return pl.pallas_call(
        functools.partial(_attn_kernel, n_src=len(sources)),
        grid=(b, N_KV_HEADS, t // tq),
        in_specs=in_specs,
        out_specs=pl.BlockSpec((1, tq, gw), lambda bi, g, i: (bi, i, g)),
        out_shape=jax.ShapeDtypeStruct((b, t, ATTN_Q_W), BF16),
        compiler_params=_cparams(("parallel", "parallel", "parallel")),
        name="attention",
    )(*args)


def _dn_prep_kernel(x_ref, xp_ref, xn_ref, ba_ref, cw_ref, na_ref, dtb_ref, g_ref, o_ref, gb_ref, buf):
    i = pl.program_id(1)
    n = pl.num_programs(1)
    tm = x_ref.shape[1]
    buf[0:8] = jnp.where(i > 0, xp_ref[0], 0.0)
    buf[8:8 + tm] = x_ref[0]
    buf[8 + tm:16 + tm] = jnp.where(i < n - 1, xn_ref[0], 0.0)
    half = CONV_K // 2
    y = cw_ref[0:1, :] * buf[8 - half:8 - half + tm]
    for j in range(1, CONV_K):
        y = y + cw_ref[j:j + 1, :] * buf[8 - half + j:8 - half + j + tm]
    y = _silu(y)
    gmat = g_ref[...]
    q = y[:, :REC_W]
    k = y[:, REC_W:2 * REC_W]
    o_ref[0, :, 0:REC_W] = q * lax.rsqrt(_mm_hi(q * q, gmat) + EPS) * HEAD_DIM ** -0.5
    o_ref[0, :, REC_W:2 * REC_W] = k * lax.rsqrt(_mm_hi(k * k, gmat) + EPS)
    o_ref[0, :, 2 * REC_W:] = y[:, 2 * REC_W:]
    ba = ba_ref[0]
    z = ba + dtb_ref[...]
    softplus = jnp.maximum(z, 0.0) + jnp.log1p(jnp.exp(-jnp.abs(z)))
    lane = lax.broadcasted_iota(jnp.int32, ba.shape, 1)
    gb_ref[0] = jnp.where(lane < 2 * DN_HEADS, _sigmoid(ba), na_ref[...] * softplus)


def _dn_prep(dn, ba, conv_w, neg_a, dt_bias, gsum):
    b, t, _ = dn.shape
    tm = _tile(t, 512)
    w3 = 3 * REC_W
    nb8 = t // 8
    return pl.pallas_call(
        _dn_prep_kernel,
        grid=(b, t // tm),
        in_specs=[pl.BlockSpec((1, tm, w3), lambda bi, i: (bi, i, 0)),
                  pl.BlockSpec((1, 8, w3), lambda bi, i: (bi, jnp.maximum(i * (tm // 8) - 1, 0), 0)),
                  pl.BlockSpec((1, 8, w3), lambda bi, i: (bi, jnp.minimum((i + 1) * (tm // 8), nb8 - 1), 0)),
                  pl.BlockSpec((1, tm, LANES), lambda bi, i: (bi, i, 0)),
                  pl.BlockSpec((8, w3), lambda bi, i: (0, 0)),
                  pl.BlockSpec((1, LANES), lambda bi, i: (0, 0)),
                  pl.BlockSpec((1, LANES), lambda bi, i: (0, 0)),
                  pl.BlockSpec((REC_W, REC_W), lambda bi, i: (0, 0))],
        out_specs=(pl.BlockSpec((1, tm, w3), lambda bi, i: (bi, i, 0)),
                   pl.BlockSpec((1, tm, LANES), lambda bi, i: (bi, i, 0))),
        out_shape=(jax.ShapeDtypeStruct((b, t, w3), F32), jax.ShapeDtypeStruct((b, t, LANES), F32)),
        scratch_shapes=[pltpu.VMEM((tm + 16, w3), F32)],
        compiler_params=_cparams(("parallel", "parallel")),
        name="deltanet_prep",
    )(dn, dn, dn, ba, conv_w, neg_a, dt_bias, gsum)


def _split2(a):
    hi = a.astype(BF16)
    return hi, (a - hi.astype(F32)).astype(BF16)


def _split3(a):
    hi = a.astype(BF16)
    r = a - hi.astype(F32)
    mid = r.astype(BF16)
    return hi, mid, (r - mid.astype(F32)).astype(BF16)


def _bdiag(a, hm):
    return jnp.concatenate([a] * DN_HEADS, axis=0) * hm


def _mm_bd(a, b, hm):
    return jnp.dot(a.astype(BF16), _bdiag(b.astype(BF16), hm), preferred_element_type=F32)


def _mm_sel_l(sel, b):
    n = b.shape[1]
    t = jnp.dot(sel.astype(BF16), jnp.concatenate(_split3(b), axis=1), preferred_element_type=F32)
    return t[:, :n] + t[:, n:2 * n] + t[:, 2 * n:]


def _mm_sel_r(a, sel):
    m = a.shape[0]
    t = jnp.dot(jnp.concatenate(_split3(a), axis=0), sel.astype(BF16), preferred_element_type=F32)
    return t[:m] + t[m:2 * m] + t[2 * m:]


def _dn_chunk(x, gb, s, ex, lmat, causal, lvl_ref, d, hm):
    n = REC_W
    lvl = lambda j: lvl_ref[d, j * CHUNK:(j + 1) * CHUNK, :]
    eye = lvl(N_LEVELS)
    ones8 = jnp.ones((8, CHUNK), F32)
    q, k, v = x[:, :n], x[:, n:2 * n], x[:, 2 * n:]
    gbx = _mm_sel_r(gb, ex)
    yield
    beta, g = gbx[:, :n], gbx[:, n:]
    cum = _mm_sel_l(lmat, g)
    clast = _mm_sel_l(ones8, g)[0:1]
    clast_rows = lax.dot_general(g, jnp.ones((CHUNK, n), F32), (((0,), (0,)), ((), ())), precision=HIGHEST,
                                 preferred_element_type=F32)
    kb = k * beta
    ks = _bdiag(k.astype(BF16), hm)
    kk = lax.dot_general(kb.astype(BF16), ks, (((1,), (1,)), ((), ())), preferred_element_type=F32)
    qk = lax.dot_general(q.astype(BF16), ks, (((1,), (1,)), ((), ())), preferred_element_type=F32)
    yield
    cum_s = _mm_sel_l(ones8, cum * eye)[0:1]
    ecum = jnp.exp(cum)
    kdec = k * jnp.exp(clast - cum)
    yield
    decay = jnp.where(causal, jnp.exp(jnp.where(causal, cum - cum_s, 0.0)), 0.0)
    m = kk * decay
    attn = qk * decay
    inv = eye - lvl(N_LEVELS - 1) * m
    for lev in range(N_LEVELS - 2, -1, -1):
        half = _mm_bd(inv, lvl(lev) * m, hm)
        yield
        inv = inv - _mm_bd(half, inv, hm)
        yield
    rhs = v * beta - _mm(kb * ecum, s)
    qs = _mm(q * ecum, s)
    yield
    v_new = _mm_bd(inv, rhs, hm)
    yield
    out = qs + _mm_bd(attn, v_new, hm)
    s_new = s * jnp.exp(clast_rows) + _mm_tn(kdec, v_new) * hm.astype(F32)
    return out, s_new


def _hg_chunk(qr, fr, v, lb, s, amat, lvl_ref, d, hm):
    n = REC_W
    lvl = lambda j: lvl_ref[d, j * CHUNK:(j + 1) * CHUNK, :]
    q = _silu(qr)
    f = lb + (1.0 - lb) * _sigmoid(fr)
    k = 1.0 - f
    g = jnp.log(f)
    ex = jnp.exp(_mm_sel_l(amat, g))
    clast_rows = lax.dot_general(g, jnp.ones((CHUNK, n), F32), (((0,), (0,)), ((), ())), precision=HIGHEST,
                                 preferred_element_type=F32)
    yield

    def level(xl, mask):
        kl = _bdiag((k * xl).astype(BF16), hm)
        return lax.dot_general((q * xl).astype(BF16), kl, (((1,), (1,)), ((), ())),
                               preferred_element_type=F32) * mask

    attn = level(1.0, lvl(N_LEVELS))
    for lev in range(N_LEVELS):
        attn = attn + level(ex[(2 + lev) * CHUNK:(3 + lev) * CHUNK], lvl(lev))
    qs = _mm(q * ex[0:CHUNK], s)
    kv = _mm_tn(k * ex[CHUNK:2 * CHUNK], v)
    yield
    out = _mm_bd(attn, v, hm) + qs
    s_new = s * jnp.exp(clast_rows) + kv * hm.astype(F32)
    return out, s_new


def _dn_scan_kernel(xf_ref, xb_ref, gf_ref, gb_ref, s0_ref, ex_ref, l_ref, cz_ref, lvl_ref, hm_ref,
                    of_ref, ob_ref, sf_ref, s_scr, *, nchunk):
    @pl.when(pl.program_id(1) == 0)
    def _():
        s_scr[...] = s0_ref[...]

    hm = hm_ref[...]
    refs = ((xf_ref, gf_ref, of_ref), (xb_ref, gb_ref, ob_ref))

    def body(c, carry):
        rows = [pl.ds(pl.multiple_of(cc * CHUNK, CHUNK), CHUNK) for cc in (c, nchunk - 1 - c)]
        insts = [(bi, d) for bi in range(s_scr.shape[0]) for d in range(2)]
        res = _lockstep([_dn_chunk(refs[d][0][bi, rows[d], :], refs[d][1][bi, rows[d], :], s_scr[bi, d],
                                   ex_ref[d], l_ref[d], cz_ref[d] > 0, lvl_ref, d, hm) for bi, d in insts])
        for (bi, d), (out, s_new) in zip(insts, res):
            refs[d][2][bi, rows[d], :] = out
            s_scr[bi, d] = s_new
        return carry

    lax.fori_loop(0, nchunk, body, 0)
    sf_ref[...] = s_scr[...]


def _hg_scan_kernel(qf_ref, qb_ref, ff_ref, fb_ref, if_ref, ib_ref, lb_ref, s0_ref, a_ref, lvl_ref, hm_ref,
                    of_ref, ob_ref, sf_ref, s_scr, *, nchunk):
    @pl.when(pl.program_id(1) == 0)
    def _():
        s_scr[...] = s0_ref[...]

    hm = hm_ref[...]
    refs = ((qf_ref, ff_ref, if_ref, of_ref), (qb_ref, fb_ref, ib_ref, ob_ref))

    def body(c, carry):
        rows = [pl.ds(pl.multiple_of(cc * CHUNK, CHUNK), CHUNK) for cc in (c, nchunk - 1 - c)]
        insts = [(bi, d) for bi in range(s_scr.shape[0]) for d in range(2)]
        res = _lockstep([_hg_chunk(refs[d][0][bi, rows[d], :], refs[d][1][bi, rows[d], :],
                                   refs[d][2][bi, rows[d], :], lb_ref[d:d + 1, :], s_scr[bi, d], a_ref[d],
                                   lvl_ref, d, hm) for bi, d in insts])
        for (bi, d), (out, s_new) in zip(insts, res):
            refs[d][3][bi, rows[d], :] = out
            s_scr[bi, d] = s_new
        return carry

    lax.fori_loop(0, nchunk, body, 0)
    sf_ref[...] = s_scr[...]


def _dn_scan(qkv, gb, s0, consts):
    b, t, w3 = qkv.shape
    tb = _tile(t, 512)
    nblk = t // tb
    n = REC_W
    nb = _tile(b, SCAN_BATCH)
    fwd = lambda bi, i: (bi, i, 0)
    bwd = lambda bi, i: (bi, nblk - 1 - i, 0)
    c3 = lambda bi, i: (0, 0, 0)
    state = pl.BlockSpec((nb, 2, n, n), lambda bi, i: (bi, 0, 0, 0))
    return pl.pallas_call(
        functools.partial(_dn_scan_kernel, nchunk=tb // CHUNK),
        grid=(b // nb, nblk),
        in_specs=[pl.BlockSpec((nb, tb, w3), fwd), pl.BlockSpec((nb, tb, w3), bwd),
                  pl.BlockSpec((nb, tb, LANES), fwd), pl.BlockSpec((nb, tb, LANES), bwd),
                  state,
                  pl.BlockSpec((2, LANES, 2 * n), c3),
                  pl.BlockSpec((2, CHUNK, CHUNK), c3),
                  pl.BlockSpec((2, CHUNK, n), c3),
                  pl.BlockSpec((2, (N_LEVELS + 1) * CHUNK, n), c3),
                  pl.BlockSpec((n, n), lambda bi, i: (0, 0))],
        out_specs=(pl.BlockSpec((nb, tb, n), fwd), pl.BlockSpec((nb, tb, n), bwd), state),
        out_shape=(jax.ShapeDtypeStruct((b, t, n), F32), jax.ShapeDtypeStruct((b, t, n), F32),
                   jax.ShapeDtypeStruct((b, 2, n, n), F32)),
        scratch_shapes=[pltpu.VMEM((nb, 2, n, n), F32)],
        compiler_params=_cparams(("parallel", "arbitrary")),
        name="deltanet_scan",
    )(qkv, qkv, gb, gb, s0, consts["ex"], consts["l"], consts["cz"], consts["lvl"], consts["hm"])


def _hg_scan(hg, lb, s0, consts):
    b, t, _ = hg.shape
    tb = _tile(t, 512)
    nblk = t // tb
    n = REC_W
    nb = _tile(b, SCAN_BATCH)
    fwd = lambda col: (lambda bi, i: (bi, i, col))
    bwd = lambda col: (lambda bi, i: (bi, nblk - 1 - i, col))
    c3 = lambda bi, i: (0, 0, 0)
    blk = lambda m: pl.BlockSpec((nb, tb, n), m)
    state = pl.BlockSpec((nb, 2, n, n), lambda bi, i: (bi, 0, 0, 0))
    return pl.pallas_call(
        functools.partial(_hg_scan_kernel, nchunk=tb // CHUNK),
        grid=(b // nb, nblk),
        in_specs=[blk(fwd(0)), blk(bwd(0)), blk(fwd(1)), blk(bwd(2)), blk(fwd(3)), blk(bwd(3)),
                  pl.BlockSpec((2, n), lambda bi, i: (0, 0)),
                  state,
                  pl.BlockSpec((2, (2 + N_LEVELS) * CHUNK, CHUNK), c3),
                  pl.BlockSpec((2, (N_LEVELS + 1) * CHUNK, n), c3),
                  pl.BlockSpec((n, n), lambda bi, i: (0, 0))],
        out_specs=(blk(fwd(0)), blk(bwd(0)), state),
        out_shape=(jax.ShapeDtypeStruct((b, t, n), F32), jax.ShapeDtypeStruct((b, t, n), F32),
                   jax.ShapeDtypeStruct((b, 2, n, n), F32)),
        scratch_shapes=[pltpu.VMEM((nb, 2, n, n), F32)],
        compiler_params=_cparams(("parallel", "arbitrary")),
        name="hgrn2_scan",
    )(hg, hg, hg, hg, hg, hg, lb, s0, consts["a"], consts["lvl"], consts["hm"])


def _scan_constants_dir(rev):
    t = np.arange(CHUNK)
    p = (CHUNK - 1 - t) if rev else t
    pt, pu = p[:, None], p[None, :]
    causal = (pu <= pt).astype(np.float32)
    head = np.arange(REC_W) // CHUNK
    hm = (head[:, None] == head[None, :]).astype(np.float32)
    rows = [causal, (pu > pt).astype(np.float32)]
    masks = []
    for lev in range(N_LEVELS):
        blk = CHUNK >> (lev + 1)
        bound = (pt // (2 * blk)) * (2 * blk) + blk - 1
        right = (pt % (2 * blk)) >= blk
        rows.append(np.where(right, (pu > bound) & (pu <= pt), (pu > pt) & (pu <= bound)).astype(np.float32))
        masks.append(((pt // (2 * blk) == pu // (2 * blk)) & right & ((pu % (2 * blk)) < blk)).astype(np.float32))
    masks.append(np.eye(CHUNK, dtype=np.float32))
    d = 1 if rev else 0
    ex = np.zeros((LANES, 2 * REC_W), np.float32)
    for h in range(DN_HEADS):
        ex[d * DN_HEADS + h, h * CHUNK:(h + 1) * CHUNK] = 1.0
        ex[2 * DN_HEADS + d * DN_HEADS + h, REC_W + h * CHUNK:REC_W + (h + 1) * CHUNK] = 1.0
    return {
        "l": causal, "cz": np.tile(causal, (1, DN_HEADS)), "hm": hm, "ex": ex,
        "a": np.concatenate(rows, axis=0),
        "lvl": np.concatenate([np.tile(mk, (1, DN_HEADS)) for mk in masks], axis=0),
    }


def _scan_constants():
    fwd, bwd = _scan_constants_dir(False), _scan_constants_dir(True)
    out = {key: jnp.asarray(np.stack([fwd[key], bwd[key]])) for key in ("l", "cz", "ex", "a", "lvl")}
    out["hm"] = jnp.asarray(fwd["hm"], BF16)
    return out


def _outproj_kernel(a_ref, dof_ref, dob_ref, z_ref, gof_ref, gob_ref, gg_ref, dnw_ref, hgw_ref, g_ref,
                    w_ref, x_ref, gate_ref, sh_ref, sc_ref, rt_ref, x1_ref, h2_ref, aff_ref):
    gmat = g_ref[...]

    def gated(o, z, nw):
        return o * lax.rsqrt(_mm_hi(o * o, gmat) + EPS) * nw * _silu(z)

    dmix = gated(dof_ref[0] + dob_ref[0], z_ref[0], dnw_ref[...])
    gmix = gated(gof_ref[0] + gob_ref[0], gg_ref[0], hgw_ref[...])
    y = (jnp.dot(a_ref[0], w_ref[0:ATTN_Q_W, :], preferred_element_type=F32)
         + jnp.dot(dmix.astype(BF16), w_ref[ATTN_Q_W:ATTN_Q_W + REC_W, :], preferred_element_type=F32)
         + jnp.dot(gmix.astype(BF16), w_ref[ATTN_Q_W + REC_W:, :], preferred_element_type=F32))
    x1 = x_ref[0] + gate_ref[0] * y
    x1_ref[0] = x1
    ms = jnp.mean(x1 * x1, axis=-1, keepdims=True)
    h2 = x1 * lax.rsqrt(ms + EPS) * (1.0 + sc_ref[0]) + sh_ref[0]
    h2_ref[0] = h2.astype(BF16)
    logits = lax.dot_general(rt_ref[...], h2, (((1,), (1,)), ((), ())), precision=HIGHEST,
                             preferred_element_type=F32)
    e = jnp.exp(logits - jnp.max(logits, axis=0, keepdims=True))
    aff_ref[0] = e / jnp.sum(e, axis=0, keepdims=True)


def _output_projection(a, dof, dob, dn, gof, gob, hg, dnw, hgw, gavg, w_out, x, gate, shift, scale, router_t):
    b, t, d = x.shape
    tm = _tile(t, 512)
    n = REC_W
    per_b = gate.shape[0] > 1
    mod_map = (lambda bi, i: (bi, 0, 0)) if per_b else (lambda bi, i: (0, 0, 0))
    const = lambda bi, i: (0, 0)
    tok = lambda bi, i: (bi, i, 0)
    return pl.pallas_call(
        _outproj_kernel,
        grid=(b, t // tm),
        in_specs=[pl.BlockSpec((1, tm, ATTN_Q_W), tok),
                  pl.BlockSpec((1, tm, n), tok), pl.BlockSpec((1, tm, n), tok),
                  pl.BlockSpec((1, tm, n), lambda bi, i: (bi, i, 3)),
                  pl.BlockSpec((1, tm, n), tok), pl.BlockSpec((1, tm, n), tok),
                  pl.BlockSpec((1, tm, n), lambda bi, i: (bi, i, 4)),
                  pl.BlockSpec((1, n), const), pl.BlockSpec((1, n), const),
                  pl.BlockSpec((n, n), const),
                  pl.BlockSpec((d, d), const),
                  pl.BlockSpec((1, tm, d), tok),
                  pl.BlockSpec((1, 1, d), mod_map), pl.BlockSpec((1, 1, d), mod_map),
                  pl.BlockSpec((1, 1, d), mod_map),
                  pl.BlockSpec((N_EXPERTS, d), const)],
        out_specs=(pl.BlockSpec((1, tm, d), tok), pl.BlockSpec((1, tm, d), tok),
                   pl.BlockSpec((1, N_EXPERTS, tm), lambda bi, i: (bi, 0, i))),
        out_shape=(jax.ShapeDtypeStruct((b, t, d), F32), jax.ShapeDtypeStruct((b, t, d), BF16),
                   jax.ShapeDtypeStruct((b, N_EXPERTS, t), F32)),
        compiler_params=_cparams(("parallel", "parallel")),
        name="output_projection_router",
    )(a, dof, dob, dn, gof, gob, hg, dnw, hgw, gavg, w_out, x, gate, shift, scale, router_t)


def _select_kernel(aff_ref, u_ref, bs_ref, bst_ref, su_ref, slot_ref, base_ref, nsub_ref, inc_scr, *, cap, nblk):
    x = aff_ref[0]
    bits = pltpu.bitcast(x, jnp.int32)
    lo = jnp.zeros((N_EXPERTS, 1), jnp.int32)
    for bit in range(30, -1, -1):
        cand = lo | (1 << bit)
        cnt = jnp.sum((bits >= cand).astype(jnp.int32), axis=1, keepdims=True)
        lo = jnp.where(cnt >= cap, cand, lo)
    gt = bits > lo
    eq = bits == lo
    umat = u_ref[...]

    def prefix(mask):
        mb = mask.astype(BF16)
        for j in range(nblk):
            inc_scr[:, j * LANES:(j + 1) * LANES] = jnp.dot(mb[:, j * LANES:(j + 1) * LANES], umat,
                                                            preferred_element_type=F32)
        totals = jnp.dot(mb, bs_ref[...], preferred_element_type=F32)
        offs = _mm_hi(totals, su_ref[...])
        return inc_scr[...] + _mm_hi(offs, bst_ref[...]), offs, totals

    eqf = eq.astype(F32)
    n_gt = jnp.sum(gt.astype(F32), axis=1, keepdims=True)
    eq_before, _, _ = prefix(eqf)
    sel = gt | (eq & ((eq_before - eqf) < (cap - n_gt)))
    self_ = sel.astype(F32)
    pos, offs, totals = prefix(self_)
    slot_ref[0] = jnp.where(sel, pos - 1.0, -1.0).astype(jnp.int32)
    start = jnp.floor(offs * (1.0 / ROUTE_ALIGN)) * ROUTE_ALIGN
    span = offs + totals - start
    n_sub = functools.reduce(jnp.add, [jnp.where(span > j * ROUTE_SUB, 1.0, 0.0)
                                       for j in range(ROUTE_WIN // ROUTE_SUB)])
    base_ref[0] = start.astype(jnp.int32)
    nsub_ref[0] = jnp.max(n_sub, axis=0, keepdims=True).astype(jnp.int32)


def _moe_select(aff_t, cap):
    b, e, t = aff_t.shape
    nblk = t // ROUTE_BLK
    u = jnp.asarray(np.triu(np.ones((LANES, LANES), np.float32)), BF16)
    blk = np.arange(t) // ROUTE_BLK
    bs = (blk[:, None] == np.arange(nblk)[None, :]).astype(np.float32)
    su = np.triu(np.ones((nblk, nblk), np.float32), 1)
    const = lambda bi: (0, 0)
    return pl.pallas_call(
        functools.partial(_select_kernel, cap=cap, nblk=nblk),
        grid=(b,),
        in_specs=[pl.BlockSpec((1, e, t), lambda bi: (bi, 0, 0)),
                  pl.BlockSpec((LANES, LANES), const),
                  pl.BlockSpec((t, nblk), const),
                  pl.BlockSpec((nblk, t), const),
                  pl.BlockSpec((nblk, nblk), const)],
        out_specs=(pl.BlockSpec((1, e, t), lambda bi: (bi, 0, 0)),
                   pl.BlockSpec((1, e, nblk), lambda bi: (bi, 0, 0)),
                   pl.BlockSpec((1, 1, nblk), lambda bi: (bi, 0, 0))),
        out_shape=(jax.ShapeDtypeStruct((b, e, t), jnp.int32), jax.ShapeDtypeStruct((b, e, nblk), jnp.int32),
                   jax.ShapeDtypeStruct((b, 1, nblk), jnp.int32)),
        scratch_shapes=[pltpu.VMEM((e, t), F32)],
        compiler_params=_cparams(("parallel",)),
        name="moe_select",
    )(aff_t, u, jnp.asarray(bs, BF16), jnp.asarray(bs.T), jnp.asarray(su))


def _gather_kernel(start_sm, nsub_sm, slot_ref, h_ref, xe_ref, *, nblk):
    bi = pl.program_id(0)
    xe_ref[0] = jnp.zeros(xe_ref.shape[1:], BF16)
    row = lax.broadcasted_iota(jnp.int32, (ROUTE_SUB, ROUTE_BLK), 0)

    def body(k, carry):
        t0 = pl.multiple_of(k * ROUTE_BLK, ROUTE_BLK)
        hblk = h_ref[0, pl.ds(t0, ROUTE_BLK), :]
        slots = slot_ref[0, :, pl.ds(t0, ROUTE_BLK)]
        firsts = [start_sm[(bi * N_EXPERTS + e) * nblk + k] for e in range(N_EXPERTS)]

        def sub(j, inner):
            starts = [pl.multiple_of(first + j * ROUTE_SUB, ROUTE_ALIGN) for first in firsts]
            onehot = jnp.concatenate(
                [jnp.where(row == slots[e:e + 1, :] - starts[e], 1.0, 0.0).astype(BF16)
                 for e in range(N_EXPERTS)], axis=0)
            rows = jnp.dot(onehot, hblk, preferred_element_type=F32).astype(BF16)
            for e in range(N_EXPERTS):
                win = pl.ds(starts[e], ROUTE_SUB)
                xe_ref[0, e, win, :] = xe_ref[0, e, win, :] + rows[e * ROUTE_SUB:(e + 1) * ROUTE_SUB]
            return inner

        lax.fori_loop(0, nsub_sm[bi * nblk + k], sub, 0)
        return carry

    lax.fori_loop(0, nblk, body, 0)


def _moe_gather(h2, slot, start_flat, nsub_flat, cap):
    b, t, d = h2.shape
    nblk = t // ROUTE_BLK
    cp = cap + ROUTE_WIN
    dq = _tile(d, 256)
    grid_spec = pltpu.PrefetchScalarGridSpec(
        num_scalar_prefetch=2,
        grid=(b, d // dq),
        in_specs=[pl.BlockSpec((1, N_EXPERTS, t), lambda bi, j, s0, s1: (bi, 0, 0)),
                  pl.BlockSpec((1, t, dq), lambda bi, j, s0, s1: (bi, 0, j))],
        out_specs=pl.BlockSpec((1, N_EXPERTS, cp, dq), lambda bi, j, s0, s1: (bi, 0, 0, j)),
    )
    return pl.pallas_call(
        functools.partial(_gather_kernel, nblk=nblk),
        grid_spec=grid_spec,
        out_shape=jax.ShapeDtypeStruct((b, N_EXPERTS, cp, d), BF16),
        compiler_params=_cparams(("parallel", "parallel")),
        name="moe_gather",
    )(start_flat, nsub_flat, slot, h2)


def _ffn_kernel(x_ref, wg_ref, wu_ref, wd_ref, y_ref, *, cap):
    x = x_ref[0, 0, 0:cap, :]
    a = jnp.dot(x, wg_ref[0], preferred_element_type=F32)
    u = jnp.dot(x, wu_ref[0], preferred_element_type=F32)
    y_ref[0, 0, 0:cap, :] = jnp.dot((_silu(a) * u).astype(BF16), wd_ref[0],
                                    preferred_element_type=F32).astype(BF16)
    y_ref[0, 0, cap:, :] = jnp.zeros((y_ref.shape[2] - cap, y_ref.shape[3]), BF16)


def _moe_ffn(xe, wg, wu, wd, cap):
    b, e, cp, d = xe.shape
    f = wg.shape[-1]
    return pl.pallas_call(
        functools.partial(_ffn_kernel, cap=cap),
        grid=(e, b),
        in_specs=[pl.BlockSpec((1, 1, cp, d), lambda ei, bi: (bi, ei, 0, 0)),
                  pl.BlockSpec((1, d, f), lambda ei, bi: (ei, 0, 0)),
                  pl.BlockSpec((1, d, f), lambda ei, bi: (ei, 0, 0)),
                  pl.BlockSpec((1, f, d), lambda ei, bi: (ei, 0, 0))],
        out_specs=pl.BlockSpec((1, 1, cp, d), lambda ei, bi: (bi, ei, 0, 0)),
        out_shape=jax.ShapeDtypeStruct((b, e, cp, d), BF16),
        compiler_params=_cparams(("parallel", "parallel")),
        name="moe_ffn",
    )(xe, wg, wu, wd)


def _combine_kernel(start_sm, nsub_sm, ye_ref, x_ref, gate_ref, slot_ref, aff_ref, o_ref, *, nblk):
    bi, k = pl.program_id(0), pl.program_id(2)
    row = lax.broadcasted_iota(jnp.int32, (ROUTE_SUB, ROUTE_BLK), 0)
    slots = slot_ref[0]
    aff = aff_ref[0]
    firsts = [start_sm[(bi * N_EXPERTS + e) * nblk + k] for e in range(N_EXPERTS)]

    def sub(j, acc):
        starts = [pl.multiple_of(first + j * ROUTE_SUB, ROUTE_ALIGN) for first in firsts]
        wsel = jnp.concatenate([jnp.where(row == slots[e:e + 1, :] - starts[e], aff[e:e + 1, :], 0.0)
                                for e in range(N_EXPERTS)], axis=0)
        ys = jnp.concatenate([ye_ref[0, e, pl.ds(starts[e], ROUTE_SUB), :] for e in range(N_EXPERTS)], axis=0)
        hi, lo = _split2(wsel)
        tn = (((0,), (0,)), ((), ()))
        return (acc + lax.dot_general(hi, ys, tn, preferred_element_type=F32)
                + lax.dot_general(lo, ys, tn, preferred_element_type=F32))

    acc = lax.fori_loop(0, nsub_sm[bi * nblk + k], sub, jnp.zeros(x_ref.shape[1:], F32))
    o_ref[0] = x_ref[0] + gate_ref[0] * acc


def _moe_combine(ye, x1, gate, slot, aff_t, start_flat, nsub_flat):
    b, t, d = x1.shape
    cp = ye.shape[2]
    nblk = t // ROUTE_BLK
    dh = d // 2
    per_b = gate.shape[0] > 1
    grid_spec = pltpu.PrefetchScalarGridSpec(
        num_scalar_prefetch=2,
        grid=(b, 2, nblk),
        in_specs=[pl.BlockSpec((1, N_EXPERTS, cp, dh), lambda bi, j, k, s0, s1: (bi, 0, 0, j)),
                  pl.BlockSpec((1, ROUTE_BLK, dh), lambda bi, j, k, s0, s1: (bi, k, j)),
                  pl.BlockSpec((1, 1, dh), (lambda bi, j, k, s0, s1: (bi, 0, j)) if per_b
                               else (lambda bi, j, k, s0, s1: (0, 0, j))),
                  pl.BlockSpec((1, N_EXPERTS, ROUTE_BLK), lambda bi, j, k, s0, s1: (bi, 0, k)),
                  pl.BlockSpec((1, N_EXPERTS, ROUTE_BLK), lambda bi, j, k, s0, s1: (bi, 0, k))],
        out_specs=pl.BlockSpec((1, ROUTE_BLK, dh), lambda bi, j, k, s0, s1: (bi, k, j)),
    )
    return pl.pallas_call(
        functools.partial(_combine_kernel, nblk=nblk),
        grid_spec=grid_spec,
        out_shape=jax.ShapeDtypeStruct((b, t, d), F32),
        compiler_params=_cparams(("parallel", "parallel", "arbitrary")),
        name="moe_combine",
    )(start_flat, nsub_flat, ye, x1, gate, slot, aff_t)


def _expert_choice_ffn(x1, h2, aff_t, gate, wg, wu, wd):
    b, t, _ = x1.shape
    cap = CAPACITY_FACTOR * t // N_EXPERTS
    slot, start, nsub = _moe_select(aff_t, cap)
    start_flat, nsub_flat = start.reshape(-1), nsub.reshape(-1)
    xe = _moe_gather(h2, slot, start_flat, nsub_flat, cap)
    ye = _moe_ffn(xe, wg, wu, wd, cap)
    return _moe_combine(ye, x1, gate, slot, aff_t, start_flat, nsub_flat)


def _rope_tables(n_tokens):
    rows = n_tokens // GRID_W
    row = jnp.repeat(jnp.arange(rows, dtype=F32), GRID_W)
    col = jnp.tile(jnp.arange(GRID_W, dtype=F32), rows)
    n_freq = HEAD_DIM // 4
    inv_freq = ROPE_THETA ** (-jnp.arange(n_freq, dtype=F32) / n_freq)
    ang_r = row[:, None] * inv_freq
    ang_c = col[:, None] * inv_freq
    cos = jnp.concatenate([jnp.cos(ang_r)] * 2 + [jnp.cos(ang_c)] * 2, axis=-1)
    sin = jnp.concatenate([jnp.sin(ang_r)] * 2 + [jnp.sin(ang_c)] * 2, axis=-1)
    return jnp.tile(cos, (1, LANES // HEAD_DIM)), jnp.tile(sin, (1, LANES // HEAD_DIM))


def _block_diag(width, block, value):
    idx = np.arange(width) // block
    return jnp.asarray((idx[:, None] == idx[None, :]).astype(np.float32) * value)


def kernel(x, c, ctx, c_ctx, w_mod, b_mod, w_in, w_out, attn_q_norm, attn_k_norm, dn_conv, dn_a_log,
           dn_dt_bias, dn_norm, hg_lower_bounds, hg_norm, moe_router, moe_w_gate, moe_w_up, moe_w_down):
    depth = w_mod.shape[0]
    b, t_lat, d = x.shape
    cos, sin = _rope_tables(t_lat)
    g_head = _block_diag(LANES, HEAD_DIM, 1.0 / HEAD_DIM)
    g_mean = _block_diag(REC_W, HEAD_DIM, 1.0 / HEAD_DIM)
    g_sum = _block_diag(REC_W, HEAD_DIM, 1.0)
    consts = _scan_constants()
    s_zero = jnp.zeros((b, 2, REC_W, REC_W), F32)

    lb_w = jax.nn.softmax(hg_lower_bounds.astype(F32), axis=0)
    hg_lb = jnp.cumsum(lb_w, axis=0) - lb_w[0]

    rows = ((b + 1 + 7) // 8) * 8
    cond = jnp.zeros((rows, d), F32).at[:b].set(c).at[b].set(c_ctx)
    mod = _modulation(cond, w_mod, b_mod)

    n_small = 4 * DN_HEADS
    w_in_r = jnp.concatenate(
        [w_in[:, :, :C_HG], w_in[:, :, C_HG + n_small:], w_in[:, :, C_HG:C_HG + n_small],
         jnp.zeros((depth, d, C_END - C_BA - n_small), w_in.dtype)], axis=-1).astype(BF16)
    w_out_b = w_out.astype(BF16)
    wg_b, wu_b, wd_b = moe_w_gate.astype(BF16), moe_w_up.astype(BF16), moe_w_down.astype(BF16)

    x_lat, x_ctx = x, ctx
    for l in range(depth):
        ctx_out = l < depth - 1
        m_lat = [mod[l, :b, j * d:(j + 1) * d][:, None, :] for j in range(6)]
        m_ctx = [mod[l, b:b + 1, j * d:(j + 1) * d][:, None, :] for j in range(6)]
        qn = jnp.tile(attn_q_norm[l], LANES // HEAD_DIM)[None, :]
        kn = jnp.tile(attn_k_norm[l], LANES // HEAD_DIM)[None, :]
        conv_w = jnp.zeros((8, 3 * REC_W), F32).at[:CONV_K].set(dn_conv[l])
        pad = jnp.zeros((LANES - 4 * DN_HEADS,), F32)
        neg_a = jnp.concatenate([jnp.zeros((2 * DN_HEADS,), F32), -jnp.exp(dn_a_log[l].reshape(-1)), pad])[None, :]
        dt_b = jnp.concatenate([jnp.zeros((2 * DN_HEADS,), F32), dn_dt_bias[l].reshape(-1), pad])[None, :]
        dnw = jnp.tile(dn_norm[l], DN_HEADS)[None, :]
        hgw = jnp.tile(hg_norm[l], HG_HEADS)[None, :]
        router_t = moe_router[l].T

        streams = {}
        for name, xs, ms, rotate in (("ctx", x_ctx, m_ctx, False), ("lat", x_lat, m_lat, True)):
            t = xs.shape[1]
            q, kt, v, dn, hg, ba = _input_projection(xs, ms[0], ms[1], w_in_r[l], qn, kn,
                                                     cos[:t], sin[:t], g_head, rotate)
            qkv, gb = _dn_prep(dn, ba, conv_w, neg_a, dt_b, g_sum)
            streams[name] = dict(q=q, kv=(kt, v), dn=dn, hg=hg, qkv=qkv, gb=gb)

        sc, sl = streams["ctx"], streams["lat"]
        dcf, dcb, dn_state = _dn_scan(sc["qkv"], sc["gb"], s_zero, consts)
        dlf, dlb, _ = _dn_scan(sl["qkv"], sl["gb"], dn_state, consts)
        gcf, gcb, hg_state = _hg_scan(sc["hg"], hg_lb[l], s_zero, consts)
        glf, glb, _ = _hg_scan(sl["hg"], hg_lb[l], hg_state, consts)

        a_lat = _attention(sl["q"], [sc["kv"], sl["kv"]])
        x1, h2, aff_t = _output_projection(
            a_lat, dlf, dlb, sl["dn"], glf, glb, sl["hg"], dnw, hgw, g_mean, w_out_b[l], x_lat,
            m_lat[2], m_lat[3], m_lat[4], router_t)
        x_lat = _expert_choice_ffn(x1, h2, aff_t, m_lat[5], wg_b[l], wu_b[l], wd_b[l])
        if ctx_out:
            a_ctx = _attention(sc["q"], [sc["kv"]])
            x1, h2, aff_t = _output_projection(
                a_ctx, dcf, dcb, sc["dn"], gcf, gcb, sc["hg"], dnw, hgw, g_mean, w_out_b[l], x_ctx,
                m_ctx[2], m_ctx[3], m_ctx[4], router_t)
            x_ctx = _expert_choice_ffn(x1, h2, aff_t, m_ctx[5], wg_b[l], wu_b[l], wd_b[l])
    return x_lat
```

```python
import functools

import numpy as np
import jax
import jax.numpy as jnp
from jax import lax
from jax.experimental import pallas as pl
from jax.experimental.pallas import tpu as pltpu

F32 = jnp.float32
BF16 = jnp.bfloat16
HIGHEST = lax.Precision.HIGHEST

HEAD_DIM = 64
N_Q_HEADS = 8
N_KV_HEADS = 2
GQA_GROUP = N_Q_HEADS // N_KV_HEADS
DN_HEADS = 4
HG_HEADS = 4
GRID_W = 64
ROPE_THETA = 10000.0
CONV_K = 5
CHUNK = 64
N_EXPERTS = 16
CAPACITY_FACTOR = 2
EPS = 1e-6
LOG2E = 1.4426950408889634

ATTN_Q_W = N_Q_HEADS * HEAD_DIM
ATTN_KV_W = N_KV_HEADS * HEAD_DIM
REC_W = DN_HEADS * HEAD_DIM
LANES = 128
ROUTE_BLK = LANES
ROUTE_ALIGN = 16
ROUTE_WIN = ROUTE_BLK + ROUTE_ALIGN
COMBINE_BLOCKS = 4
ROUTE_SUB = ROUTE_WIN // 3
N_LEVELS = 6
SCAN_BATCH = 4
VMEM_LIMIT = 56 * 1024 * 1024

C_Q, C_K, C_V, C_DN, C_HG, C_BA, C_END = 0, 512, 640, 768, 1792, 3072, 3200


def _mm(a, b):
    return jnp.dot(a.astype(BF16), b.astype(BF16), preferred_element_type=F32)


def _mm_nt(a, b):
    return lax.dot_general(a.astype(BF16), b.astype(BF16), (((1,), (1,)), ((), ())),
                           preferred_element_type=F32)


def _mm_tn(a, b):
    return lax.dot_general(a.astype(BF16), b.astype(BF16), (((0,), (0,)), ((), ())),
                           preferred_element_type=F32)


def _mm_hi(a, b):
    return jnp.dot(a, b, precision=HIGHEST, preferred_element_type=F32)


def _sigmoid(x):
    return 1.0 / (1.0 + jnp.exp(-x))


def _silu(x):
    return x * _sigmoid(x)


def _cparams(sem):
    return pltpu.CompilerParams(dimension_semantics=sem, vmem_limit_bytes=VMEM_LIMIT)


def _tile(n, pref):
    return pref if n % pref == 0 else n


def _cast_kernel(x_ref, o_ref):
    o_ref[...] = x_ref[...].astype(BF16)


def _cast_bf16(w):
    r, c = w.shape[-2:]
    w3 = w.reshape(-1, r, c)
    rb = _tile(r, 512)
    out = pl.pallas_call(
        _cast_kernel,
        grid=(w3.shape[0], r // rb),
        in_specs=[pl.BlockSpec((1, rb, c), lambda i, j: (i, j, 0))],
        out_specs=pl.BlockSpec((1, rb, c), lambda i, j: (i, j, 0)),
        out_shape=jax.ShapeDtypeStruct(w3.shape, BF16),
        compiler_params=_cparams(("parallel", "parallel")),
        name="cast_bf16",
    )(w3)
    return out.reshape(w.shape)


def _mod_kernel(c_ref, w_ref, b_ref, o_ref):
    o_ref[0] = _mm_hi(_silu(c_ref[...]), w_ref[0]) + b_ref[0]


def _modulation(cond, w_mod, b_mod):
    depth, d, n = w_mod.shape
    rows = cond.shape[0]
    tn = _tile(n, 1536)
    return pl.pallas_call(
        _mod_kernel,
        grid=(depth, n // tn),
        in_specs=[pl.BlockSpec((rows, d), lambda l, j: (0, 0)),
                  pl.BlockSpec((1, d, tn), lambda l, j: (l, 0, j)),
                  pl.BlockSpec((1, 1, tn), lambda l, j: (l, 0, j))],
        out_specs=pl.BlockSpec((1, rows, tn), lambda l, j: (l, 0, j)),
        out_shape=jax.ShapeDtypeStruct((depth, rows, n), F32),
        compiler_params=_cparams(("parallel", "parallel")),
        name="modulation",
    )(cond, w_mod, b_mod.reshape(depth, 1, n))


def _inproj_kernel(x_ref, sh_ref, sc_ref, w_ref, qn_ref, kn_ref, cos_ref, sin_ref, g_ref,
                   q_ref, k_ref, v_ref, dn_ref, hg_ref, ba_ref, *, rotate):
    x = x_ref[0]
    ms = jnp.mean(x * x, axis=-1, keepdims=True)
    h = x * lax.rsqrt(ms + EPS) * (1.0 + sc_ref[0]) + sh_ref[0]
    hb = h.astype(BF16)
    gmat = g_ref[...]

    def proj(lo, hi):
        return jnp.dot(hb, w_ref[:, lo:hi], preferred_element_type=F32)

    def head_norm_rope(t, nw):
        t = t * lax.rsqrt(_mm_hi(t * t, gmat) + EPS) * nw
        if rotate:
            lane = lax.broadcasted_iota(jnp.int32, t.shape, 1)
            first = (lane % 32) < 16
            rot = jnp.where(first, -pltpu.roll(t, LANES - 16, 1), pltpu.roll(t, 16, 1))
            t = t * cos_ref[...] + rot * sin_ref[...]
        return t

    def q_group(j):
        t = proj(C_Q + j * LANES, C_Q + (j + 1) * LANES)
        yield
        qj = head_norm_rope(t, qn_ref[...])
        q_ref[0, :, j * LANES:(j + 1) * LANES] = (qj * (HEAD_DIM ** -0.5 * LOG2E)).astype(BF16)

    def k_group():
        t = proj(C_K, C_V)
        yield
        kt = head_norm_rope(t, kn_ref[...]).T
        k_ref[0, 0] = kt[:HEAD_DIM].astype(BF16)
        k_ref[0, 1] = kt[HEAD_DIM:].astype(BF16)

    def v_group():
        v = proj(C_V, C_DN)
        yield
        ones = jnp.ones((v.shape[0], LANES - HEAD_DIM), F32)
        v_ref[0, 0] = jnp.concatenate([v[:, :HEAD_DIM], ones], axis=1).astype(BF16)
        v_ref[0, 1] = jnp.concatenate([v[:, HEAD_DIM:], ones], axis=1).astype(BF16)

    def plain_group(o_ref, base, lo, hi):
        t = proj(base + lo, base + hi)
        yield
        o_ref[0, :, lo:hi] = t

    step = 4 * LANES
    plain = [plain_group(o_ref, base, lo, min(lo + step, width))
             for o_ref, base, width in ((dn_ref, C_DN, C_HG - C_DN), (hg_ref, C_HG, C_BA - C_HG),
                                        (ba_ref, C_BA, C_END - C_BA))
             for lo in range(0, width, step)]
    _lockstep([q_group(j) for j in range(ATTN_Q_W // LANES)] + [k_group(), v_group()] + plain, stagger=True)


def _input_projection(x, shift, scale, w, qn, kn, cos, sin, gmat, rotate):
    b, t, d = x.shape
    tm = _tile(t, 512)
    per_b = shift.shape[0] > 1
    mod_map = (lambda bi, i: (bi, 0, 0)) if per_b else (lambda bi, i: (0, 0, 0))
    const = lambda bi, i: (0, 0)
    out_shape = (
        jax.ShapeDtypeStruct((b, t, ATTN_Q_W), BF16),
        jax.ShapeDtypeStruct((b, N_KV_HEADS, HEAD_DIM, t), BF16),
        jax.ShapeDtypeStruct((b, N_KV_HEADS, t, LANES), BF16),
        jax.ShapeDtypeStruct((b, t, C_HG - C_DN), F32),
        jax.ShapeDtypeStruct((b, t, C_BA - C_HG), F32),
        jax.ShapeDtypeStruct((b, t, C_END - C_BA), F32),
    )
    return pl.pallas_call(
        functools.partial(_inproj_kernel, rotate=rotate),
        grid=(b, t // tm),
        in_specs=[pl.BlockSpec((1, tm, d), lambda bi, i: (bi, i, 0)),
                  pl.BlockSpec((1, 1, d), mod_map),
                  pl.BlockSpec((1, 1, d), mod_map),
                  pl.BlockSpec((d, C_END), const),
                  pl.BlockSpec((1, LANES), const),
                  pl.BlockSpec((1, LANES), const),
                  pl.BlockSpec((tm, LANES), lambda bi, i: (i, 0)),
                  pl.BlockSpec((tm, LANES), lambda bi, i: (i, 0)),
                  pl.BlockSpec((LANES, LANES), const)],
        out_specs=(pl.BlockSpec((1, tm, ATTN_Q_W), lambda bi, i: (bi, i, 0)),
                   pl.BlockSpec((1, N_KV_HEADS, HEAD_DIM, tm), lambda bi, i: (bi, 0, 0, i)),
                   pl.BlockSpec((1, N_KV_HEADS, tm, LANES), lambda bi, i: (bi, 0, i, 0)),
                   pl.BlockSpec((1, tm, C_HG - C_DN), lambda bi, i: (bi, i, 0)),
                   pl.BlockSpec((1, tm, C_BA - C_HG), lambda bi, i: (bi, i, 0)),
                   pl.BlockSpec((1, tm, C_END - C_BA), lambda bi, i: (bi, i, 0))),
        out_shape=out_shape,
        compiler_params=_cparams(("parallel", "parallel")),
        name="input_projection",
    )(x, shift, scale, w, qn, kn, cos, sin, gmat)


def _lockstep(gens, stagger=False):
    results = [None] * len(gens)
    live, started = [], 0
    while live or started < len(gens):
        fresh = 1 if stagger else len(gens)
        live += list(range(started, min(started + fresh, len(gens))))
        started = min(started + fresh, len(gens))
        for i in reversed(list(live)):
            try:
                next(gens[i])
            except StopIteration as stop:
                results[i] = stop.value
                live.remove(i)
    return results


def _attn_head(qh, kts, vs):
    ss = [jnp.dot(qh, kt, preferred_element_type=F32).astype(BF16) for kt in kts]
    yield
    m = functools.reduce(jnp.maximum, [jnp.max(s, axis=-1, keepdims=True) for s in ss])
    ps = [jnp.exp2(s - m) for s in ss]
    yield
    o = functools.reduce(jnp.add, [jnp.dot(p, v[0, 0], preferred_element_type=F32) for p, v in zip(ps, vs)])
    return o[:, :HEAD_DIM] / o[:, HEAD_DIM:HEAD_DIM + 1]


def _attn_kernel(*refs, n_src):
    q_ref, o_ref = refs[0], refs[-1]
    kts = [refs[1 + 2 * i][0, 0] for i in range(n_src)]
    vs = [refs[2 + 2 * i] for i in range(n_src)]
    q = q_ref[0]
    outs = _lockstep([_attn_head(q[:, h * HEAD_DIM:(h + 1) * HEAD_DIM], kts, vs) for h in range(GQA_GROUP)],
                     stagger=True)
    o_ref[0] = jnp.concatenate(outs, axis=-1).astype(BF16)


def _attention(q, sources):
    b, t, _ = q.shape
    tq = _tile(t, 256)
    gw = GQA_GROUP * HEAD_DIM
    in_specs = [pl.BlockSpec((1, tq, gw), lambda bi, g, i: (bi, i, g))]
    args = [q]
    for kt, v in sources:
        tk = kt.shape[-1]
        in_specs.append(pl.BlockSpec((1, 1, HEAD_DIM, tk), lambda bi, g, i: (bi, g, 0, 0)))
        in_specs.append(pl.BlockSpec((1, 1, tk, LANES), lambda bi, g, i: (bi, g, 0, 0)))
        args += [kt, v]
    return pl.pallas_call(
        functools.partial(_attn_kernel, n_src=len(sources)),
        grid=(b, N_KV_HEADS, t // tq),
        in_specs=in_specs,
        out_specs=pl.BlockSpec((1, tq, gw), lambda bi, g, i: (bi, i, g)),
        out_shape=jax.ShapeDtypeStruct((b, t, ATTN_Q_W), BF16),
        compiler_params=_cparams(("parallel", "parallel", "parallel")),
        name="attention",
    )(*args)


def _dn_prep_kernel(x_ref, xp_ref, xn_ref, ba_ref, cw_ref, na_ref, dtb_ref, g_ref, o_ref, gb_ref, buf):
    i = pl.program_id(1)
    n = pl.num_programs(1)
    tm = x_ref.shape[1]
    buf[0:8] = jnp.where(i > 0, xp_ref[0], 0.0)
    buf[8:8 + tm] = x_ref[0]
    buf[8 + tm:16 + tm] = jnp.where(i < n - 1, xn_ref[0], 0.0)
    half = CONV_K // 2
    y = cw_ref[0:1, :] * buf[8 - half:8 - half + tm]
    for j in range(1, CONV_K):
        y = y + cw_ref[j:j + 1, :] * buf[8 - half + j:8 - half + j + tm]
    y = _silu(y)
    gmat = g_ref[...]
    q = y[:, :REC_W]
    k = y[:, REC_W:2 * REC_W]
    o_ref[0, :, 0:REC_W] = q * lax.rsqrt(_mm_hi(q * q, gmat) + EPS) * HEAD_DIM ** -0.5
    o_ref[0, :, REC_W:2 * REC_W] = k * lax.rsqrt(_mm_hi(k * k, gmat) + EPS)
    o_ref[0, :, 2 * REC_W:] = y[:, 2 * REC_W:]
    ba = ba_ref[0]
    z = ba + dtb_ref[...]
    softplus = jnp.maximum(z, 0.0) + jnp.log1p(jnp.exp(-jnp.abs(z)))
    lane = lax.broadcasted_iota(jnp.int32, ba.shape, 1)
    gb_ref[0] = jnp.where(lane < 2 * DN_HEADS, _sigmoid(ba), na_ref[...] * softplus)


def _dn_prep(dn, ba, conv_w, neg_a, dt_bias, gsum):
    b, t, _ = dn.shape
    tm = _tile(t, 512)
    w3 = 3 * REC_W
    nb8 = t // 8
    return pl.pallas_call(
        _dn_prep_kernel,
        grid=(b, t // tm),
        in_specs=[pl.BlockSpec((1, tm, w3), lambda bi, i: (bi, i, 0)),
                  pl.BlockSpec((1, 8, w3), lambda bi, i: (bi, jnp.maximum(i * (tm // 8) - 1, 0), 0)),
                  pl.BlockSpec((1, 8, w3), lambda bi, i: (bi, jnp.minimum((i + 1) * (tm // 8), nb8 - 1), 0)),
                  pl.BlockSpec((1, tm, LANES), lambda bi, i: (bi, i, 0)),
                  pl.BlockSpec((8, w3), lambda bi, i: (0, 0)),
                  pl.BlockSpec((1, LANES), lambda bi, i: (0, 0)),
                  pl.BlockSpec((1, LANES), lambda bi, i: (0, 0)),
                  pl.BlockSpec((REC_W, REC_W), lambda bi, i: (0, 0))],
        out_specs=(pl.BlockSpec((1, tm, w3), lambda bi, i: (bi, i, 0)),
                   pl.BlockSpec((1, tm, LANES), lambda bi, i: (bi, i, 0))),
        out_shape=(jax.ShapeDtypeStruct((b, t, w3), F32), jax.ShapeDtypeStruct((b, t, LANES), F32)),
        scratch_shapes=[pltpu.VMEM((tm + 16, w3), F32)],
        compiler_params=_cparams(("parallel", "parallel")),
        name="deltanet_prep",
    )(dn, dn, dn, ba, conv_w, neg_a, dt_bias, gsum)


def _split2(a):
    hi = a.astype(BF16)
    return hi, (a - hi.astype(F32)).astype(BF16)


def _split3(a):
    hi = a.astype(BF16)
    r = a - hi.astype(F32)
    mid = r.astype(BF16)
    return hi, mid, (r - mid.astype(F32)).astype(BF16)


def _bdiag(a, hm):
    return jnp.concatenate([a] * DN_HEADS, axis=0) * hm


def _mm_bd(a, b, hm):
    return jnp.dot(a.astype(BF16), _bdiag(b.astype(BF16), hm), preferred_element_type=F32)


def _mm_sel_l(sel, b):
    n = b.shape[1]
    t = jnp.dot(sel.astype(BF16), jnp.concatenate(_split3(b), axis=1), preferred_element_type=F32)
    return t[:, :n] + t[:, n:2 * n] + t[:, 2 * n:]


def _mm_sel_r(a, sel):
    m = a.shape[0]
    t = jnp.dot(jnp.concatenate(_split3(a), axis=0), sel.astype(BF16), preferred_element_type=F32)
    return t[:m] + t[m:2 * m] + t[2 * m:]


def _dn_chunk(x, gb, st, ex, lmat, causal, lvl_ref, d, hm):
    n = REC_W
    lvl = lambda j: lvl_ref[d, j * CHUNK:(j + 1) * CHUNK, :]
    eye = lvl(N_LEVELS)
    ones8 = jnp.ones((8, CHUNK), F32)
    q, k, v = x[:, :n], x[:, n:2 * n], x[:, 2 * n:]
    gbx = _mm_sel_r(gb, ex)
    yield
    beta, g = gbx[:, :n], gbx[:, n:]
    cum = _mm_sel_l(lmat, g)
    last = 0 if d else CHUNK - 1
    clast = cum[last:last + 1]
    kb = k * beta
    ks = _bdiag(k.astype(BF16), hm)
    kk = lax.dot_general(kb.astype(BF16), ks, (((1,), (1,)), ((), ())), preferred_element_type=F32)
    qk = lax.dot_general(q.astype(BF16), ks, (((1,), (1,)), ((), ())), preferred_element_type=F32)
    yield
    cum_s = _mm_sel_l(ones8, cum * eye)[0:1]
    ecum = jnp.exp(cum)
    kdec = k * jnp.exp(clast - cum)
    yield
    decay = jnp.where(causal, jnp.exp(jnp.where(causal, cum - cum_s, 0.0)), 0.0)
    m = kk * decay
    attn = qk * decay
    inv = eye - lvl(N_LEVELS - 1) * m
    for lev in range(N_LEVELS - 2, -1, -1):
        half = _mm_bd(inv, lvl(lev) * m, hm)
        yield
        inv = inv - _mm_bd(half, inv, hm)
        yield
    rhs = v * beta - _mm_nt(kb * ecum, st)
    qs = _mm_nt(q * ecum, st)
    yield
    v_new = _mm_bd(inv, rhs, hm)
    yield
    out = qs + _mm_bd(attn, v_new, hm)
    st_new = st * jnp.exp(clast) + _mm_tn(v_new, kdec) * hm.astype(F32)
    return out, st_new


def _hg_chunk(qr, fr, v, lb, st, amat, lvl_ref, d, hm):
    lvl = lambda j: lvl_ref[d, j * CHUNK:(j + 1) * CHUNK, :]
    last = 0 if d else CHUNK - 1
    q = _silu(qr)
    f = lb + (1.0 - lb) * _sigmoid(fr)
    k = 1.0 - f
    g = jnp.log(f)
    ex = jnp.exp(_mm_sel_l(amat, g))
    yield

    def level(xl, mask):
        kl = _bdiag((k * xl).astype(BF16), hm)
        return lax.dot_general((q * xl).astype(BF16), kl, (((1,), (1,)), ((), ())),
                               preferred_element_type=F32) * mask

    attn = level(1.0, lvl(N_LEVELS))
    for lev in range(N_LEVELS):
        attn = attn + level(ex[(2 + lev) * CHUNK:(3 + lev) * CHUNK], lvl(lev))
    qs = _mm_nt(q * ex[0:CHUNK], st)
    vk = _mm_tn(v, k * ex[CHUNK:2 * CHUNK])
    yield
    out = _mm_bd(attn, v, hm) + qs
    st_new = st * ex[last:last + 1] + vk * hm.astype(F32)
    return out, st_new


def _dn_scan_kernel(xf_ref, xb_ref, gf_ref, gb_ref, s0_ref, ex_ref, l_ref, cz_ref, lvl_ref, hm_ref,
                    of_ref, ob_ref, sf_ref, s_scr, *, nchunk):
    @pl.when(pl.program_id(1) == 0)
    def _():
        s_scr[...] = s0_ref[...]

    hm = hm_ref[...]
    refs = ((xf_ref, gf_ref, of_ref), (xb_ref, gb_ref, ob_ref))

    def body(c, carry):
        rows = [pl.ds(pl.multiple_of(cc * CHUNK, CHUNK), CHUNK) for cc in (c, nchunk - 1 - c)]
        insts = [(bi, d) for bi in range(s_scr.shape[0]) for d in range(2)]
        res = _lockstep([_dn_chunk(refs[d][0][bi, rows[d], :], refs[d][1][bi, rows[d], :], s_scr[bi, d],
                                   ex_ref[d], l_ref[d], cz_ref[d] > 0, lvl_ref, d, hm) for bi, d in insts])
        for (bi, d), (out, s_new) in zip(insts, res):
            refs[d][2][bi, rows[d], :] = out
            s_scr[bi, d] = s_new
        return carry

    lax.fori_loop(0, nchunk, body, 0)
    sf_ref[...] = s_scr[...]


def _hg_scan_kernel(qf_ref, qb_ref, ff_ref, fb_ref, if_ref, ib_ref, lb_ref, s0_ref, a_ref, lvl_ref, hm_ref,
                    of_ref, ob_ref, sf_ref, s_scr, *, nchunk):
    @pl.when(pl.program_id(1) == 0)
    def _():
        s_scr[...] = s0_ref[...]

    hm = hm_ref[...]
    refs = ((qf_ref, ff_ref, if_ref, of_ref), (qb_ref, fb_ref, ib_ref, ob_ref))

    def body(c, carry):
        rows = [pl.ds(pl.multiple_of(cc * CHUNK, CHUNK), CHUNK) for cc in (c, nchunk - 1 - c)]
        insts = [(bi, d) for bi in range(s_scr.shape[0]) for d in range(2)]
        res = _lockstep([_hg_chunk(refs[d][0][bi, rows[d], :], refs[d][1][bi, rows[d], :],
                                   refs[d][2][bi, rows[d], :], lb_ref[d:d + 1, :], s_scr[bi, d], a_ref[d],
                                   lvl_ref, d, hm) for bi, d in insts])
        for (bi, d), (out, s_new) in zip(insts, res):
            refs[d][3][bi, rows[d], :] = out
            s_scr[bi, d] = s_new
        return carry

    lax.fori_loop(0, nchunk, body, 0)
    sf_ref[...] = s_scr[...]


def _dn_scan(qkv, gb, s0, consts):
    b, t, w3 = qkv.shape
    tb = _tile(t, 512)
    nblk = t // tb
    n = REC_W
    nb = _tile(b, SCAN_BATCH)
    fwd = lambda bi, i: (bi, i, 0)
    bwd = lambda bi, i: (bi, nblk - 1 - i, 0)
    c3 = lambda bi, i: (0, 0, 0)
    state = pl.BlockSpec((nb, 2, n, n), lambda bi, i: (bi, 0, 0, 0))
    return pl.pallas_call(
        functools.partial(_dn_scan_kernel, nchunk=tb // CHUNK),
        grid=(b // nb, nblk),
        in_specs=[pl.BlockSpec((nb, tb, w3), fwd), pl.BlockSpec((nb, tb, w3), bwd),
                  pl.BlockSpec((nb, tb, LANES), fwd), pl.BlockSpec((nb, tb, LANES), bwd),
                  state,
                  pl.BlockSpec((2, LANES, 2 * n), c3),
                  pl.BlockSpec((2, CHUNK, CHUNK), c3),
                  pl.BlockSpec((2, CHUNK, n), c3),
                  pl.BlockSpec((2, (N_LEVELS + 1) * CHUNK, n), c3),
                  pl.BlockSpec((n, n), lambda bi, i: (0, 0))],
        out_specs=(pl.BlockSpec((nb, tb, n), fwd), pl.BlockSpec((nb, tb, n), bwd), state),
        out_shape=(jax.ShapeDtypeStruct((b, t, n), F32), jax.ShapeDtypeStruct((b, t, n), F32),
                   jax.ShapeDtypeStruct((b, 2, n, n), F32)),
        scratch_shapes=[pltpu.VMEM((nb, 2, n, n), F32)],
        compiler_params=_cparams(("parallel", "arbitrary")),
        name="deltanet_scan",
    )(qkv, qkv, gb, gb, s0, consts["ex"], consts["l"], consts["cz"], consts["lvl"], consts["hm"])


def _hg_scan(hg, lb, s0, consts):
    b, t, _ = hg.shape
    tb = _tile(t, 512)
    nblk = t // tb
    n = REC_W
    nb = _tile(b, SCAN_BATCH)
    fwd = lambda col: (lambda bi, i: (bi, i, col))
    bwd = lambda col: (lambda bi, i: (bi, nblk - 1 - i, col))
    c3 = lambda bi, i: (0, 0, 0)
    blk = lambda m: pl.BlockSpec((nb, tb, n), m)
    state = pl.BlockSpec((nb, 2, n, n), lambda bi, i: (bi, 0, 0, 0))
    return pl.pallas_call(
        functools.partial(_hg_scan_kernel, nchunk=tb // CHUNK),
        grid=(b // nb, nblk),
        in_specs=[blk(fwd(0)), blk(bwd(0)), blk(fwd(1)), blk(bwd(2)), blk(fwd(3)), blk(bwd(3)),
                  pl.BlockSpec((2, n), lambda bi, i: (0, 0)),
                  state,
                  pl.BlockSpec((2, (2 + N_LEVELS) * CHUNK, CHUNK), c3),
                  pl.BlockSpec((2, (N_LEVELS + 1) * CHUNK, n), c3),
                  pl.BlockSpec((n, n), lambda bi, i: (0, 0))],
        out_specs=(blk(fwd(0)), blk(bwd(0)), state),
        out_shape=(jax.ShapeDtypeStruct((b, t, n), F32), jax.ShapeDtypeStruct((b, t, n), F32),
                   jax.ShapeDtypeStruct((b, 2, n, n), F32)),
        scratch_shapes=[pltpu.VMEM((nb, 2, n, n), F32)],
        compiler_params=_cparams(("parallel", "arbitrary")),
        name="hgrn2_scan",
    )(hg, hg, hg, hg, hg, hg, lb, s0, consts["a"], consts["lvl"], consts["hm"])


def _scan_constants_dir(rev):
    t = np.arange(CHUNK)
    p = (CHUNK - 1 - t) if rev else t
    pt, pu = p[:, None], p[None, :]
    causal = (pu <= pt).astype(np.float32)
    head = np.arange(REC_W) // CHUNK
    hm = (head[:, None] == head[None, :]).astype(np.float32)
    rows = [causal, (pu > pt).astype(np.float32)]
    masks = []
    for lev in range(N_LEVELS):
        blk = CHUNK >> (lev + 1)
        bound = (pt // (2 * blk)) * (2 * blk) + blk - 1
        right = (pt % (2 * blk)) >= blk
        rows.append(np.where(right, (pu > bound) & (pu <= pt), (pu > pt) & (pu <= bound)).astype(np.float32))
        masks.append(((pt // (2 * blk) == pu // (2 * blk)) & right & ((pu % (2 * blk)) < blk)).astype(np.float32))
    masks.append(np.eye(CHUNK, dtype=np.float32))
    d = 1 if rev else 0
    ex = np.zeros((LANES, 2 * REC_W), np.float32)
    for h in range(DN_HEADS):
        ex[d * DN_HEADS + h, h * CHUNK:(h + 1) * CHUNK] = 1.0
        ex[2 * DN_HEADS + d * DN_HEADS + h, REC_W + h * CHUNK:REC_W + (h + 1) * CHUNK] = 1.0
    return {
        "l": causal, "cz": np.tile(causal, (1, DN_HEADS)), "hm": hm, "ex": ex,
        "a": np.concatenate(rows, axis=0),
        "lvl": np.concatenate([np.tile(mk, (1, DN_HEADS)) for mk in masks], axis=0),
    }


def _scan_constants():
    fwd, bwd = _scan_constants_dir(False), _scan_constants_dir(True)
    out = {key: jnp.asarray(np.stack([fwd[key], bwd[key]])) for key in ("l", "cz", "ex", "a", "lvl")}
    out["hm"] = jnp.asarray(fwd["hm"], BF16)
    return out


def _outproj_kernel(a_ref, dof_ref, dob_ref, z_ref, gof_ref, gob_ref, gg_ref, dnw_ref, hgw_ref, g_ref,
                    w_ref, x_ref, gate_ref, sh_ref, sc_ref, rt_ref, x1_ref, h2_ref, aff_ref):
    gmat = g_ref[...]

    def gated(o, z, nw):
        return o * lax.rsqrt(_mm_hi(o * o, gmat) + EPS) * nw * _silu(z)

    dmix = gated(dof_ref[0] + dob_ref[0], z_ref[0], dnw_ref[...])
    gmix = gated(gof_ref[0] + gob_ref[0], gg_ref[0], hgw_ref[...])
    y = (jnp.dot(a_ref[0], w_ref[0:ATTN_Q_W, :], preferred_element_type=F32)
         + jnp.dot(dmix.astype(BF16), w_ref[ATTN_Q_W:ATTN_Q_W + REC_W, :], preferred_element_type=F32)
         + jnp.dot(gmix.astype(BF16), w_ref[ATTN_Q_W + REC_W:, :], preferred_element_type=F32))
    x1 = x_ref[0] + gate_ref[0] * y
    x1_ref[0] = x1
    ms = jnp.mean(x1 * x1, axis=-1, keepdims=True)
    h2 = x1 * lax.rsqrt(ms + EPS) * (1.0 + sc_ref[0]) + sh_ref[0]
    h2_ref[0] = h2.astype(BF16)
    logits = lax.dot_general(rt_ref[...], h2, (((1,), (1,)), ((), ())), precision=HIGHEST,
                             preferred_element_type=F32)
    e = jnp.exp(logits - jnp.max(logits, axis=0, keepdims=True))
    aff_ref[0] = e / jnp.sum(e, axis=0, keepdims=True)


def _output_projection(a, dof, dob, dn, gof, gob, hg, dnw, hgw, gavg, w_out, x, gate, shift, scale, router_t):
    b, t, d = x.shape
    tm = _tile(t, 512)
    n = REC_W
    per_b = gate.shape[0] > 1
    mod_map = (lambda bi, i: (bi, 0, 0)) if per_b else (lambda bi, i: (0, 0, 0))
    const = lambda bi, i: (0, 0)
    tok = lambda bi, i: (bi, i, 0)
    return pl.pallas_call(
        _outproj_kernel,
        grid=(b, t // tm),
        in_specs=[pl.BlockSpec((1, tm, ATTN_Q_W), tok),
                  pl.BlockSpec((1, tm, n), tok), pl.BlockSpec((1, tm, n), tok),
                  pl.BlockSpec((1, tm, n), lambda bi, i: (bi, i, 3)),
                  pl.BlockSpec((1, tm, n), tok), pl.BlockSpec((1, tm, n), tok),
                  pl.BlockSpec((1, tm, n), lambda bi, i: (bi, i, 4)),
                  pl.BlockSpec((1, n), const), pl.BlockSpec((1, n), const),
                  pl.BlockSpec((n, n), const),
                  pl.BlockSpec((d, d), const),
                  pl.BlockSpec((1, tm, d), tok),
                  pl.BlockSpec((1, 1, d), mod_map), pl.BlockSpec((1, 1, d), mod_map),
                  pl.BlockSpec((1, 1, d), mod_map),
                  pl.BlockSpec((N_EXPERTS, d), const)],
        out_specs=(pl.BlockSpec((1, tm, d), tok), pl.BlockSpec((1, tm, d), tok),
                   pl.BlockSpec((1, N_EXPERTS, tm), lambda bi, i: (bi, 0, i))),
        out_shape=(jax.ShapeDtypeStruct((b, t, d), F32), jax.ShapeDtypeStruct((b, t, d), BF16),
                   jax.ShapeDtypeStruct((b, N_EXPERTS, t), F32)),
        compiler_params=_cparams(("parallel", "parallel")),
        name="output_projection_router",
    )(a, dof, dob, dn, gof, gob, hg, dnw, hgw, gavg, w_out, x, gate, shift, scale, router_t)


def _select_kernel(aff_ref, u_ref, bs_ref, bst_ref, su_ref, slot_ref, base_ref, nsub_ref, inc_scr, *, cap, nblk):
    x = aff_ref[0]
    bits = pltpu.bitcast(x, jnp.int32)
    lo = jnp.zeros((N_EXPERTS, 1), jnp.int32)
    for bit in range(30, -1, -1):
        cand = lo | (1 << bit)
        cnt = jnp.sum((bits >= cand).astype(jnp.int32), axis=1, keepdims=True)
        lo = jnp.where(cnt >= cap, cand, lo)
    gt = bits > lo
    eq = bits == lo
    umat = u_ref[...]

    def prefix(mask):
        mb = mask.astype(BF16)
        for j in range(nblk):
            inc_scr[:, j * LANES:(j + 1) * LANES] = jnp.dot(mb[:, j * LANES:(j + 1) * LANES], umat,
                                                            preferred_element_type=F32)
        totals = jnp.dot(mb, bs_ref[...], preferred_element_type=F32)
        offs = _mm_hi(totals, su_ref[...])
        return inc_scr[...] + _mm_hi(offs, bst_ref[...]), offs, totals

    eqf = eq.astype(F32)
    n_gt = jnp.sum(gt.astype(F32), axis=1, keepdims=True)
    eq_before, _, _ = prefix(eqf)
    sel = gt | (eq & ((eq_before - eqf) < (cap - n_gt)))
    self_ = sel.astype(F32)
    pos, offs, totals = prefix(self_)
    slot_ref[0] = jnp.where(sel, pos - 1.0, -1.0).astype(jnp.int32)
    start = jnp.floor(offs * (1.0 / ROUTE_ALIGN)) * ROUTE_ALIGN
    span = offs + totals - start
    n_sub = functools.reduce(jnp.add, [jnp.where(span > j * ROUTE_SUB, 1.0, 0.0)
                                       for j in range(ROUTE_WIN // ROUTE_SUB)])
    base_ref[0] = start.astype(jnp.int32)
    nsub_ref[0] = jnp.max(n_sub, axis=0, keepdims=True).astype(jnp.int32)


def _moe_select(aff_t, cap):
    b, e, t = aff_t.shape
    nblk = t // ROUTE_BLK
    u = jnp.asarray(np.triu(np.ones((LANES, LANES), np.float32)), BF16)
    blk = np.arange(t) // ROUTE_BLK
    bs = (blk[:, None] == np.arange(nblk)[None, :]).astype(np.float32)
    su = np.triu(np.ones((nblk, nblk), np.float32), 1)
    const = lambda bi: (0, 0)
    return pl.pallas_call(
        functools.partial(_select_kernel, cap=cap, nblk=nblk),
        grid=(b,),
        in_specs=[pl.BlockSpec((1, e, t), lambda bi: (bi, 0, 0)),
                  pl.BlockSpec((LANES, LANES), const),
                  pl.BlockSpec((t, nblk), const),
                  pl.BlockSpec((nblk, t), const),
                  pl.BlockSpec((nblk, nblk), const)],
        out_specs=(pl.BlockSpec((1, e, t), lambda bi: (bi, 0, 0)),
                   pl.BlockSpec((1, e, nblk), lambda bi: (bi, 0, 0)),
                   pl.BlockSpec((1, 1, nblk), lambda bi: (bi, 0, 0))),
        out_shape=(jax.ShapeDtypeStruct((b, e, t), jnp.int32), jax.ShapeDtypeStruct((b, e, nblk), jnp.int32),
                   jax.ShapeDtypeStruct((b, 1, nblk), jnp.int32)),
        scratch_shapes=[pltpu.VMEM((e, t), F32)],
        compiler_params=_cparams(("parallel",)),
        name="moe_select",
    )(aff_t, u, jnp.asarray(bs, BF16), jnp.asarray(bs.T), jnp.asarray(su))


def _gather_kernel(start_sm, nsub_sm, slot_ref, h_ref, xe_ref, *, nblk):
    bi = pl.program_id(0)
    xe_ref[0] = jnp.zeros(xe_ref.shape[1:], BF16)
    row = lax.broadcasted_iota(jnp.int32, (ROUTE_SUB, ROUTE_BLK), 0)

    def body(k, carry):
        t0 = pl.multiple_of(k * ROUTE_BLK, ROUTE_BLK)
        hblk = h_ref[0, pl.ds(t0, ROUTE_BLK), :]
        slots = slot_ref[0, :, pl.ds(t0, ROUTE_BLK)]
        firsts = [start_sm[(bi * N_EXPERTS + e) * nblk + k] for e in range(N_EXPERTS)]

        def sub(j, inner):
            starts = [pl.multiple_of(first + j * ROUTE_SUB, ROUTE_ALIGN) for first in firsts]
            onehot = jnp.concatenate(
                [jnp.where(row == slots[e:e + 1, :] - starts[e], 1.0, 0.0).astype(BF16)
                 for e in range(N_EXPERTS)], axis=0)
            rows = jnp.dot(onehot, hblk, preferred_element_type=F32).astype(BF16)
            for e in range(N_EXPERTS):
                win = pl.ds(starts[e], ROUTE_SUB)
                xe_ref[0, e, win, :] = xe_ref[0, e, win, :] + rows[e * ROUTE_SUB:(e + 1) * ROUTE_SUB]
            return inner

        lax.fori_loop(0, nsub_sm[bi * nblk + k], sub, 0)
        return carry

    lax.fori_loop(0, nblk, body, 0)


def _moe_gather(h2, slot, start_flat, nsub_flat, cap):
    b, t, d = h2.shape
    nblk = t // ROUTE_BLK
    cp = cap + ROUTE_WIN
    dq = _tile(d, 256)
    grid_spec = pltpu.PrefetchScalarGridSpec(
        num_scalar_prefetch=2,
        grid=(b, d // dq),
        in_specs=[pl.BlockSpec((1, N_EXPERTS, t), lambda bi, j, s0, s1: (bi, 0, 0)),
                  pl.BlockSpec((1, t, dq), lambda bi, j, s0, s1: (bi, 0, j))],
        out_specs=pl.BlockSpec((1, N_EXPERTS, cp, dq), lambda bi, j, s0, s1: (bi, 0, 0, j)),
    )
    return pl.pallas_call(
        functools.partial(_gather_kernel, nblk=nblk),
        grid_spec=grid_spec,
        out_shape=jax.ShapeDtypeStruct((b, N_EXPERTS, cp, d), BF16),
        compiler_params=_cparams(("parallel", "parallel")),
        name="moe_gather",
    )(start_flat, nsub_flat, slot, h2)


def _ffn_kernel(x_ref, wg_ref, wu_ref, wd_ref, y_ref, *, cap):
    x = x_ref[0, 0, 0:cap, :]
    a = jnp.dot(x, wg_ref[0], preferred_element_type=F32)
    u = jnp.dot(x, wu_ref[0], preferred_element_type=F32)
    y_ref[0, 0, 0:cap, :] = jnp.dot((_silu(a) * u).astype(BF16), wd_ref[0],
                                    preferred_element_type=F32).astype(BF16)
    y_ref[0, 0, cap:, :] = jnp.zeros((y_ref.shape[2] - cap, y_ref.shape[3]), BF16)


def _moe_ffn(xe, wg, wu, wd, cap):
    b, e, cp, d = xe.shape
    f = wg.shape[-1]
    return pl.pallas_call(
        functools.partial(_ffn_kernel, cap=cap),
        grid=(e, b),
        in_specs=[pl.BlockSpec((1, 1, cp, d), lambda ei, bi: (bi, ei, 0, 0)),
                  pl.BlockSpec((1, d, f), lambda ei, bi: (ei, 0, 0)),
                  pl.BlockSpec((1, d, f), lambda ei, bi: (ei, 0, 0)),
                  pl.BlockSpec((1, f, d), lambda ei, bi: (ei, 0, 0))],
        out_specs=pl.BlockSpec((1, 1, cp, d), lambda ei, bi: (bi, ei, 0, 0)),
        out_shape=jax.ShapeDtypeStruct((b, e, cp, d), BF16),
        compiler_params=_cparams(("parallel", "parallel")),
        name="moe_ffn",
    )(xe, wg, wu, wd)


def _combine_kernel(start_sm, nsub_sm, ye_ref, x_ref, gate_ref, slot_ref, aff_ref, o_ref, *, nblk, per_step):
    bi = pl.program_id(0)
    row = lax.broadcasted_iota(jnp.int32, (ROUTE_SUB, ROUTE_BLK), 0)
    for c in range(per_step):
        k = pl.program_id(2) * per_step + c
        tok = slice(c * ROUTE_BLK, (c + 1) * ROUTE_BLK)
        slots = slot_ref[0, :, tok]
        aff = aff_ref[0, :, tok]
        firsts = [start_sm[(bi * N_EXPERTS + e) * nblk + k] for e in range(N_EXPERTS)]

        def sub(j, acc, slots=slots, aff=aff, firsts=firsts):
            starts = [pl.multiple_of(first + j * ROUTE_SUB, ROUTE_ALIGN) for first in firsts]
            wsel = jnp.concatenate([jnp.where(row == slots[e:e + 1, :] - starts[e], aff[e:e + 1, :], 0.0)
                                    for e in range(N_EXPERTS)], axis=0)
            ys = jnp.concatenate([ye_ref[0, e, pl.ds(starts[e], ROUTE_SUB), :] for e in range(N_EXPERTS)],
                                 axis=0)
            hi, lo = _split2(wsel)
            tn = (((0,), (0,)), ((), ()))
            return (acc + lax.dot_general(hi, ys, tn, preferred_element_type=F32)
                    + lax.dot_general(lo, ys, tn, preferred_element_type=F32))

        acc = lax.fori_loop(0, nsub_sm[bi * nblk + k], sub, jnp.zeros((ROUTE_BLK, x_ref.shape[2]), F32))
        o_ref[0, tok, :] = x_ref[0, tok, :] + gate_ref[0] * acc


def _moe_combine(ye, x1, gate, slot, aff_t, start_flat, nsub_flat):
    b, t, d = x1.shape
    cp = ye.shape[2]
    nblk = t // ROUTE_BLK
    dh = d // 2
    per_b = gate.shape[0] > 1
    per_step = _tile(nblk, COMBINE_BLOCKS)
    tok = per_step * ROUTE_BLK
    grid_spec = pltpu.PrefetchScalarGridSpec(
        num_scalar_prefetch=2,
        grid=(b, 2, nblk // per_step),
        in_specs=[pl.BlockSpec((1, N_EXPERTS, cp, dh), lambda bi, j, k, s0, s1: (bi, 0, 0, j)),
                  pl.BlockSpec((1, tok, dh), lambda bi, j, k, s0, s1: (bi, k, j)),
                  pl.BlockSpec((1, 1, dh), (lambda bi, j, k, s0, s1: (bi, 0, j)) if per_b
                               else (lambda bi, j, k, s0, s1: (0, 0, j))),
                  pl.BlockSpec((1, N_EXPERTS, tok), lambda bi, j, k, s0, s1: (bi, 0, k)),
                  pl.BlockSpec((1, N_EXPERTS, tok), lambda bi, j, k, s0, s1: (bi, 0, k))],
        out_specs=pl.BlockSpec((1, tok, dh), lambda bi, j, k, s0, s1: (bi, k, j)),
    )
    return pl.pallas_call(
        functools.partial(_combine_kernel, nblk=nblk, per_step=per_step),
        grid_spec=grid_spec,
        out_shape=jax.ShapeDtypeStruct((b, t, d), F32),
        compiler_params=_cparams(("parallel", "parallel", "arbitrary")),
        name="moe_combine",
    )(start_flat, nsub_flat, ye, x1, gate, slot, aff_t)


def _expert_choice_ffn(x1, h2, aff_t, gate, wg, wu, wd):
    b, t, _ = x1.shape
    cap = CAPACITY_FACTOR * t // N_EXPERTS
    slot, start, nsub = _moe_select(aff_t, cap)
    start_flat, nsub_flat = start.reshape(-1), nsub.reshape(-1)
    xe = _moe_gather(h2, slot, start_flat, nsub_flat, cap)
    ye = _moe_ffn(xe, wg, wu, wd, cap)
    return _moe_combine(ye, x1, gate, slot, aff_t, start_flat, nsub_flat)


def _rope_tables(n_tokens):
    rows = n_tokens // GRID_W
    row = jnp.repeat(jnp.arange(rows, dtype=F32), GRID_W)
    col = jnp.tile(jnp.arange(GRID_W, dtype=F32), rows)
    n_freq = HEAD_DIM // 4
    inv_freq = ROPE_THETA ** (-jnp.arange(n_freq, dtype=F32) / n_freq)
    ang_r = row[:, None] * inv_freq
    ang_c = col[:, None] * inv_freq
    cos = jnp.concatenate([jnp.cos(ang_r)] * 2 + [jnp.cos(ang_c)] * 2, axis=-1)
    sin = jnp.concatenate([jnp.sin(ang_r)] * 2 + [jnp.sin(ang_c)] * 2, axis=-1)
    return jnp.tile(cos, (1, LANES // HEAD_DIM)), jnp.tile(sin, (1, LANES // HEAD_DIM))


def _block_diag(width, block, value):
    idx = np.arange(width) // block
    return jnp.asarray((idx[:, None] == idx[None, :]).astype(np.float32) * value)


def kernel(x, c, ctx, c_ctx, w_mod, b_mod, w_in, w_out, attn_q_norm, attn_k_norm, dn_conv, dn_a_log,
           dn_dt_bias, dn_norm, hg_lower_bounds, hg_norm, moe_router, moe_w_gate, moe_w_up, moe_w_down):
    depth = w_mod.shape[0]
    b, t_lat, d = x.shape
    cos, sin = _rope_tables(t_lat)
    g_head = _block_diag(LANES, HEAD_DIM, 1.0 / HEAD_DIM)
    g_mean = _block_diag(REC_W, HEAD_DIM, 1.0 / HEAD_DIM)
    g_sum = _block_diag(REC_W, HEAD_DIM, 1.0)
    consts = _scan_constants()
    s_zero = jnp.zeros((b, 2, REC_W, REC_W), F32)

    lb_w = jax.nn.softmax(hg_lower_bounds.astype(F32), axis=0)
    hg_lb = jnp.cumsum(lb_w, axis=0) - lb_w[0]

    rows = ((b + 1 + 7) // 8) * 8
    cond = jnp.zeros((rows, d), F32).at[:b].set(c).at[b].set(c_ctx)
    mod = _modulation(cond, w_mod, b_mod)

    n_small = 4 * DN_HEADS
    w_in_r = jnp.concatenate(
        [w_in[:, :, :C_HG], w_in[:, :, C_HG + n_small:], w_in[:, :, C_HG:C_HG + n_small],
         jnp.zeros((depth, d, C_END - C_BA - n_small), w_in.dtype)], axis=-1).astype(BF16)
    w_out_b = _cast_bf16(w_out)
    wg_b, wu_b, wd_b = _cast_bf16(moe_w_gate), _cast_bf16(moe_w_up), _cast_bf16(moe_w_down)

    x_lat, x_ctx = x, ctx
    for l in range(depth):
        ctx_out = l < depth - 1
        m_lat = [mod[l, :b, j * d:(j + 1) * d][:, None, :] for j in range(6)]
        m_ctx = [mod[l, b:b + 1, j * d:(j + 1) * d][:, None, :] for j in range(6)]
        qn = jnp.tile(attn_q_norm[l], LANES // HEAD_DIM)[None, :]
        kn = jnp.tile(attn_k_norm[l], LANES // HEAD_DIM)[None, :]
        conv_w = jnp.zeros((8, 3 * REC_W), F32).at[:CONV_K].set(dn_conv[l])
        pad = jnp.zeros((LANES - 4 * DN_HEADS,), F32)
        neg_a = jnp.concatenate([jnp.zeros((2 * DN_HEADS,), F32), -jnp.exp(dn_a_log[l].reshape(-1)), pad])[None, :]
        dt_b = jnp.concatenate([jnp.zeros((2 * DN_HEADS,), F32), dn_dt_bias[l].reshape(-1), pad])[None, :]
        dnw = jnp.tile(dn_norm[l], DN_HEADS)[None, :]
        hgw = jnp.tile(hg_norm[l], HG_HEADS)[None, :]
        router_t = moe_router[l].T

        streams = {}
        for name, xs, ms, rotate in (("ctx", x_ctx, m_ctx, False), ("lat", x_lat, m_lat, True)):
            t = xs.shape[1]
            q, kt, v, dn, hg, ba = _input_projection(xs, ms[0], ms[1], w_in_r[l], qn, kn,
                                                     cos[:t], sin[:t], g_head, rotate)
            qkv, gb = _dn_prep(dn, ba, conv_w, neg_a, dt_b, g_sum)
            streams[name] = dict(q=q, kv=(kt, v), dn=dn, hg=hg, qkv=qkv, gb=gb)

        sc, sl = streams["ctx"], streams["lat"]
        dcf, dcb, dn_state = _dn_scan(sc["qkv"], sc["gb"], s_zero, consts)
        dlf, dlb, _ = _dn_scan(sl["qkv"], sl["gb"], dn_state, consts)
        gcf, gcb, hg_state = _hg_scan(sc["hg"], hg_lb[l], s_zero, consts)
        glf, glb, _ = _hg_scan(sl["hg"], hg_lb[l], hg_state, consts)

        a_lat = _attention(sl["q"], [sc["kv"], sl["kv"]])
        x1, h2, aff_t = _output_projection(
            a_lat, dlf, dlb, sl["dn"], glf, glb, sl["hg"], dnw, hgw, g_mean, w_out_b[l], x_lat,
            m_lat[2], m_lat[3], m_lat[4], router_t)
        x_lat = _expert_choice_ffn(x1, h2, aff_t, m_lat[5], wg_b[l], wu_b[l], wd_b[l])
        if ctx_out:
            a_ctx = _attention(sc["q"], [sc["kv"]])
            x1, h2, aff_t = _output_projection(
                a_ctx, dcf, dcb, sc["dn"], gcf, gcb, sc["hg"], dnw, hgw, g_mean, w_out_b[l], x_ctx,
                m_ctx[2], m_ctx[3], m_ctx[4], router_t)
            x_ctx = _expert_choice_ffn(x1, h2, aff_t, m_ctx[5], wg_b[l], wu_b[l], wd_b[l])
    return x_lat
```

```python
import functools

import numpy as np
import jax
import jax.numpy as jnp
from jax import lax
from jax.experimental import pallas as pl
from jax.experimental.pallas import tpu as pltpu

F32 = jnp.float32
BF16 = jnp.bfloat16
HIGHEST = lax.Precision.HIGHEST

HEAD_DIM = 64
N_Q_HEADS = 8
N_KV_HEADS = 2
GQA_GROUP = N_Q_HEADS // N_KV_HEADS
DN_HEADS = 4
HG_HEADS = 4
GRID_W = 64
ROPE_THETA = 10000.0
CONV_K = 5
CHUNK = 64
N_EXPERTS = 16
CAPACITY_FACTOR = 2
EPS = 1e-6
LOG2E = 1.4426950408889634
V_ROWS = HEAD_DIM + 16

ATTN_Q_W = N_Q_HEADS * HEAD_DIM
ATTN_KV_W = N_KV_HEADS * HEAD_DIM
REC_W = DN_HEADS * HEAD_DIM
LANES = 128
ROUTE_BLK = LANES
ROUTE_ALIGN = 16
ROUTE_WIN = ROUTE_BLK + ROUTE_ALIGN
COMBINE_BLOCKS = 4
ROUTE_SUB = ROUTE_WIN // 3
N_LEVELS = 6
SCAN_BATCH = 4
VMEM_LIMIT = 56 * 1024 * 1024

C_Q, C_K, C_V, C_DN, C_HG, C_BA, C_END = 0, 512, 640, 768, 1792, 3072, 3200


def _mm(a, b):
    return jnp.dot(a.astype(BF16), b.astype(BF16), preferred_element_type=F32)


def _mm_nt(a, b):
    return lax.dot_general(a.astype(BF16), b.astype(BF16), (((1,), (1,)), ((), ())),
                           preferred_element_type=F32)


def _mm_tn(a, b):
    return lax.dot_general(a.astype(BF16), b.astype(BF16), (((0,), (0,)), ((), ())),
                           preferred_element_type=F32)


def _mm_hi(a, b):
    return jnp.dot(a, b, precision=HIGHEST, preferred_element_type=F32)


def _sigmoid(x):
    return 1.0 / (1.0 + jnp.exp(-x))


def _silu(x):
    return x * _sigmoid(x)


def _cparams(sem):
    return pltpu.CompilerParams(dimension_semantics=sem, vmem_limit_bytes=VMEM_LIMIT)


def _tile(n, pref):
    return pref if n % pref == 0 else n


def _cast_kernel(x_ref, o_ref):
    o_ref[...] = x_ref[...].astype(BF16)


def _cast_bf16(w):
    r, c = w.shape[-2:]
    w3 = w.reshape(-1, r, c)
    rb = _tile(r, 512)
    out = pl.pallas_call(
        _cast_kernel,
        grid=(w3.shape[0], r // rb),
        in_specs=[pl.BlockSpec((1, rb, c), lambda i, j: (i, j, 0))],
        out_specs=pl.BlockSpec((1, rb, c), lambda i, j: (i, j, 0)),
        out_shape=jax.ShapeDtypeStruct(w3.shape, BF16),
        compiler_params=_cparams(("parallel", "parallel")),
        name="cast_bf16",
    )(w3)
    return out.reshape(w.shape)


def _mod_kernel(c_ref, w_ref, b_ref, o_ref):
    o_ref[0] = _mm_hi(_silu(c_ref[...]), w_ref[0]) + b_ref[0]


def _modulation(cond, w_mod, b_mod):
    depth, d, n = w_mod.shape
    rows = cond.shape[0]
    tn = _tile(n, 1536)
    return pl.pallas_call(
        _mod_kernel,
        grid=(depth, n // tn),
        in_specs=[pl.BlockSpec((rows, d), lambda l, j: (0, 0)),
                  pl.BlockSpec((1, d, tn), lambda l, j: (l, 0, j)),
                  pl.BlockSpec((1, 1, tn), lambda l, j: (l, 0, j))],
        out_specs=pl.BlockSpec((1, rows, tn), lambda l, j: (l, 0, j)),
        out_shape=jax.ShapeDtypeStruct((depth, rows, n), F32),
        compiler_params=_cparams(("parallel", "parallel")),
        name="modulation",
    )(cond, w_mod, b_mod.reshape(depth, 1, n))


def _inproj_kernel(x_ref, sh_ref, sc_ref, w_ref, qn_ref, kn_ref, cos_ref, sin_ref, g_ref,
                   q_ref, k_ref, v_ref, dn_ref, hg_ref, ba_ref, *, rotate):
    x = x_ref[0]
    ms = jnp.mean(x * x, axis=-1, keepdims=True)
    h = x * lax.rsqrt(ms + EPS) * (1.0 + sc_ref[0]) + sh_ref[0]
    hb = h.astype(BF16)
    gmat = g_ref[...]

    def proj(lo, hi):
        return jnp.dot(hb, w_ref[:, lo:hi], preferred_element_type=F32)

    def head_norm_rope(t, nw):
        t = t * lax.rsqrt(_mm_hi(t * t, gmat) + EPS) * nw
        if rotate:
            lane = lax.broadcasted_iota(jnp.int32, t.shape, 1)
            first = (lane % 32) < 16
            rot = jnp.where(first, -pltpu.roll(t, LANES - 16, 1), pltpu.roll(t, 16, 1))
            t = t * cos_ref[...] + rot * sin_ref[...]
        return t

    def q_group(j):
        t = proj(C_Q + j * LANES, C_Q + (j + 1) * LANES)
        yield
        qj = head_norm_rope(t, qn_ref[...])
        qt = (qj * (HEAD_DIM ** -0.5 * LOG2E)).T.astype(BF16)
        q_ref[0, 2 * j] = qt[:HEAD_DIM]
        q_ref[0, 2 * j + 1] = qt[HEAD_DIM:]

    def k_group():
        t = proj(C_K, C_V)
        yield
        k = head_norm_rope(t, kn_ref[...]).astype(BF16)
        k_ref[0, 0] = k[:, :HEAD_DIM]
        k_ref[0, 1] = k[:, HEAD_DIM:]

    def v_group():
        v = proj(C_V, C_DN)
        yield
        vt = v.T
        ones = jnp.ones((V_ROWS - HEAD_DIM, vt.shape[1]), F32)
        v_ref[0, 0] = jnp.concatenate([vt[:HEAD_DIM], ones], axis=0).astype(BF16)
        v_ref[0, 1] = jnp.concatenate([vt[HEAD_DIM:], ones], axis=0).astype(BF16)

    def plain_group(o_ref, base, lo, hi):
        t = proj(base + lo, base + hi)
        yield
        o_ref[0, :, lo:hi] = t

    step = 4 * LANES
    plain = [plain_group(o_ref, base, lo, min(lo + step, width))
             for o_ref, base, width in ((dn_ref, C_DN, C_HG - C_DN), (hg_ref, C_HG, C_BA - C_HG),
                                        (ba_ref, C_BA, C_END - C_BA))
             for lo in range(0, width, step)]
    _lockstep([q_group(j) for j in range(ATTN_Q_W // LANES)] + [k_group(), v_group()] + plain, stagger=True)


def _input_projection(x, shift, scale, w, qn, kn, cos, sin, gmat, rotate):
    b, t, d = x.shape
    tm = _tile(t, 512)
    per_b = shift.shape[0] > 1
    mod_map = (lambda bi, i: (bi, 0, 0)) if per_b else (lambda bi, i: (0, 0, 0))
    const = lambda bi, i: (0, 0)
    out_shape = (
        jax.ShapeDtypeStruct((b, N_Q_HEADS, HEAD_DIM, t), BF16),
        jax.ShapeDtypeStruct((b, N_KV_HEADS, t, HEAD_DIM), BF16),
        jax.ShapeDtypeStruct((b, N_KV_HEADS, V_ROWS, t), BF16),
        jax.ShapeDtypeStruct((b, t, C_HG - C_DN), F32),
        jax.ShapeDtypeStruct((b, t, C_BA - C_HG), F32),
        jax.ShapeDtypeStruct((b, t, C_END - C_BA), F32),
    )
    return pl.pallas_call(
        functools.partial(_inproj_kernel, rotate=rotate),
        grid=(b, t // tm),
        in_specs=[pl.BlockSpec((1, tm, d), lambda bi, i: (bi, i, 0)),
                  pl.BlockSpec((1, 1, d), mod_map),
                  pl.BlockSpec((1, 1, d), mod_map),
                  pl.BlockSpec((d, C_END), const),
                  pl.BlockSpec((1, LANES), const),
                  pl.BlockSpec((1, LANES), const),
                  pl.BlockSpec((tm, LANES), lambda bi, i: (i, 0)),
                  pl.BlockSpec((tm, LANES), lambda bi, i: (i, 0)),
                  pl.BlockSpec((LANES, LANES), const)],
        out_specs=(pl.BlockSpec((1, N_Q_HEADS, HEAD_DIM, tm), lambda bi, i: (bi, 0, 0, i)),
                   pl.BlockSpec((1, N_KV_HEADS, tm, HEAD_DIM), lambda bi, i: (bi, 0, i, 0)),
                   pl.BlockSpec((1, N_KV_HEADS, V_ROWS, tm), lambda bi, i: (bi, 0, 0, i)),
                   pl.BlockSpec((1, tm, C_HG - C_DN), lambda bi, i: (bi, i, 0)),
                   pl.BlockSpec((1, tm, C_BA - C_HG), lambda bi, i: (bi, i, 0)),
                   pl.BlockSpec((1, tm, C_END - C_BA), lambda bi, i: (bi, i, 0))),
        out_shape=out_shape,
        compiler_params=_cparams(("parallel", "parallel")),
        name="input_projection",
    )(x, shift, scale, w, qn, kn, cos, sin, gmat)


def _lockstep(gens, stagger=False):
    results = [None] * len(gens)
    live, started = [], 0
    while live or started < len(gens):
        fresh = 1 if stagger else len(gens)
        live += list(range(started, min(started + fresh, len(gens))))
        started = min(started + fresh, len(gens))
        for i in reversed(list(live)):
            try:
                next(gens[i])
            except StopIteration as stop:
                results[i] = stop.value
                live.remove(i)
    return results


def _attn_head(qt, ks, vts):
    ss = [jnp.dot(k[0, 0], qt, preferred_element_type=F32).astype(BF16) for k in ks]
    yield
    m = functools.reduce(jnp.maximum, [jnp.max(s, axis=0, keepdims=True) for s in ss])
    ps = [jnp.exp2(s - m) for s in ss]
    yield
    o = functools.reduce(jnp.add, [jnp.dot(vt[0, 0], p, preferred_element_type=F32) for p, vt in zip(ps, vts)])
    return o[:HEAD_DIM] / o[HEAD_DIM:HEAD_DIM + 1]


def _attn_kernel(*refs, n_src):
    q_ref, o_ref = refs[0], refs[-1]
    ks = [refs[1 + 2 * i] for i in range(n_src)]
    vts = [refs[2 + 2 * i] for i in range(n_src)]
    outs = _lockstep([_attn_head(q_ref[0, h], ks, vts) for h in range(GQA_GROUP)], stagger=True)
    o_ref[0] = jnp.concatenate(outs, axis=0).T.astype(BF16)


def _attention(qt, sources, t):
    b = qt.shape[0]
    tq = _tile(t, 256)
    gw = GQA_GROUP * HEAD_DIM
    in_specs = [pl.BlockSpec((1, GQA_GROUP, HEAD_DIM, tq), lambda bi, g, i: (bi, g, 0, i))]
    args = [qt]
    for k, vt in sources:
        tk = k.shape[2]
        in_specs.append(pl.BlockSpec((1, 1, tk, HEAD_DIM), lambda bi, g, i: (bi, g, 0, 0)))
        in_specs.append(pl.BlockSpec((1, 1, V_ROWS, tk), lambda bi, g, i: (bi, g, 0, 0)))
        args += [k, vt]
    return pl.pallas_call(
        functools.partial(_attn_kernel, n_src=len(sources)),
        grid=(b, N_KV_HEADS, t // tq),
        in_specs=in_specs,
        out_specs=pl.BlockSpec((1, tq, gw), lambda bi, g, i: (bi, i, g)),
        out_shape=jax.ShapeDtypeStruct((b, t, ATTN_Q_W), BF16),
        compiler_params=_cparams(("parallel", "parallel", "parallel")),
        name="attention",
    )(*args)


def _dn_prep_kernel(x_ref, xp_ref, xn_ref, ba_ref, cw_ref, na_ref, dtb_ref, g_ref, o_ref, gb_ref, buf):
    i = pl.program_id(1)
    n = pl.num_programs(1)
    tm = x_ref.shape[1]
    buf[0:8] = jnp.where(i > 0, xp_ref[0], 0.0)
    buf[8:8 + tm] = x_ref[0]
    buf[8 + tm:16 + tm] = jnp.where(i < n - 1, xn_ref[0], 0.0)
    half = CONV_K // 2
    y = cw_ref[0:1, :] * buf[8 - half:8 - half + tm]
    for j in range(1, CONV_K):
        y = y + cw_ref[j:j + 1, :] * buf[8 - half + j:8 - half + j + tm]
    y = _silu(y)
    gmat = g_ref[...]
    q = y[:, :REC_W]
    k = y[:, REC_W:2 * REC_W]
    o_ref[0, :, 0:REC_W] = q * lax.rsqrt(_mm_hi(q * q, gmat) + EPS) * HEAD_DIM ** -0.5
    o_ref[0, :, REC_W:2 * REC_W] = k * lax.rsqrt(_mm_hi(k * k, gmat) + EPS)
    o_ref[0, :, 2 * REC_W:] = y[:, 2 * REC_W:]
    ba = ba_ref[0]
    z = ba + dtb_ref[...]
    softplus = jnp.maximum(z, 0.0) + jnp.log1p(jnp.exp(-jnp.abs(z)))
    lane = lax.broadcasted_iota(jnp.int32, ba.shape, 1)
    gb_ref[0] = jnp.where(lane < 2 * DN_HEADS, _sigmoid(ba), na_ref[...] * softplus)


def _dn_prep(dn, ba, conv_w, neg_a, dt_bias, gsum):
    b, t, _ = dn.shape
    tm = _tile(t, 512)
    w3 = 3 * REC_W
    nb8 = t // 8
    return pl.pallas_call(
        _dn_prep_kernel,
        grid=(b, t // tm),
        in_specs=[pl.BlockSpec((1, tm, w3), lambda bi, i: (bi, i, 0)),
                  pl.BlockSpec((1, 8, w3), lambda bi, i: (bi, jnp.maximum(i * (tm // 8) - 1, 0), 0)),
                  pl.BlockSpec((1, 8, w3), lambda bi, i: (bi, jnp.minimum((i + 1) * (tm // 8), nb8 - 1), 0)),
                  pl.BlockSpec((1, tm, LANES), lambda bi, i: (bi, i, 0)),
                  pl.BlockSpec((8, w3), lambda bi, i: (0, 0)),
                  pl.BlockSpec((1, LANES), lambda bi, i: (0, 0)),
                  pl.BlockSpec((1, LANES), lambda bi, i: (0, 0)),
                  pl.BlockSpec((REC_W, REC_W), lambda bi, i: (0, 0))],
        out_specs=(pl.BlockSpec((1, tm, w3), lambda bi, i: (bi, i, 0)),
                   pl.BlockSpec((1, tm, LANES), lambda bi, i: (bi, i, 0))),
        out_shape=(jax.ShapeDtypeStruct((b, t, w3), F32), jax.ShapeDtypeStruct((b, t, LANES), F32)),
        scratch_shapes=[pltpu.VMEM((tm + 16, w3), F32)],
        compiler_params=_cparams(("parallel", "parallel")),
        name="deltanet_prep",
    )(dn, dn, dn, ba, conv_w, neg_a, dt_bias, gsum)


def _split2(a):
    hi = a.astype(BF16)
    return hi, (a - hi.astype(F32)).astype(BF16)


def _split3(a):
    hi = a.astype(BF16)
    r = a - hi.astype(F32)
    mid = r.astype(BF16)
    return hi, mid, (r - mid.astype(F32)).astype(BF16)


def _bdiag(a, hm):
    return jnp.concatenate([a] * DN_HEADS, axis=0) * hm


def _mm_bd(a, b, hm):
    return jnp.dot(a.astype(BF16), _bdiag(b.astype(BF16), hm), preferred_element_type=F32)


def _mm_sel_l(sel, b, terms=3):
    n = b.shape[1]
    parts = _split3(b) if terms == 3 else _split2(b)
    t = jnp.dot(sel.astype(BF16), jnp.concatenate(parts, axis=1), preferred_element_type=F32)
    return functools.reduce(jnp.add, [t[:, i * n:(i + 1) * n] for i in range(terms)])


def _mm_sel_r(a, sel):
    m = a.shape[0]
    t = jnp.dot(jnp.concatenate(_split3(a), axis=0), sel.astype(BF16), preferred_element_type=F32)
    return t[:m] + t[m:2 * m] + t[2 * m:]


def _dn_chunk(x, gb, st, ex, lmat, causal, lvl_ref, d, hm):
    n = REC_W
    lvl = lambda j: lvl_ref[d, j * CHUNK:(j + 1) * CHUNK, :]
    eye = lvl(N_LEVELS)
    ones8 = jnp.ones((8, CHUNK), F32)
    q, k, v = x[:, :n], x[:, n:2 * n], x[:, 2 * n:]
    gbx = _mm_sel_r(gb, ex)
    yield
    beta, g = gbx[:, :n], gbx[:, n:]
    cum = _mm_sel_l(lmat, g)
    last = 0 if d else CHUNK - 1
    clast = cum[last:last + 1]
    kb = k * beta
    ks = _bdiag(k.astype(BF16), hm)
    kk = lax.dot_general(kb.astype(BF16), ks, (((1,), (1,)), ((), ())), preferred_element_type=F32)
    qk = lax.dot_general(q.astype(BF16), ks, (((1,), (1,)), ((), ())), preferred_element_type=F32)
    yield
    cum_s = _mm_sel_l(ones8, cum * eye)[0:1]
    ecum = jnp.exp(cum)
    kdec = k * jnp.exp(clast - cum)
    yield
    decay = jnp.where(causal, jnp.exp(jnp.where(causal, cum - cum_s, 0.0)), 0.0)
    m = kk * decay
    attn = qk * decay
    inv = eye - lvl(N_LEVELS - 1) * m
    for lev in range(N_LEVELS - 2, -1, -1):
        half = _mm_bd(inv, lvl(lev) * m, hm)
        yield
        inv = inv - _mm_bd(half, inv, hm)
        yield
    rhs = v * beta - _mm_nt(kb * ecum, st)
    qs = _mm_nt(q * ecum, st)
    yield
    v_new = _mm_bd(inv, rhs, hm)
    yield
    out = qs + _mm_bd(attn, v_new, hm)
    st_new = st * jnp.exp(clast) + _mm_tn(v_new, kdec) * hm.astype(F32)
    return out, st_new


def _hg_chunk(qr, fr, v, lb, st, amat, lvl_ref, d, hm):
    lvl = lambda j: lvl_ref[d, j * CHUNK:(j + 1) * CHUNK, :]
    last = 0 if d else CHUNK - 1
    q = _silu(qr)
    f = lb + (1.0 - lb) * _sigmoid(fr)
    k = 1.0 - f
    g = jnp.log(f)
    ex = jnp.exp(_mm_sel_l(amat, g, terms=2))
    yield

    def level(xl, mask):
        kl = _bdiag((k * xl).astype(BF16), hm)
        return lax.dot_general((q * xl).astype(BF16), kl, (((1,), (1,)), ((), ())),
                               preferred_element_type=F32) * mask

    attn = level(1.0, lvl(N_LEVELS))
    for lev in range(N_LEVELS):
        attn = attn + level(ex[(2 + lev) * CHUNK:(3 + lev) * CHUNK], lvl(lev))
    qs = _mm_nt(q * ex[0:CHUNK], st)
    vk = _mm_tn(v, k * ex[CHUNK:2 * CHUNK])
    yield
    out = _mm_bd(attn, v, hm) + qs
    st_new = st * ex[last:last + 1] + vk * hm.astype(F32)
    return out, st_new


def _dn_scan_kernel(xf_ref, xb_ref, gf_ref, gb_ref, s0_ref, ex_ref, l_ref, cz_ref, lvl_ref, hm_ref,
                    of_ref, ob_ref, sf_ref, s_scr, *, nchunk):
    @pl.when(pl.program_id(1) == 0)
    def _():
        s_scr[...] = s0_ref[...]

    hm = hm_ref[...]
    refs = ((xf_ref, gf_ref, of_ref), (xb_ref, gb_ref, ob_ref))

    def body(c, carry):
        rows = [pl.ds(pl.multiple_of(cc * CHUNK, CHUNK), CHUNK) for cc in (c, nchunk - 1 - c)]
        insts = [(bi, d) for bi in range(s_scr.shape[0]) for d in range(2)]
        res = _lockstep([_dn_chunk(refs[d][0][bi, rows[d], :], refs[d][1][bi, rows[d], :], s_scr[bi, d],
                                   ex_ref[d], l_ref[d], cz_ref[d] > 0, lvl_ref, d, hm) for bi, d in insts])
        for (bi, d), (out, s_new) in zip(insts, res):
            refs[d][2][bi, rows[d], :] = out
            s_scr[bi, d] = s_new
        return carry

    lax.fori_loop(0, nchunk, body, 0)
    sf_ref[...] = s_scr[...]


def _hg_scan_kernel(qf_ref, qb_ref, ff_ref, fb_ref, if_ref, ib_ref, lb_ref, s0_ref, a_ref, lvl_ref, hm_ref,
                    of_ref, ob_ref, sf_ref, s_scr, *, nchunk):
    @pl.when(pl.program_id(1) == 0)
    def _():
        s_scr[...] = s0_ref[...]

    hm = hm_ref[...]
    refs = ((qf_ref, ff_ref, if_ref, of_ref), (qb_ref, fb_ref, ib_ref, ob_ref))

    def body(c, carry):
        rows = [pl.ds(pl.multiple_of(cc * CHUNK, CHUNK), CHUNK) for cc in (c, nchunk - 1 - c)]
        insts = [(bi, d) for bi in range(s_scr.shape[0]) for d in range(2)]
        res = _lockstep([_hg_chunk(refs[d][0][bi, rows[d], :], refs[d][1][bi, rows[d], :],
                                   refs[d][2][bi, rows[d], :], lb_ref[d:d + 1, :], s_scr[bi, d], a_ref[d],
                                   lvl_ref, d, hm) for bi, d in insts])
        for (bi, d), (out, s_new) in zip(insts, res):
            refs[d][3][bi, rows[d], :] = out
            s_scr[bi, d] = s_new
        return carry

    lax.fori_loop(0, nchunk, body, 0)
    sf_ref[...] = s_scr[...]


def _dn_scan(qkv, gb, s0, consts):
    b, t, w3 = qkv.shape
    tb = _tile(t, 512)
    nblk = t // tb
    n = REC_W
    nb = _tile(b, SCAN_BATCH)
    fwd = lambda bi, i: (bi, i, 0)
    bwd = lambda bi, i: (bi, nblk - 1 - i, 0)
    c3 = lambda bi, i: (0, 0, 0)
    state = pl.BlockSpec((nb, 2, n, n), lambda bi, i: (bi, 0, 0, 0))
    return pl.pallas_call(
        functools.partial(_dn_scan_kernel, nchunk=tb // CHUNK),
        grid=(b // nb, nblk),
        in_specs=[pl.BlockSpec((nb, tb, w3), fwd), pl.BlockSpec((nb, tb, w3), bwd),
                  pl.BlockSpec((nb, tb, LANES), fwd), pl.BlockSpec((nb, tb, LANES), bwd),
                  state,
                  pl.BlockSpec((2, LANES, 2 * n), c3),
                  pl.BlockSpec((2, CHUNK, CHUNK), c3),
                  pl.BlockSpec((2, CHUNK, n), c3),
                  pl.BlockSpec((2, (N_LEVELS + 1) * CHUNK, n), c3),
                  pl.BlockSpec((n, n), lambda bi, i: (0, 0))],
        out_specs=(pl.BlockSpec((nb, tb, n), fwd), pl.BlockSpec((nb, tb, n), bwd), state),
        out_shape=(jax.ShapeDtypeStruct((b, t, n), F32), jax.ShapeDtypeStruct((b, t, n), F32),
                   jax.ShapeDtypeStruct((b, 2, n, n), F32)),
        scratch_shapes=[pltpu.VMEM((nb, 2, n, n), F32)],
        compiler_params=_cparams(("parallel", "arbitrary")),
        name="deltanet_scan",
    )(qkv, qkv, gb, gb, s0, consts["ex"], consts["l"], consts["cz"], consts["lvl"], consts["hm"])


def _hg_scan(hg, lb, s0, consts):
    b, t, _ = hg.shape
    tb = _tile(t, 512)
    nblk = t // tb
    n = REC_W
    nb = _tile(b, SCAN_BATCH)
    fwd = lambda col: (lambda bi, i: (bi, i, col))
    bwd = lambda col: (lambda bi, i: (bi, nblk - 1 - i, col))
    c3 = lambda bi, i: (0, 0, 0)
    blk = lambda m: pl.BlockSpec((nb, tb, n), m)
    state = pl.BlockSpec((nb, 2, n, n), lambda bi, i: (bi, 0, 0, 0))
    return pl.pallas_call(
        functools.partial(_hg_scan_kernel, nchunk=tb // CHUNK),
        grid=(b // nb, nblk),
        in_specs=[blk(fwd(0)), blk(bwd(0)), blk(fwd(1)), blk(bwd(2)), blk(fwd(3)), blk(bwd(3)),
                  pl.BlockSpec((2, n), lambda bi, i: (0, 0)),
                  state,
                  pl.BlockSpec((2, (2 + N_LEVELS) * CHUNK, CHUNK), c3),
                  pl.BlockSpec((2, (N_LEVELS + 1) * CHUNK, n), c3),
                  pl.BlockSpec((n, n), lambda bi, i: (0, 0))],
        out_specs=(blk(fwd(0)), blk(bwd(0)), state),
        out_shape=(jax.ShapeDtypeStruct((b, t, n), F32), jax.ShapeDtypeStruct((b, t, n), F32),
                   jax.ShapeDtypeStruct((b, 2, n, n), F32)),
        scratch_shapes=[pltpu.VMEM((nb, 2, n, n), F32)],
        compiler_params=_cparams(("parallel", "arbitrary")),
        name="hgrn2_scan",
    )(hg, hg, hg, hg, hg, hg, lb, s0, consts["a"], consts["lvl"], consts["hm"])


def _scan_constants_dir(rev):
    t = np.arange(CHUNK)
    p = (CHUNK - 1 - t) if rev else t
    pt, pu = p[:, None], p[None, :]
    causal = (pu <= pt).astype(np.float32)
    head = np.arange(REC_W) // CHUNK
    hm = (head[:, None] == head[None, :]).astype(np.float32)
    rows = [causal, (pu > pt).astype(np.float32)]
    masks = []
    for lev in range(N_LEVELS):
        blk = CHUNK >> (lev + 1)
        bound = (pt // (2 * blk)) * (2 * blk) + blk - 1
        right = (pt % (2 * blk)) >= blk
        rows.append(np.where(right, (pu > bound) & (pu <= pt), (pu > pt) & (pu <= bound)).astype(np.float32))
        masks.append(((pt // (2 * blk) == pu // (2 * blk)) & right & ((pu % (2 * blk)) < blk)).astype(np.float32))
    masks.append(np.eye(CHUNK, dtype=np.float32))
    d = 1 if rev else 0
    ex = np.zeros((LANES, 2 * REC_W), np.float32)
    for h in range(DN_HEADS):
        ex[d * DN_HEADS + h, h * CHUNK:(h + 1) * CHUNK] = 1.0
        ex[2 * DN_HEADS + d * DN_HEADS + h, REC_W + h * CHUNK:REC_W + (h + 1) * CHUNK] = 1.0
    return {
        "l": causal, "cz": np.tile(causal, (1, DN_HEADS)), "hm": hm, "ex": ex,
        "a": np.concatenate(rows, axis=0),
        "lvl": np.concatenate([np.tile(mk, (1, DN_HEADS)) for mk in masks], axis=0),
    }


def _scan_constants():
    fwd, bwd = _scan_constants_dir(False), _scan_constants_dir(True)
    out = {key: jnp.asarray(np.stack([fwd[key], bwd[key]])) for key in ("l", "cz", "ex", "a", "lvl")}
    out["hm"] = jnp.asarray(fwd["hm"], BF16)
    return out


def _outproj_kernel(a_ref, dof_ref, dob_ref, z_ref, gof_ref, gob_ref, gg_ref, dnw_ref, hgw_ref, g_ref,
                    w_ref, x_ref, gate_ref, sh_ref, sc_ref, rt_ref, x1_ref, h2_ref, aff_ref):
    gmat = g_ref[...]

    def gated(o, z, nw):
        return o * lax.rsqrt(_mm_hi(o * o, gmat) + EPS) * nw * _silu(z)

    dmix = gated(dof_ref[0] + dob_ref[0], z_ref[0], dnw_ref[...])
    gmix = gated(gof_ref[0] + gob_ref[0], gg_ref[0], hgw_ref[...])
    y = (jnp.dot(a_ref[0], w_ref[0:ATTN_Q_W, :], preferred_element_type=F32)
         + jnp.dot(dmix.astype(BF16), w_ref[ATTN_Q_W:ATTN_Q_W + REC_W, :], preferred_element_type=F32)
         + jnp.dot(gmix.astype(BF16), w_ref[ATTN_Q_W + REC_W:, :], preferred_element_type=F32))
    x1 = x_ref[0] + gate_ref[0] * y
    x1_ref[0] = x1
    ms = jnp.mean(x1 * x1, axis=-1, keepdims=True)
    h2 = x1 * lax.rsqrt(ms + EPS) * (1.0 + sc_ref[0]) + sh_ref[0]
    h2_ref[0] = h2.astype(BF16)
    logits = lax.dot_general(rt_ref[...], h2, (((1,), (1,)), ((), ())), precision=HIGHEST,
                             preferred_element_type=F32)
    e = jnp.exp(logits - jnp.max(logits, axis=0, keepdims=True))
    aff_ref[0] = e / jnp.sum(e, axis=0, keepdims=True)


def _output_projection(a, dof, dob, dn, gof, gob, hg, dnw, hgw, gavg, w_out, x, gate, shift, scale, router_t):
    b, t, d = x.shape
    tm = _tile(t, 512)
    n = REC_W
    per_b = gate.shape[0] > 1
    mod_map = (lambda bi, i: (bi, 0, 0)) if per_b else (lambda bi, i: (0, 0, 0))
    const = lambda bi, i: (0, 0)
    tok = lambda bi, i: (bi, i, 0)
    return pl.pallas_call(
        _outproj_kernel,
        grid=(b, t // tm),
        in_specs=[pl.BlockSpec((1, tm, ATTN_Q_W), tok),
                  pl.BlockSpec((1, tm, n), tok), pl.BlockSpec((1, tm, n), tok),
                  pl.BlockSpec((1, tm, n), lambda bi, i: (bi, i, 3)),
                  pl.BlockSpec((1, tm, n), tok), pl.BlockSpec((1, tm, n), tok),
                  pl.BlockSpec((1, tm, n), lambda bi, i: (bi, i, 4)),
                  pl.BlockSpec((1, n), const), pl.BlockSpec((1, n), const),
                  pl.BlockSpec((n, n), const),
                  pl.BlockSpec((d, d), const),
                  pl.BlockSpec((1, tm, d), tok),
                  pl.BlockSpec((1, 1, d), mod_map), pl.BlockSpec((1, 1, d), mod_map),
                  pl.BlockSpec((1, 1, d), mod_map),
                  pl.BlockSpec((N_EXPERTS, d), const)],
        out_specs=(pl.BlockSpec((1, tm, d), tok), pl.BlockSpec((1, tm, d), tok),
                   pl.BlockSpec((1, N_EXPERTS, tm), lambda bi, i: (bi, 0, i))),
        out_shape=(jax.ShapeDtypeStruct((b, t, d), F32), jax.ShapeDtypeStruct((b, t, d), BF16),
                   jax.ShapeDtypeStruct((b, N_EXPERTS, t), F32)),
        compiler_params=_cparams(("parallel", "parallel")),
        name="output_projection_router",
    )(a, dof, dob, dn, gof, gob, hg, dnw, hgw, gavg, w_out, x, gate, shift, scale, router_t)


def _select_kernel(aff_ref, u_ref, bs_ref, bst_ref, su_ref, slot_ref, base_ref, nsub_ref, inc_scr, *, cap, nblk):
    x = aff_ref[0]
    bits = pltpu.bitcast(x, jnp.int32)
    lo = jnp.zeros((N_EXPERTS, 1), jnp.int32)
    for bit in range(30, -1, -1):
        cand = lo | (1 << bit)
        cnt = jnp.sum((bits >= cand).astype(jnp.int32), axis=1, keepdims=True)
        lo = jnp.where(cnt >= cap, cand, lo)
    gt = bits > lo
    eq = bits == lo
    umat = u_ref[...]

    def prefix(mask):
        mb = mask.astype(BF16)
        for j in range(nblk):
            inc_scr[:, j * LANES:(j + 1) * LANES] = jnp.dot(mb[:, j * LANES:(j + 1) * LANES], umat,
                                                            preferred_element_type=F32)
        totals = jnp.dot(mb, bs_ref[...], preferred_element_type=F32)
        offs = _mm_hi(totals, su_ref[...])
        return inc_scr[...] + _mm_hi(offs, bst_ref[...]), offs, totals

    eqf = eq.astype(F32)
    n_gt = jnp.sum(gt.astype(F32), axis=1, keepdims=True)
    eq_before, _, _ = prefix(eqf)
    sel = gt | (eq & ((eq_before - eqf) < (cap - n_gt)))
    self_ = sel.astype(F32)
    pos, offs, totals = prefix(self_)
    slot_ref[0] = jnp.where(sel, pos - 1.0, -1.0).astype(jnp.int32)
    start = jnp.floor(offs * (1.0 / ROUTE_ALIGN)) * ROUTE_ALIGN
    span = offs + totals - start
    n_sub = functools.reduce(jnp.add, [jnp.where(span > j * ROUTE_SUB, 1.0, 0.0)
                                       for j in range(ROUTE_WIN // ROUTE_SUB)])
    base_ref[0] = start.astype(jnp.int32)
    nsub_ref[0] = jnp.max(n_sub, axis=0, keepdims=True).astype(jnp.int32)


def _moe_select(aff_t, cap):
    b, e, t = aff_t.shape
    nblk = t // ROUTE_BLK
    u = jnp.asarray(np.triu(np.ones((LANES, LANES), np.float32)), BF16)
    blk = np.arange(t) // ROUTE_BLK
    bs = (blk[:, None] == np.arange(nblk)[None, :]).astype(np.float32)
    su = np.triu(np.ones((nblk, nblk), np.float32), 1)
    const = lambda bi: (0, 0)
    return pl.pallas_call(
        functools.partial(_select_kernel, cap=cap, nblk=nblk),
        grid=(b,),
        in_specs=[pl.BlockSpec((1, e, t), lambda bi: (bi, 0, 0)),
                  pl.BlockSpec((LANES, LANES), const),
                  pl.BlockSpec((t, nblk), const),
                  pl.BlockSpec((nblk, t), const),
                  pl.BlockSpec((nblk, nblk), const)],
        out_specs=(pl.BlockSpec((1, e, t), lambda bi: (bi, 0, 0)),
                   pl.BlockSpec((1, e, nblk), lambda bi: (bi, 0, 0)),
                   pl.BlockSpec((1, 1, nblk), lambda bi: (bi, 0, 0))),
        out_shape=(jax.ShapeDtypeStruct((b, e, t), jnp.int32), jax.ShapeDtypeStruct((b, e, nblk), jnp.int32),
                   jax.ShapeDtypeStruct((b, 1, nblk), jnp.int32)),
        scratch_shapes=[pltpu.VMEM((e, t), F32)],
        compiler_params=_cparams(("parallel",)),
        name="moe_select",
    )(aff_t, u, jnp.asarray(bs, BF16), jnp.asarray(bs.T), jnp.asarray(su))


def _gather_kernel(start_sm, nsub_sm, slot_ref, h_ref, xe_ref, *, nblk):
    bi = pl.program_id(0)
    xe_ref[0] = jnp.zeros(xe_ref.shape[1:], BF16)
    row = lax.broadcasted_iota(jnp.int32, (ROUTE_SUB, ROUTE_BLK), 0)

    def body(k, carry):
        t0 = pl.multiple_of(k * ROUTE_BLK, ROUTE_BLK)
        hblk = h_ref[0, pl.ds(t0, ROUTE_BLK), :]
        slots = slot_ref[0, :, pl.ds(t0, ROUTE_BLK)]
        firsts = [start_sm[(bi * N_EXPERTS + e) * nblk + k] for e in range(N_EXPERTS)]

        def sub(j, inner):
            starts = [pl.multiple_of(first + j * ROUTE_SUB, ROUTE_ALIGN) for first in firsts]
            onehot = jnp.concatenate(
                [jnp.where(row == slots[e:e + 1, :] - starts[e], 1.0, 0.0).astype(BF16)
                 for e in range(N_EXPERTS)], axis=0)
            rows = jnp.dot(onehot, hblk, preferred_element_type=F32).astype(BF16)
            for e in range(N_EXPERTS):
                win = pl.ds(starts[e], ROUTE_SUB)
                xe_ref[0, e, win, :] = xe_ref[0, e, win, :] + rows[e * ROUTE_SUB:(e + 1) * ROUTE_SUB]
            return inner

        lax.fori_loop(0, nsub_sm[bi * nblk + k], sub, 0)
        return carry

    lax.fori_loop(0, nblk, body, 0)


def _moe_gather(h2, slot, start_flat, nsub_flat, cap):
    b, t, d = h2.shape
    nblk = t // ROUTE_BLK
    cp = cap + ROUTE_WIN
    dq = _tile(d, 256)
    grid_spec = pltpu.PrefetchScalarGridSpec(
        num_scalar_prefetch=2,
        grid=(b, d // dq),
        in_specs=[pl.BlockSpec((1, N_EXPERTS, t), lambda bi, j, s0, s1: (bi, 0, 0)),
                  pl.BlockSpec((1, t, dq), lambda bi, j, s0, s1: (bi, 0, j))],
        out_specs=pl.BlockSpec((1, N_EXPERTS, cp, dq), lambda bi, j, s0, s1: (bi, 0, 0, j)),
    )
    return pl.pallas_call(
        functools.partial(_gather_kernel, nblk=nblk),
        grid_spec=grid_spec,
        out_shape=jax.ShapeDtypeStruct((b, N_EXPERTS, cp, d), BF16),
        compiler_params=_cparams(("parallel", "parallel")),
        name="moe_gather",
    )(start_flat, nsub_flat, slot, h2)


def _ffn_kernel(x_ref, wg_ref, wu_ref, wd_ref, y_ref, *, cap):
    x = x_ref[0, 0, 0:cap, :]
    a = jnp.dot(x, wg_ref[0], preferred_element_type=F32)
    u = jnp.dot(x, wu_ref[0], preferred_element_type=F32)
    y_ref[0, 0, 0:cap, :] = jnp.dot((_silu(a) * u).astype(BF16), wd_ref[0],
                                    preferred_element_type=F32).astype(BF16)
    y_ref[0, 0, cap:, :] = jnp.zeros((y_ref.shape[2] - cap, y_ref.shape[3]), BF16)


def _moe_ffn(xe, wg, wu, wd, cap):
    b, e, cp, d = xe.shape
    f = wg.shape[-1]
    return pl.pallas_call(
        functools.partial(_ffn_kernel, cap=cap),
        grid=(e, b),
        in_specs=[pl.BlockSpec((1, 1, cp, d), lambda ei, bi: (bi, ei, 0, 0)),
                  pl.BlockSpec((1, d, f), lambda ei, bi: (ei, 0, 0)),
                  pl.BlockSpec((1, d, f), lambda ei, bi: (ei, 0, 0)),
                  pl.BlockSpec((1, f, d), lambda ei, bi: (ei, 0, 0))],
        out_specs=pl.BlockSpec((1, 1, cp, d), lambda ei, bi: (bi, ei, 0, 0)),
        out_shape=jax.ShapeDtypeStruct((b, e, cp, d), BF16),
        compiler_params=_cparams(("parallel", "parallel")),
        name="moe_ffn",
    )(xe, wg, wu, wd)


def _combine_kernel(start_sm, nsub_sm, ye_ref, x_ref, gate_ref, slot_ref, aff_ref, o_ref, *, nblk, per_step):
    bi = pl.program_id(0)
    row = lax.broadcasted_iota(jnp.int32, (ROUTE_SUB, ROUTE_BLK), 0)
    for c in range(per_step):
        k = pl.program_id(2) * per_step + c
        tok = slice(c * ROUTE_BLK, (c + 1) * ROUTE_BLK)
        slots = slot_ref[0, :, tok]
        aff = aff_ref[0, :, tok]
        firsts = [start_sm[(bi * N_EXPERTS + e) * nblk + k] for e in range(N_EXPERTS)]

        def sub(j, acc, slots=slots, aff=aff, firsts=firsts):
            starts = [pl.multiple_of(first + j * ROUTE_SUB, ROUTE_ALIGN) for first in firsts]
            wsel = jnp.concatenate([jnp.where(row == slots[e:e + 1, :] - starts[e], aff[e:e + 1, :], 0.0)
                                    for e in range(N_EXPERTS)], axis=0)
            ys = jnp.concatenate([ye_ref[0, e, pl.ds(starts[e], ROUTE_SUB), :] for e in range(N_EXPERTS)],
                                 axis=0)
            hi, lo = _split2(wsel)
            tn = (((0,), (0,)), ((), ()))
            return (acc + lax.dot_general(hi, ys, tn, preferred_element_type=F32)
                    + lax.dot_general(lo, ys, tn, preferred_element_type=F32))

        acc = lax.fori_loop(0, nsub_sm[bi * nblk + k], sub, jnp.zeros((ROUTE_BLK, x_ref.shape[2]), F32))
        o_ref[0, tok, :] = x_ref[0, tok, :] + gate_ref[0] * acc


def _moe_combine(ye, x1, gate, slot, aff_t, start_flat, nsub_flat):
    b, t, d = x1.shape
    cp = ye.shape[2]
    nblk = t // ROUTE_BLK
    dh = d // 2
    per_b = gate.shape[0] > 1
    per_step = _tile(nblk, COMBINE_BLOCKS)
    tok = per_step * ROUTE_BLK
    grid_spec = pltpu.PrefetchScalarGridSpec(
        num_scalar_prefetch=2,
        grid=(b, 2, nblk // per_step),
        in_specs=[pl.BlockSpec((1, N_EXPERTS, cp, dh), lambda bi, j, k, s0, s1: (bi, 0, 0, j)),
                  pl.BlockSpec((1, tok, dh), lambda bi, j, k, s0, s1: (bi, k, j)),
                  pl.BlockSpec((1, 1, dh), (lambda bi, j, k, s0, s1: (bi, 0, j)) if per_b
                               else (lambda bi, j, k, s0, s1: (0, 0, j))),
                  pl.BlockSpec((1, N_EXPERTS, tok), lambda bi, j, k, s0, s1: (bi, 0, k)),
                  pl.BlockSpec((1, N_EXPERTS, tok), lambda bi, j, k, s0, s1: (bi, 0, k))],
        out_specs=pl.BlockSpec((1, tok, dh), lambda bi, j, k, s0, s1: (bi, k, j)),
    )
    return pl.pallas_call(
        functools.partial(_combine_kernel, nblk=nblk, per_step=per_step),
        grid_spec=grid_spec,
        out_shape=jax.ShapeDtypeStruct((b, t, d), F32),
        compiler_params=_cparams(("parallel", "parallel", "arbitrary")),
        name="moe_combine",
    )(start_flat, nsub_flat, ye, x1, gate, slot, aff_t)


def _expert_choice_ffn(x1, h2, aff_t, gate, wg, wu, wd):
    b, t, _ = x1.shape
    cap = CAPACITY_FACTOR * t // N_EXPERTS
    slot, start, nsub = _moe_select(aff_t, cap)
    start_flat, nsub_flat = start.reshape(-1), nsub.reshape(-1)
    xe = _moe_gather(h2, slot, start_flat, nsub_flat, cap)
    ye = _moe_ffn(xe, wg, wu, wd, cap)
    return _moe_combine(ye, x1, gate, slot, aff_t, start_flat, nsub_flat)


def _rope_tables(n_tokens):
    rows = n_tokens // GRID_W
    row = jnp.repeat(jnp.arange(rows, dtype=F32), GRID_W)
    col = jnp.tile(jnp.arange(GRID_W, dtype=F32), rows)
    n_freq = HEAD_DIM // 4
    inv_freq = ROPE_THETA ** (-jnp.arange(n_freq, dtype=F32) / n_freq)
    ang_r = row[:, None] * inv_freq
    ang_c = col[:, None] * inv_freq
    cos = jnp.concatenate([jnp.cos(ang_r)] * 2 + [jnp.cos(ang_c)] * 2, axis=-1)
    sin = jnp.concatenate([jnp.sin(ang_r)] * 2 + [jnp.sin(ang_c)] * 2, axis=-1)
    return jnp.tile(cos, (1, LANES // HEAD_DIM)), jnp.tile(sin, (1, LANES // HEAD_DIM))


def _block_diag(width, block, value):
    idx = np.arange(width) // block
    return jnp.asarray((idx[:, None] == idx[None, :]).astype(np.float32) * value)


def kernel(x, c, ctx, c_ctx, w_mod, b_mod, w_in, w_out, attn_q_norm, attn_k_norm, dn_conv, dn_a_log,
           dn_dt_bias, dn_norm, hg_lower_bounds, hg_norm, moe_router, moe_w_gate, moe_w_up, moe_w_down):
    depth = w_mod.shape[0]
    b, t_lat, d = x.shape
    cos, sin = _rope_tables(t_lat)
    g_head = _block_diag(LANES, HEAD_DIM, 1.0 / HEAD_DIM)
    g_mean = _block_diag(REC_W, HEAD_DIM, 1.0 / HEAD_DIM)
    g_sum = _block_diag(REC_W, HEAD_DIM, 1.0)
    consts = _scan_constants()
    s_zero = jnp.zeros((b, 2, REC_W, REC_W), F32)

    lb_w = jax.nn.softmax(hg_lower_bounds.astype(F32), axis=0)
    hg_lb = jnp.cumsum(lb_w, axis=0) - lb_w[0]

    rows = ((b + 1 + 7) // 8) * 8
    cond = jnp.zeros((rows, d), F32).at[:b].set(c).at[b].set(c_ctx)
    mod = _modulation(cond, w_mod, b_mod)

    n_small = 4 * DN_HEADS
    w_in_r = jnp.concatenate(
        [w_in[:, :, :C_HG], w_in[:, :, C_HG + n_small:], w_in[:, :, C_HG:C_HG + n_small],
         jnp.zeros((depth, d, C_END - C_BA - n_small), w_in.dtype)], axis=-1).astype(BF16)
    w_out_b = _cast_bf16(w_out)
    wg_b, wu_b, wd_b = _cast_bf16(moe_w_gate), _cast_bf16(moe_w_up), _cast_bf16(moe_w_down)

    x_lat, x_ctx = x, ctx
    for l in range(depth):
        ctx_out = l < depth - 1
        m_lat = [mod[l, :b, j * d:(j + 1) * d][:, None, :] for j in range(6)]
        m_ctx = [mod[l, b:b + 1, j * d:(j + 1) * d][:, None, :] for j in range(6)]
        qn = jnp.tile(attn_q_norm[l], LANES // HEAD_DIM)[None, :]
        kn = jnp.tile(attn_k_norm[l], LANES // HEAD_DIM)[None, :]
        conv_w = jnp.zeros((8, 3 * REC_W), F32).at[:CONV_K].set(dn_conv[l])
        pad = jnp.zeros((LANES - 4 * DN_HEADS,), F32)
        neg_a = jnp.concatenate([jnp.zeros((2 * DN_HEADS,), F32), -jnp.exp(dn_a_log[l].reshape(-1)), pad])[None, :]
        dt_b = jnp.concatenate([jnp.zeros((2 * DN_HEADS,), F32), dn_dt_bias[l].reshape(-1), pad])[None, :]
        dnw = jnp.tile(dn_norm[l], DN_HEADS)[None, :]
        hgw = jnp.tile(hg_norm[l], HG_HEADS)[None, :]
        router_t = moe_router[l].T

        streams = {}
        for name, xs, ms, rotate in (("ctx", x_ctx, m_ctx, False), ("lat", x_lat, m_lat, True)):
            t = xs.shape[1]
            q, kt, v, dn, hg, ba = _input_projection(xs, ms[0], ms[1], w_in_r[l], qn, kn,
                                                     cos[:t], sin[:t], g_head, rotate)
            qkv, gb = _dn_prep(dn, ba, conv_w, neg_a, dt_b, g_sum)
            streams[name] = dict(q=q, kv=(kt, v), dn=dn, hg=hg, qkv=qkv, gb=gb)

        sc, sl = streams["ctx"], streams["lat"]
        dcf, dcb, dn_state = _dn_scan(sc["qkv"], sc["gb"], s_zero, consts)
        dlf, dlb, _ = _dn_scan(sl["qkv"], sl["gb"], dn_state, consts)
        gcf, gcb, hg_state = _hg_scan(sc["hg"], hg_lb[l], s_zero, consts)
        glf, glb, _ = _hg_scan(sl["hg"], hg_lb[l], hg_state, consts)

        a_lat = _attention(sl["q"], [sc["kv"], sl["kv"]], t_lat)
        x1, h2, aff_t = _output_projection(
            a_lat, dlf, dlb, sl["dn"], glf, glb, sl["hg"], dnw, hgw, g_mean, w_out_b[l], x_lat,
            m_lat[2], m_lat[3], m_lat[4], router_t)
        x_lat = _expert_choice_ffn(x1, h2, aff_t, m_lat[5], wg_b[l], wu_b[l], wd_b[l])
        if ctx_out:
            a_ctx = _attention(sc["q"], [sc["kv"]], x_ctx.shape[1])
            x1, h2, aff_t = _output_projection(
                a_ctx, dcf, dcb, sc["dn"], gcf, gcb, sc["hg"], dnw, hgw, g_mean, w_out_b[l], x_ctx,
                m_ctx[2], m_ctx[3], m_ctx[4], router_t)
            x_ctx = _expert_choice_ffn(x1, h2, aff_t, m_ctx[5], wg_b[l], wu_b[l], wd_b[l])
    return x_lat
```

```python
import functools

import numpy as np
import jax
import jax.numpy as jnp
from jax import lax
from jax.experimental import pallas as pl
from jax.experimental.pallas import tpu as pltpu

F32 = jnp.float32
BF16 = jnp.bfloat16
HIGHEST = lax.Precision.HIGHEST

HEAD_DIM = 64
N_Q_HEADS = 8
N_KV_HEADS = 2
GQA_GROUP = N_Q_HEADS // N_KV_HEADS
DN_HEADS = 4
HG_HEADS = 4
GRID_W = 64
ROPE_THETA = 10000.0
CONV_K = 5
CHUNK = 64
N_EXPERTS = 16
CAPACITY_FACTOR = 2
EPS = 1e-6
LOG2E = 1.4426950408889634
V_ROWS = HEAD_DIM + 16

ATTN_Q_W = N_Q_HEADS * HEAD_DIM
ATTN_KV_W = N_KV_HEADS * HEAD_DIM
REC_W = DN_HEADS * HEAD_DIM
LANES = 128
ROUTE_BLK = LANES
ROUTE_ALIGN = 16
ROUTE_WIN = ROUTE_BLK + ROUTE_ALIGN
COMBINE_BLOCKS = 4
ROUTE_SUB = ROUTE_WIN // 3
N_LEVELS = 6
SCAN_BATCH = 4
VMEM_LIMIT = 56 * 1024 * 1024

C_Q, C_K, C_V, C_DN, C_HG, C_BA, C_END = 0, 512, 640, 768, 1792, 3072, 3200


def _mm(a, b):
    return jnp.dot(a.astype(BF16), b.astype(BF16), preferred_element_type=F32)


def _mm_nt(a, b):
    return lax.dot_general(a.astype(BF16), b.astype(BF16), (((1,), (1,)), ((), ())),
                           preferred_element_type=F32)


def _mm_tn(a, b):
    return lax.dot_general(a.astype(BF16), b.astype(BF16), (((0,), (0,)), ((), ())),
                           preferred_element_type=F32)


def _mm_hi(a, b):
    return jnp.dot(a, b, precision=HIGHEST, preferred_element_type=F32)


def _mm_group(a, gmat):
    hi = a.astype(BF16)
    lo = (a - hi.astype(F32)).astype(BF16)
    t = jnp.dot(jnp.concatenate([hi, lo], axis=0), gmat.astype(BF16), preferred_element_type=F32)
    return t[:a.shape[0]] + t[a.shape[0]:]


def _sigmoid(x):
    return 1.0 / (1.0 + jnp.exp(-x))


def _silu(x):
    return x * _sigmoid(x)


def _cparams(sem):
    return pltpu.CompilerParams(dimension_semantics=sem, vmem_limit_bytes=VMEM_LIMIT)


def _tile(n, pref):
    return pref if n % pref == 0 else n


def _cast_kernel(x_ref, o_ref):
    o_ref[...] = x_ref[...].astype(BF16)


def _cast_bf16(w):
    r, c = w.shape[-2:]
    w3 = w.reshape(-1, r, c)
    rb = _tile(r, 512)
    out = pl.pallas_call(
        _cast_kernel,
        grid=(w3.shape[0], r // rb),
        in_specs=[pl.BlockSpec((1, rb, c), lambda i, j: (i, j, 0))],
        out_specs=pl.BlockSpec((1, rb, c), lambda i, j: (i, j, 0)),
        out_shape=jax.ShapeDtypeStruct(w3.shape, BF16),
        compiler_params=_cparams(("parallel", "parallel")),
        name="cast_bf16",
    )(w3)
    return out.reshape(w.shape)


def _mod_kernel(c_ref, w_ref, b_ref, o_ref):
    o_ref[0] = _mm_hi(_silu(c_ref[...]), w_ref[0]) + b_ref[0]


def _modulation(cond, w_mod, b_mod):
    depth, d, n = w_mod.shape
    rows = cond.shape[0]
    tn = _tile(n, 1536)
    return pl.pallas_call(
        _mod_kernel,
        grid=(depth, n // tn),
        in_specs=[pl.BlockSpec((rows, d), lambda l, j: (0, 0)),
                  pl.BlockSpec((1, d, tn), lambda l, j: (l, 0, j)),
                  pl.BlockSpec((1, 1, tn), lambda l, j: (l, 0, j))],
        out_specs=pl.BlockSpec((1, rows, tn), lambda l, j: (l, 0, j)),
        out_shape=jax.ShapeDtypeStruct((depth, rows, n), F32),
        compiler_params=_cparams(("parallel", "parallel")),
        name="modulation",
    )(cond, w_mod, b_mod.reshape(depth, 1, n))


def _inproj_kernel(x_ref, sh_ref, sc_ref, w_ref, qn_ref, kn_ref, cos_ref, sin_ref, g_ref,
                   q_ref, k_ref, v_ref, dn_ref, hg_ref, ba_ref, *, rotate):
    x = x_ref[0]
    ms = jnp.mean(x * x, axis=-1, keepdims=True)
    h = x * lax.rsqrt(ms + EPS) * (1.0 + sc_ref[0]) + sh_ref[0]
    hb = h.astype(BF16)
    gmat = g_ref[...]

    def proj(lo, hi):
        return jnp.dot(hb, w_ref[:, lo:hi], preferred_element_type=F32)

    def head_norm_rope(t, nw):
        t = t * lax.rsqrt(_mm_group(t * t, gmat) + EPS) * nw
        if rotate:
            lane = lax.broadcasted_iota(jnp.int32, t.shape, 1)
            first = (lane % 32) < 16
            rot = jnp.where(first, -pltpu.roll(t, LANES - 16, 1), pltpu.roll(t, 16, 1))
            t = t * cos_ref[...] + rot * sin_ref[...]
        return t

    def q_group(j):
        t = proj(C_Q + j * LANES, C_Q + (j + 1) * LANES)
        yield
        qj = head_norm_rope(t, qn_ref[...])
        qt = (qj * (HEAD_DIM ** -0.5 * LOG2E)).T.astype(BF16)
        q_ref[0, 2 * j] = qt[:HEAD_DIM]
        q_ref[0, 2 * j + 1] = qt[HEAD_DIM:]

    def k_group():
        t = proj(C_K, C_V)
        yield
        k = head_norm_rope(t, kn_ref[...]).astype(BF16)
        k_ref[0, 0] = k[:, :HEAD_DIM]
        k_ref[0, 1] = k[:, HEAD_DIM:]

    def v_group():
        v = proj(C_V, C_DN)
        yield
        vt = v.T
        ones = jnp.ones((V_ROWS - HEAD_DIM, vt.shape[1]), F32)
        v_ref[0, 0] = jnp.concatenate([vt[:HEAD_DIM], ones], axis=0).astype(BF16)
        v_ref[0, 1] = jnp.concatenate([vt[HEAD_DIM:], ones], axis=0).astype(BF16)

    def plain_group(o_ref, base, lo, hi):
        t = proj(base + lo, base + hi)
        yield
        o_ref[0, :, lo:hi] = t

    step = 4 * LANES
    plain = [plain_group(o_ref, base, lo, min(lo + step, width))
             for o_ref, base, width in ((dn_ref, C_DN, C_HG - C_DN), (hg_ref, C_HG, C_BA - C_HG),
                                        (ba_ref, C_BA, C_END - C_BA))
             for lo in range(0, width, step)]
    _lockstep([q_group(j) for j in range(ATTN_Q_W // LANES)] + [k_group(), v_group()] + plain, stagger=True)


def _input_projection(x, shift, scale, w, qn, kn, cos, sin, gmat, rotate):
    b, t, d = x.shape
    tm = _tile(t, 512)
    per_b = shift.shape[0] > 1
    mod_map = (lambda bi, i: (bi, 0, 0)) if per_b else (lambda bi, i: (0, 0, 0))
    const = lambda bi, i: (0, 0)
    out_shape = (
        jax.ShapeDtypeStruct((b, N_Q_HEADS, HEAD_DIM, t), BF16),
        jax.ShapeDtypeStruct((b, N_KV_HEADS, t, HEAD_DIM), BF16),
        jax.ShapeDtypeStruct((b, N_KV_HEADS, V_ROWS, t), BF16),
        jax.ShapeDtypeStruct((b, t, C_HG - C_DN), F32),
        jax.ShapeDtypeStruct((b, t, C_BA - C_HG), F32),
        jax.ShapeDtypeStruct((b, t, C_END - C_BA), F32),
    )
    return pl.pallas_call(
        functools.partial(_inproj_kernel, rotate=rotate),
        grid=(b, t // tm),
        in_specs=[pl.BlockSpec((1, tm, d), lambda bi, i: (bi, i, 0)),
                  pl.BlockSpec((1, 1, d), mod_map),
                  pl.BlockSpec((1, 1, d), mod_map),
                  pl.BlockSpec((d, C_END), const),
                  pl.BlockSpec((1, LANES), const),
                  pl.BlockSpec((1, LANES), const),
                  pl.BlockSpec((tm, LANES), lambda bi, i: (i, 0)),
                  pl.BlockSpec((tm, LANES), lambda bi, i: (i, 0)),
                  pl.BlockSpec((LANES, LANES), const)],
        out_specs=(pl.BlockSpec((1, N_Q_HEADS, HEAD_DIM, tm), lambda bi, i: (bi, 0, 0, i)),
                   pl.BlockSpec((1, N_KV_HEADS, tm, HEAD_DIM), lambda bi, i: (bi, 0, i, 0)),
                   pl.BlockSpec((1, N_KV_HEADS, V_ROWS, tm), lambda bi, i: (bi, 0, 0, i)),
                   pl.BlockSpec((1, tm, C_HG - C_DN), lambda bi, i: (bi, i, 0)),
                   pl.BlockSpec((1, tm, C_BA - C_HG), lambda bi, i: (bi, i, 0)),
                   pl.BlockSpec((1, tm, C_END - C_BA), lambda bi, i: (bi, i, 0))),
        out_shape=out_shape,
        compiler_params=_cparams(("parallel", "parallel")),
        name="input_projection",
    )(x, shift, scale, w, qn, kn, cos, sin, gmat)


def _lockstep(gens, stagger=False):
    results = [None] * len(gens)
    live, started = [], 0
    while live or started < len(gens):
        fresh = 1 if stagger else len(gens)
        live += list(range(started, min(started + fresh, len(gens))))
        started = min(started + fresh, len(gens))
        for i in reversed(list(live)):
            try:
                next(gens[i])
            except StopIteration as stop:
                results[i] = stop.value
                live.remove(i)
    return results


def _attn_head(qt, ks, vts):
    ss = [jnp.dot(k[0, 0], qt, preferred_element_type=F32).astype(BF16) for k in ks]
    yield
    m = functools.reduce(jnp.maximum, [jnp.max(s, axis=0, keepdims=True) for s in ss])
    ps = [jnp.exp2(s - m) for s in ss]
    yield
    o = functools.reduce(jnp.add, [jnp.dot(vt[0, 0], p, preferred_element_type=F32) for p, vt in zip(ps, vts)])
    return o[:HEAD_DIM] / o[HEAD_DIM:HEAD_DIM + 1]


def _attn_kernel(*refs, n_src):
    q_ref, o_ref = refs[0], refs[-1]
    ks = [refs[1 + 2 * i] for i in range(n_src)]
    vts = [refs[2 + 2 * i] for i in range(n_src)]
    outs = _lockstep([_attn_head(q_ref[0, h], ks, vts) for h in range(GQA_GROUP)], stagger=True)
    o_ref[0] = jnp.concatenate(outs, axis=0).T.astype(BF16)


def _attention(qt, sources, t):
    b = qt.shape[0]
    tq = _tile(t, 256)
    gw = GQA_GROUP * HEAD_DIM
    in_specs = [pl.BlockSpec((1, GQA_GROUP, HEAD_DIM, tq), lambda bi, g, i: (bi, g, 0, i))]
    args = [qt]
    for k, vt in sources:
        tk = k.shape[2]
        in_specs.append(pl.BlockSpec((1, 1, tk, HEAD_DIM), lambda bi, g, i: (bi, g, 0, 0)))
        in_specs.append(pl.BlockSpec((1, 1, V_ROWS, tk), lambda bi, g, i: (bi, g, 0, 0)))
        args += [k, vt]
    return pl.pallas_call(
        functools.partial(_attn_kernel, n_src=len(sources)),
        grid=(b, N_KV_HEADS, t // tq),
        in_specs=in_specs,
        out_specs=pl.BlockSpec((1, tq, gw), lambda bi, g, i: (bi, i, g)),
        out_shape=jax.ShapeDtypeStruct((b, t, ATTN_Q_W), BF16),
        compiler_params=_cparams(("parallel", "parallel", "parallel")),
        name="attention",
    )(*args)


def _dn_prep_kernel(x_ref, xp_ref, xn_ref, ba_ref, cw_ref, na_ref, dtb_ref, g_ref, o_ref, gb_ref, buf):
    i = pl.program_id(1)
    n = pl.num_programs(1)
    tm = x_ref.shape[1]
    buf[0:8] = jnp.where(i > 0, xp_ref[0], 0.0)
    buf[8:8 + tm] = x_ref[0]
    buf[8 + tm:16 + tm] = jnp.where(i < n - 1, xn_ref[0], 0.0)
    half = CONV_K // 2
    y = cw_ref[0:1, :] * buf[8 - half:8 - half + tm]
    for j in range(1, CONV_K):
        y = y + cw_ref[j:j + 1, :] * buf[8 - half + j:8 - half + j + tm]
    y = _silu(y)
    gmat = g_ref[...]
    q = y[:, :REC_W]
    k = y[:, REC_W:2 * REC_W]
    o_ref[0, :, 0:REC_W] = q * lax.rsqrt(_mm_group(q * q, gmat) + EPS) * HEAD_DIM ** -0.5
    o_ref[0, :, REC_W:2 * REC_W] = k * lax.rsqrt(_mm_group(k * k, gmat) + EPS)
    o_ref[0, :, 2 * REC_W:] = y[:, 2 * REC_W:]
    ba = ba_ref[0]
    z = ba + dtb_ref[...]
    softplus = jnp.maximum(z, 0.0) + jnp.log1p(jnp.exp(-jnp.abs(z)))
    lane = lax.broadcasted_iota(jnp.int32, ba.shape, 1)
    gb_ref[0] = jnp.where(lane < 2 * DN_HEADS, _sigmoid(ba), na_ref[...] * softplus)


def _dn_prep(dn, ba, conv_w, neg_a, dt_bias, gsum):
    b, t, _ = dn.shape
    tm = _tile(t, 512)
    w3 = 3 * REC_W
    nb8 = t // 8
    return pl.pallas_call(
        _dn_prep_kernel,
        grid=(b, t // tm),
        in_specs=[pl.BlockSpec((1, tm, w3), lambda bi, i: (bi, i, 0)),
                  pl.BlockSpec((1, 8, w3), lambda bi, i: (bi, jnp.maximum(i * (tm // 8) - 1, 0), 0)),
                  pl.BlockSpec((1, 8, w3), lambda bi, i: (bi, jnp.minimum((i + 1) * (tm // 8), nb8 - 1), 0)),
                  pl.BlockSpec((1, tm, LANES), lambda bi, i: (bi, i, 0)),
                  pl.BlockSpec((8, w3), lambda bi, i: (0, 0)),
                  pl.BlockSpec((1, LANES), lambda bi, i: (0, 0)),
                  pl.BlockSpec((1, LANES), lambda bi, i: (0, 0)),
                  pl.BlockSpec((REC_W, REC_W), lambda bi, i: (0, 0))],
        out_specs=(pl.BlockSpec((1, tm, w3), lambda bi, i: (bi, i, 0)),
                   pl.BlockSpec((1, tm, LANES), lambda bi, i: (bi, i, 0))),
        out_shape=(jax.ShapeDtypeStruct((b, t, w3), F32), jax.ShapeDtypeStruct((b, t, LANES), F32)),
        scratch_shapes=[pltpu.VMEM((tm + 16, w3), F32)],
        compiler_params=_cparams(("parallel", "parallel")),
        name="deltanet_prep",
    )(dn, dn, dn, ba, conv_w, neg_a, dt_bias, gsum)


def _split2(a):
    hi = a.astype(BF16)
    return hi, (a - hi.astype(F32)).astype(BF16)


def _split3(a):
    hi = a.astype(BF16)
    r = a - hi.astype(F32)
    mid = r.astype(BF16)
    return hi, mid, (r - mid.astype(F32)).astype(BF16)


def _bdiag(a, hm):
    return jnp.concatenate([a] * DN_HEADS, axis=0) * hm


def _mm_bd(a, b, hm):
    return jnp.dot(a.astype(BF16), _bdiag(b.astype(BF16), hm), preferred_element_type=F32)


def _mm_sel_l(sel, b, terms=3):
    n = b.shape[1]
    parts = _split3(b) if terms == 3 else _split2(b)
    t = jnp.dot(sel.astype(BF16), jnp.concatenate(parts, axis=1), preferred_element_type=F32)
    return functools.reduce(jnp.add, [t[:, i * n:(i + 1) * n] for i in range(terms)])


def _mm_sel_r(a, sel):
    m = a.shape[0]
    t = jnp.dot(jnp.concatenate(_split3(a), axis=0), sel.astype(BF16), preferred_element_type=F32)
    return t[:m] + t[m:2 * m] + t[2 * m:]


def _dn_chunk(x, gb, st, ex, lmat, causal, lvl_ref, d, hm):
    n = REC_W
    lvl = lambda j: lvl_ref[d, j * CHUNK:(j + 1) * CHUNK, :]
    eye = lvl(N_LEVELS)
    ones8 = jnp.ones((8, CHUNK), F32)
    q, k, v = x[:, :n], x[:, n:2 * n], x[:, 2 * n:]
    gbx = _mm_sel_r(gb, ex)
    yield
    beta, g = gbx[:, :n], gbx[:, n:]
    cum = _mm_sel_l(lmat, g)
    last = 0 if d else CHUNK - 1
    clast = cum[last:last + 1]
    kb = k * beta
    ks = _bdiag(k.astype(BF16), hm)
    kk = lax.dot_general(kb.astype(BF16), ks, (((1,), (1,)), ((), ())), preferred_element_type=F32)
    qk = lax.dot_general(q.astype(BF16), ks, (((1,), (1,)), ((), ())), preferred_element_type=F32)
    yield
    cum_s = _mm_sel_l(ones8, cum * eye)[0:1]
    ecum = jnp.exp(cum)
    kdec = k * jnp.exp(clast - cum)
    yield
    decay = jnp.where(causal, jnp.exp(jnp.where(causal, cum - cum_s, 0.0)), 0.0)
    m = kk * decay
    attn = qk * decay
    inv = eye - lvl(N_LEVELS - 1) * m
    for lev in range(N_LEVELS - 2, -1, -1):
        half = _mm_bd(inv, lvl(lev) * m, hm)
        yield
        inv = inv - _mm_bd(half, inv, hm)
        yield
    rhs = v * beta - _mm_nt(kb * ecum, st)
    qs = _mm_nt(q * ecum, st)
    yield
    v_new = _mm_bd(inv, rhs, hm)
    yield
    out = qs + _mm_bd(attn, v_new, hm)
    st_new = st * jnp.exp(clast) + _mm_tn(v_new, kdec) * hm.astype(F32)
    return out, st_new


def _hg_chunk(qr, fr, v, lb, st, amat, lvl_ref, d, hm):
    lvl = lambda j: lvl_ref[d, j * CHUNK:(j + 1) * CHUNK, :]
    last = 0 if d else CHUNK - 1
    q = _silu(qr)
    f = lb + (1.0 - lb) * _sigmoid(fr)
    k = 1.0 - f
    g = jnp.log(f)
    ex = jnp.exp(_mm_sel_l(amat, g, terms=2))
    yield

    def level(xl, mask):
        kl = _bdiag((k * xl).astype(BF16), hm)
        return lax.dot_general((q * xl).astype(BF16), kl, (((1,), (1,)), ((), ())),
                               preferred_element_type=F32) * mask

    attn = level(1.0, lvl(N_LEVELS))
    for lev in range(N_LEVELS):
        attn = attn + level(ex[(2 + lev) * CHUNK:(3 + lev) * CHUNK], lvl(lev))
    qs = _mm_nt(q * ex[0:CHUNK], st)
    vk = _mm_tn(v, k * ex[CHUNK:2 * CHUNK])
    yield
    out = _mm_bd(attn, v, hm) + qs
    st_new = st * ex[last:last + 1] + vk * hm.astype(F32)
    return out, st_new


def _dn_scan_kernel(xf_ref, xb_ref, gf_ref, gb_ref, s0_ref, ex_ref, l_ref, cz_ref, lvl_ref, hm_ref,
                    of_ref, ob_ref, sf_ref, s_scr, *, nchunk):
    @pl.when(pl.program_id(1) == 0)
    def _():
        s_scr[...] = s0_ref[...]

    hm = hm_ref[...]
    refs = ((xf_ref, gf_ref, of_ref), (xb_ref, gb_ref, ob_ref))

    def body(c, carry):
        rows = [pl.ds(pl.multiple_of(cc * CHUNK, CHUNK), CHUNK) for cc in (c, nchunk - 1 - c)]
        insts = [(bi, d) for bi in range(s_scr.shape[0]) for d in range(2)]
        res = _lockstep([_dn_chunk(refs[d][0][bi, rows[d], :], refs[d][1][bi, rows[d], :], s_scr[bi, d],
                                   ex_ref[d], l_ref[d], cz_ref[d] > 0, lvl_ref, d, hm) for bi, d in insts])
        for (bi, d), (out, s_new) in zip(insts, res):
            refs[d][2][bi, rows[d], :] = out
            s_scr[bi, d] = s_new
        return carry

    lax.fori_loop(0, nchunk, body, 0)
    sf_ref[...] = s_scr[...]


def _hg_scan_kernel(qf_ref, qb_ref, ff_ref, fb_ref, if_ref, ib_ref, lb_ref, s0_ref, a_ref, lvl_ref, hm_ref,
                    of_ref, ob_ref, sf_ref, s_scr, *, nchunk):
    @pl.when(pl.program_id(1) == 0)
    def _():
        s_scr[...] = s0_ref[...]

    hm = hm_ref[...]
    refs = ((qf_ref, ff_ref, if_ref, of_ref), (qb_ref, fb_ref, ib_ref, ob_ref))

    def body(c, carry):
        rows = [pl.ds(pl.multiple_of(cc * CHUNK, CHUNK), CHUNK) for cc in (c, nchunk - 1 - c)]
        insts = [(bi, d) for bi in range(s_scr.shape[0]) for d in range(2)]
        res = _lockstep([_hg_chunk(refs[d][0][bi, rows[d], :], refs[d][1][bi, rows[d], :],
                                   refs[d][2][bi, rows[d], :], lb_ref[d:d + 1, :], s_scr[bi, d], a_ref[d],
                                   lvl_ref, d, hm) for bi, d in insts])
        for (bi, d), (out, s_new) in zip(insts, res):
            refs[d][3][bi, rows[d], :] = out
            s_scr[bi, d] = s_new
        return carry

    lax.fori_loop(0, nchunk, body, 0)
    sf_ref[...] = s_scr[...]


def _dn_scan(qkv, gb, s0, consts):
    b, t, w3 = qkv.shape
    tb = _tile(t, 512)
    nblk = t // tb
    n = REC_W
    nb = _tile(b, SCAN_BATCH)
    fwd = lambda bi, i: (bi, i, 0)
    bwd = lambda bi, i: (bi, nblk - 1 - i, 0)
    c3 = lambda bi, i: (0, 0, 0)
    state = pl.BlockSpec((nb, 2, n, n), lambda bi, i: (bi, 0, 0, 0))
    return pl.pallas_call(
        functools.partial(_dn_scan_kernel, nchunk=tb // CHUNK),
        grid=(b // nb, nblk),
        in_specs=[pl.BlockSpec((nb, tb, w3), fwd), pl.BlockSpec((nb, tb, w3), bwd),
                  pl.BlockSpec((nb, tb, LANES), fwd), pl.BlockSpec((nb, tb, LANES), bwd),
                  state,
                  pl.BlockSpec((2, LANES, 2 * n), c3),
                  pl.BlockSpec((2, CHUNK, CHUNK), c3),
                  pl.BlockSpec((2, CHUNK, n), c3),
                  pl.BlockSpec((2, (N_LEVELS + 1) * CHUNK, n), c3),
                  pl.BlockSpec((n, n), lambda bi, i: (0, 0))],
        out_specs=(pl.BlockSpec((nb, tb, n), fwd), pl.BlockSpec((nb, tb, n), bwd), state),
        out_shape=(jax.ShapeDtypeStruct((b, t, n), F32), jax.ShapeDtypeStruct((b, t, n), F32),
                   jax.ShapeDtypeStruct((b, 2, n, n), F32)),
        scratch_shapes=[pltpu.VMEM((nb, 2, n, n), F32)],
        compiler_params=_cparams(("parallel", "arbitrary")),
        name="deltanet_scan",
    )(qkv, qkv, gb, gb, s0, consts["ex"], consts["l"], consts["cz"], consts["lvl"], consts["hm"])


def _hg_scan(hg, lb, s0, consts):
    b, t, _ = hg.shape
    tb = _tile(t, 512)
    nblk = t // tb
    n = REC_W
    nb = _tile(b, SCAN_BATCH)
    fwd = lambda col: (lambda bi, i: (bi, i, col))
    bwd = lambda col: (lambda bi, i: (bi, nblk - 1 - i, col))
    c3 = lambda bi, i: (0, 0, 0)
    blk = lambda m: pl.BlockSpec((nb, tb, n), m)
    state = pl.BlockSpec((nb, 2, n, n), lambda bi, i: (bi, 0, 0, 0))
    return pl.pallas_call(
        functools.partial(_hg_scan_kernel, nchunk=tb // CHUNK),
        grid=(b // nb, nblk),
        in_specs=[blk(fwd(0)), blk(bwd(0)), blk(fwd(1)), blk(bwd(2)), blk(fwd(3)), blk(bwd(3)),
                  pl.BlockSpec((2, n), lambda bi, i: (0, 0)),
                  state,
                  pl.BlockSpec((2, (2 + N_LEVELS) * CHUNK, CHUNK), c3),
                  pl.BlockSpec((2, (N_LEVELS + 1) * CHUNK, n), c3),
                  pl.BlockSpec((n, n), lambda bi, i: (0, 0))],
        out_specs=(blk(fwd(0)), blk(bwd(0)), state),
        out_shape=(jax.ShapeDtypeStruct((b, t, n), F32), jax.ShapeDtypeStruct((b, t, n), F32),
                   jax.ShapeDtypeStruct((b, 2, n, n), F32)),
        scratch_shapes=[pltpu.VMEM((nb, 2, n, n), F32)],
        compiler_params=_cparams(("parallel", "arbitrary")),
        name="hgrn2_scan",
    )(hg, hg, hg, hg, hg, hg, lb, s0, consts["a"], consts["lvl"], consts["hm"])


def _scan_constants_dir(rev):
    t = np.arange(CHUNK)
    p = (CHUNK - 1 - t) if rev else t
    pt, pu = p[:, None], p[None, :]
    causal = (pu <= pt).astype(np.float32)
    head = np.arange(REC_W) // CHUNK
    hm = (head[:, None] == head[None, :]).astype(np.float32)
    rows = [causal, (pu > pt).astype(np.float32)]
    masks = []
    for lev in range(N_LEVELS):
        blk = CHUNK >> (lev + 1)
        bound = (pt // (2 * blk)) * (2 * blk) + blk - 1
        right = (pt % (2 * blk)) >= blk
        rows.append(np.where(right, (pu > bound) & (pu <= pt), (pu > pt) & (pu <= bound)).astype(np.float32))
        masks.append(((pt // (2 * blk) == pu // (2 * blk)) & right & ((pu % (2 * blk)) < blk)).astype(np.float32))
    masks.append(np.eye(CHUNK, dtype=np.float32))
    d = 1 if rev else 0
    ex = np.zeros((LANES, 2 * REC_W), np.float32)
    for h in range(DN_HEADS):
        ex[d * DN_HEADS + h, h * CHUNK:(h + 1) * CHUNK] = 1.0
        ex[2 * DN_HEADS + d * DN_HEADS + h, REC_W + h * CHUNK:REC_W + (h + 1) * CHUNK] = 1.0
    return {
        "l": causal, "cz": np.tile(causal, (1, DN_HEADS)), "hm": hm, "ex": ex,
        "a": np.concatenate(rows, axis=0),
        "lvl": np.concatenate([np.tile(mk, (1, DN_HEADS)) for mk in masks], axis=0),
    }


def _scan_constants():
    fwd, bwd = _scan_constants_dir(False), _scan_constants_dir(True)
    out = {key: jnp.asarray(np.stack([fwd[key], bwd[key]])) for key in ("l", "cz", "ex", "a", "lvl")}
    out["hm"] = jnp.asarray(fwd["hm"], BF16)
    return out


def _outproj_kernel(a_ref, dof_ref, dob_ref, z_ref, gof_ref, gob_ref, gg_ref, dnw_ref, hgw_ref, g_ref,
                    w_ref, x_ref, gate_ref, sh_ref, sc_ref, rt_ref, x1_ref, h2_ref, aff_ref):
    gmat = g_ref[...]

    def gated(o, z, nw):
        return o * lax.rsqrt(_mm_group(o * o, gmat) + EPS) * nw * _silu(z)

    dmix = gated(dof_ref[0] + dob_ref[0], z_ref[0], dnw_ref[...])
    gmix = gated(gof_ref[0] + gob_ref[0], gg_ref[0], hgw_ref[...])
    y = (jnp.dot(a_ref[0], w_ref[0:ATTN_Q_W, :], preferred_element_type=F32)
         + jnp.dot(dmix.astype(BF16), w_ref[ATTN_Q_W:ATTN_Q_W + REC_W, :], preferred_element_type=F32)
         + jnp.dot(gmix.astype(BF16), w_ref[ATTN_Q_W + REC_W:, :], preferred_element_type=F32))
    x1 = x_ref[0] + gate_ref[0] * y
    x1_ref[0] = x1
    ms = jnp.mean(x1 * x1, axis=-1, keepdims=True)
    h2 = x1 * lax.rsqrt(ms + EPS) * (1.0 + sc_ref[0]) + sh_ref[0]
    h2_ref[0] = h2.astype(BF16)
    logits = lax.dot_general(rt_ref[...], h2, (((1,), (1,)), ((), ())), precision=HIGHEST,
                             preferred_element_type=F32)
    e = jnp.exp(logits - jnp.max(logits, axis=0, keepdims=True))
    aff_ref[0] = e / jnp.sum(e, axis=0, keepdims=True)


def _output_projection(a, dof, dob, dn, gof, gob, hg, dnw, hgw, gavg, w_out, x, gate, shift, scale, router_t):
    b, t, d = x.shape
    tm = _tile(t, 512)
    n = REC_W
    per_b = gate.shape[0] > 1
    mod_map = (lambda bi, i: (bi, 0, 0)) if per_b else (lambda bi, i: (0, 0, 0))
    const = lambda bi, i: (0, 0)
    tok = lambda bi, i: (bi, i, 0)
    return pl.pallas_call(
        _outproj_kernel,
        grid=(b, t // tm),
        in_specs=[pl.BlockSpec((1, tm, ATTN_Q_W), tok),
                  pl.BlockSpec((1, tm, n), tok), pl.BlockSpec((1, tm, n), tok),
                  pl.BlockSpec((1, tm, n), lambda bi, i: (bi, i, 3)),
                  pl.BlockSpec((1, tm, n), tok), pl.BlockSpec((1, tm, n), tok),
                  pl.BlockSpec((1, tm, n), lambda bi, i: (bi, i, 4)),
                  pl.BlockSpec((1, n), const), pl.BlockSpec((1, n), const),
                  pl.BlockSpec((n, n), const),
                  pl.BlockSpec((d, d), const),
                  pl.BlockSpec((1, tm, d), tok),
                  pl.BlockSpec((1, 1, d), mod_map), pl.BlockSpec((1, 1, d), mod_map),
                  pl.BlockSpec((1, 1, d), mod_map),
                  pl.BlockSpec((N_EXPERTS, d), const)],
        out_specs=(pl.BlockSpec((1, tm, d), tok), pl.BlockSpec((1, tm, d), tok),
                   pl.BlockSpec((1, N_EXPERTS, tm), lambda bi, i: (bi, 0, i))),
        out_shape=(jax.ShapeDtypeStruct((b, t, d), F32), jax.ShapeDtypeStruct((b, t, d), BF16),
                   jax.ShapeDtypeStruct((b, N_EXPERTS, t), F32)),
        compiler_params=_cparams(("parallel", "parallel")),
        name="output_projection_router",
    )(a, dof, dob, dn, gof, gob, hg, dnw, hgw, gavg, w_out, x, gate, shift, scale, router_t)


def _select_kernel(aff_ref, u_ref, bs_ref, bst_ref, su_ref, slot_ref, base_ref, nsub_ref, inc_scr, *, cap, nblk):
    x = aff_ref[0]
    bits = pltpu.bitcast(x, jnp.int32)
    lo = jnp.zeros((N_EXPERTS, 1), jnp.int32)
    for bit in range(30, -1, -1):
        cand = lo | (1 << bit)
        cnt = jnp.sum((bits >= cand).astype(jnp.int32), axis=1, keepdims=True)
        lo = jnp.where(cnt >= cap, cand, lo)
    gt = bits > lo
    eq = bits == lo
    umat = u_ref[...]

    def prefix(mask):
        mb = mask.astype(BF16)
        for j in range(nblk):
            inc_scr[:, j * LANES:(j + 1) * LANES] = jnp.dot(mb[:, j * LANES:(j + 1) * LANES], umat,
                                                            preferred_element_type=F32)
        totals = jnp.dot(mb, bs_ref[...], preferred_element_type=F32)
        offs = _mm_hi(totals, su_ref[...])
        return inc_scr[...] + _mm_hi(offs, bst_ref[...]), offs, totals

    eqf = eq.astype(F32)
    n_gt = jnp.sum(gt.astype(F32), axis=1, keepdims=True)
    eq_before, _, _ = prefix(eqf)
    sel = gt | (eq & ((eq_before - eqf) < (cap - n_gt)))
    self_ = sel.astype(F32)
    pos, offs, totals = prefix(self_)
    slot_ref[0] = jnp.where(sel, pos - 1.0, -1.0).astype(jnp.int32)
    start = jnp.floor(offs * (1.0 / ROUTE_ALIGN)) * ROUTE_ALIGN
    span = offs + totals - start
    n_sub = functools.reduce(jnp.add, [jnp.where(span > j * ROUTE_SUB, 1.0, 0.0)
                                       for j in range(ROUTE_WIN // ROUTE_SUB)])
    base_ref[0] = start.astype(jnp.int32)
    nsub_ref[0] = jnp.max(n_sub, axis=0, keepdims=True).astype(jnp.int32)


def _moe_select(aff_t, cap):
    b, e, t = aff_t.shape
    nblk = t // ROUTE_BLK
    u = jnp.asarray(np.triu(np.ones((LANES, LANES), np.float32)), BF16)
    blk = np.arange(t) // ROUTE_BLK
    bs = (blk[:, None] == np.arange(nblk)[None, :]).astype(np.float32)
    su = np.triu(np.ones((nblk, nblk), np.float32), 1)
    const = lambda bi: (0, 0)
    return pl.pallas_call(
        functools.partial(_select_kernel, cap=cap, nblk=nblk),
        grid=(b,),
        in_specs=[pl.BlockSpec((1, e, t), lambda bi: (bi, 0, 0)),
                  pl.BlockSpec((LANES, LANES), const),
                  pl.BlockSpec((t, nblk), const),
                  pl.BlockSpec((nblk, t), const),
                  pl.BlockSpec((nblk, nblk), const)],
        out_specs=(pl.BlockSpec((1, e, t), lambda bi: (bi, 0, 0)),
                   pl.BlockSpec((1, e, nblk), lambda bi: (bi, 0, 0)),
                   pl.BlockSpec((1, 1, nblk), lambda bi: (bi, 0, 0))),
        out_shape=(jax.ShapeDtypeStruct((b, e, t), jnp.int32), jax.ShapeDtypeStruct((b, e, nblk), jnp.int32),
                   jax.ShapeDtypeStruct((b, 1, nblk), jnp.int32)),
        scratch_shapes=[pltpu.VMEM((e, t), F32)],
        compiler_params=_cparams(("parallel",)),
        name="moe_select",
    )(aff_t, u, jnp.asarray(bs, BF16), jnp.asarray(bs.T), jnp.asarray(su))


def _gather_kernel(start_sm, nsub_sm, slot_ref, h_ref, xe_ref, *, nblk):
    bi = pl.program_id(0)
    xe_ref[0] = jnp.zeros(xe_ref.shape[1:], BF16)
    row = lax.broadcasted_iota(jnp.int32, (ROUTE_SUB, ROUTE_BLK), 0)

    def body(k, carry):
        t0 = pl.multiple_of(k * ROUTE_BLK, ROUTE_BLK)
        hblk = h_ref[0, pl.ds(t0, ROUTE_BLK), :]
        slots = slot_ref[0, :, pl.ds(t0, ROUTE_BLK)]
        firsts = [start_sm[(bi * N_EXPERTS + e) * nblk + k] for e in range(N_EXPERTS)]

        def sub(j, inner):
            starts = [pl.multiple_of(first + j * ROUTE_SUB, ROUTE_ALIGN) for first in firsts]
            onehot = jnp.concatenate(
                [jnp.where(row == slots[e:e + 1, :] - starts[e], 1.0, 0.0).astype(BF16)
                 for e in range(N_EXPERTS)], axis=0)
            rows = jnp.dot(onehot, hblk, preferred_element_type=F32).astype(BF16)
            for e in range(N_EXPERTS):
                win = pl.ds(starts[e], ROUTE_SUB)
                xe_ref[0, e, win, :] = xe_ref[0, e, win, :] + rows[e * ROUTE_SUB:(e + 1) * ROUTE_SUB]
            return inner

        lax.fori_loop(0, nsub_sm[bi * nblk + k], sub, 0)
        return carry

    lax.fori_loop(0, nblk, body, 0)


def _moe_gather(h2, slot, start_flat, nsub_flat, cap):
    b, t, d = h2.shape
    nblk = t // ROUTE_BLK
    cp = cap + ROUTE_WIN
    dq = _tile(d, 256)
    grid_spec = pltpu.PrefetchScalarGridSpec(
        num_scalar_prefetch=2,
        grid=(b, d // dq),
        in_specs=[pl.BlockSpec((1, N_EXPERTS, t), lambda bi, j, s0, s1: (bi, 0, 0)),
                  pl.BlockSpec((1, t, dq), lambda bi, j, s0, s1: (bi, 0, j))],
        out_specs=pl.BlockSpec((1, N_EXPERTS, cp, dq), lambda bi, j, s0, s1: (bi, 0, 0, j)),
    )
    return pl.pallas_call(
        functools.partial(_gather_kernel, nblk=nblk),
        grid_spec=grid_spec,
        out_shape=jax.ShapeDtypeStruct((b, N_EXPERTS, cp, d), BF16),
        compiler_params=_cparams(("parallel", "parallel")),
        name="moe_gather",
    )(start_flat, nsub_flat, slot, h2)


def _ffn_kernel(x_ref, wg_ref, wu_ref, wd_ref, y_ref, *, cap):
    x = x_ref[0, 0, 0:cap, :]
    a = jnp.dot(x, wg_ref[0], preferred_element_type=F32)
    u = jnp.dot(x, wu_ref[0], preferred_element_type=F32)
    y_ref[0, 0, 0:cap, :] = jnp.dot((_silu(a) * u).astype(BF16), wd_ref[0],
                                    preferred_element_type=F32).astype(BF16)
    y_ref[0, 0, cap:, :] = jnp.zeros((y_ref.shape[2] - cap, y_ref.shape[3]), BF16)


def _moe_ffn(xe, wg, wu, wd, cap):
    b, e, cp, d = xe.shape
    f = wg.shape[-1]
    return pl.pallas_call(
        functools.partial(_ffn_kernel, cap=cap),
        grid=(e, b),
        in_specs=[pl.BlockSpec((1, 1, cp, d), lambda ei, bi: (bi, ei, 0, 0)),
                  pl.BlockSpec((1, d, f), lambda ei, bi: (ei, 0, 0)),
                  pl.BlockSpec((1, d, f), lambda ei, bi: (ei, 0, 0)),
                  pl.BlockSpec((1, f, d), lambda ei, bi: (ei, 0, 0))],
        out_specs=pl.BlockSpec((1, 1, cp, d), lambda ei, bi: (bi, ei, 0, 0)),
        out_shape=jax.ShapeDtypeStruct((b, e, cp, d), BF16),
        compiler_params=_cparams(("parallel", "parallel")),
        name="moe_ffn",
    )(xe, wg, wu, wd)


def _combine_kernel(start_sm, nsub_sm, ye_ref, x_ref, gate_ref, slot_ref, aff_ref, o_ref, *, nblk, per_step):
    bi = pl.program_id(0)
    row = lax.broadcasted_iota(jnp.int32, (ROUTE_SUB, ROUTE_BLK), 0)
    for c in range(per_step):
        k = pl.program_id(2) * per_step + c
        tok = slice(c * ROUTE_BLK, (c + 1) * ROUTE_BLK)
        slots = slot_ref[0, :, tok]
        aff = aff_ref[0, :, tok]
        firsts = [start_sm[(bi * N_EXPERTS + e) * nblk + k] for e in range(N_EXPERTS)]

        def sub(j, acc, slots=slots, aff=aff, firsts=firsts):
            starts = [pl.multiple_of(first + j * ROUTE_SUB, ROUTE_ALIGN) for first in firsts]
            wsel = jnp.concatenate([jnp.where(row == slots[e:e + 1, :] - starts[e], aff[e:e + 1, :], 0.0)
                                    for e in range(N_EXPERTS)], axis=0)
            ys = jnp.concatenate([ye_ref[0, e, pl.ds(starts[e], ROUTE_SUB), :] for e in range(N_EXPERTS)],
                                 axis=0)
            hi, lo = _split2(wsel)
            tn = (((0,), (0,)), ((), ()))
            return (acc + lax.dot_general(hi, ys, tn, preferred_element_type=F32)
                    + lax.dot_general(lo, ys, tn, preferred_element_type=F32))

        acc = lax.fori_loop(0, nsub_sm[bi * nblk + k], sub, jnp.zeros((ROUTE_BLK, x_ref.shape[2]), F32))
        o_ref[0, tok, :] = x_ref[0, tok, :] + gate_ref[0] * acc


def _moe_combine(ye, x1, gate, slot, aff_t, start_flat, nsub_flat):
    b, t, d = x1.shape
    cp = ye.shape[2]
    nblk = t // ROUTE_BLK
    dh = d // 2
    per_b = gate.shape[0] > 1
    per_step = _tile(nblk, COMBINE_BLOCKS)
    tok = per_step * ROUTE_BLK
    grid_spec = pltpu.PrefetchScalarGridSpec(
        num_scalar_prefetch=2,
        grid=(b, 2, nblk // per_step),
        in_specs=[pl.BlockSpec((1, N_EXPERTS, cp, dh), lambda bi, j, k, s0, s1: (bi, 0, 0, j)),
                  pl.BlockSpec((1, tok, dh), lambda bi, j, k, s0, s1: (bi, k, j)),
                  pl.BlockSpec((1, 1, dh), (lambda bi, j, k, s0, s1: (bi, 0, j)) if per_b
                               else (lambda bi, j, k, s0, s1: (0, 0, j))),
                  pl.BlockSpec((1, N_EXPERTS, tok), lambda bi, j, k, s0, s1: (bi, 0, k)),
                  pl.BlockSpec((1, N_EXPERTS, tok), lambda bi, j, k, s0, s1: (bi, 0, k))],
        out_specs=pl.BlockSpec((1, tok, dh), lambda bi, j, k, s0, s1: (bi, k, j)),
    )
    return pl.pallas_call(
        functools.partial(_combine_kernel, nblk=nblk, per_step=per_step),
        grid_spec=grid_spec,
        out_shape=jax.ShapeDtypeStruct((b, t, d), F32),
        compiler_params=_cparams(("parallel", "parallel", "arbitrary")),
        name="moe_combine",
    )(start_flat, nsub_flat, ye, x1, gate, slot, aff_t)


def _expert_choice_ffn(x1, h2, aff_t, gate, wg, wu, wd):
    b, t, _ = x1.shape
    cap = CAPACITY_FACTOR * t // N_EXPERTS
    slot, start, nsub = _moe_select(aff_t, cap)
    start_flat, nsub_flat = start.reshape(-1), nsub.reshape(-1)
    xe = _moe_gather(h2, slot, start_flat, nsub_flat, cap)
    ye = _moe_ffn(xe, wg, wu, wd, cap)
    return _moe_combine(ye, x1, gate, slot, aff_t, start_flat, nsub_flat)


def _rope_tables(n_tokens):
    rows = n_tokens // GRID_W
    row = jnp.repeat(jnp.arange(rows, dtype=F32), GRID_W)
    col = jnp.tile(jnp.arange(GRID_W, dtype=F32), rows)
    n_freq = HEAD_DIM // 4
    inv_freq = ROPE_THETA ** (-jnp.arange(n_freq, dtype=F32) / n_freq)
    ang_r = row[:, None] * inv_freq
    ang_c = col[:, None] * inv_freq
    cos = jnp.concatenate([jnp.cos(ang_r)] * 2 + [jnp.cos(ang_c)] * 2, axis=-1)
    sin = jnp.concatenate([jnp.sin(ang_r)] * 2 + [jnp.sin(ang_c)] * 2, axis=-1)
    return jnp.tile(cos, (1, LANES // HEAD_DIM)), jnp.tile(sin, (1, LANES // HEAD_DIM))


def _block_diag(width, block, value):
    idx = np.arange(width) // block
    return jnp.asarray((idx[:, None] == idx[None, :]).astype(np.float32) * value)


def kernel(x, c, ctx, c_ctx, w_mod, b_mod, w_in, w_out, attn_q_norm, attn_k_norm, dn_conv, dn_a_log,
           dn_dt_bias, dn_norm, hg_lower_bounds, hg_norm, moe_router, moe_w_gate, moe_w_up, moe_w_down):
    depth = w_mod.shape[0]
    b, t_lat, d = x.shape
    cos, sin = _rope_tables(t_lat)
    g_head = _block_diag(LANES, HEAD_DIM, 1.0 / HEAD_DIM)
    g_mean = _block_diag(REC_W, HEAD_DIM, 1.0 / HEAD_DIM)
    g_sum = _block_diag(REC_W, HEAD_DIM, 1.0)
    consts = _scan_constants()
    s_zero = jnp.zeros((b, 2, REC_W, REC_W), F32)

    lb_w = jax.nn.softmax(hg_lower_bounds.astype(F32), axis=0)
    hg_lb = jnp.cumsum(lb_w, axis=0) - lb_w[0]

    rows = ((b + 1 + 7) // 8) * 8
    cond = jnp.zeros((rows, d), F32).at[:b].set(c).at[b].set(c_ctx)
    mod = _modulation(cond, w_mod, b_mod)

    n_small = 4 * DN_HEADS
    w_in_r = jnp.concatenate(
        [w_in[:, :, :C_HG], w_in[:, :, C_HG + n_small:], w_in[:, :, C_HG:C_HG + n_small],
         jnp.zeros((depth, d, C_END - C_BA - n_small), w_in.dtype)], axis=-1).astype(BF16)
    w_out_b = _cast_bf16(w_out)
    wg_b, wu_b, wd_b = _cast_bf16(moe_w_gate), _cast_bf16(moe_w_up), _cast_bf16(moe_w_down)

    x_lat, x_ctx = x, ctx
    for l in range(depth):
        ctx_out = l < depth - 1
        m_lat = [mod[l, :b, j * d:(j + 1) * d][:, None, :] for j in range(6)]
        m_ctx = [mod[l, b:b + 1, j * d:(j + 1) * d][:, None, :] for j in range(6)]
        qn = jnp.tile(attn_q_norm[l], LANES // HEAD_DIM)[None, :]
        kn = jnp.tile(attn_k_norm[l], LANES // HEAD_DIM)[None, :]
        conv_w = jnp.zeros((8, 3 * REC_W), F32).at[:CONV_K].set(dn_conv[l])
        pad = jnp.zeros((LANES - 4 * DN_HEADS,), F32)
        neg_a = jnp.concatenate([jnp.zeros((2 * DN_HEADS,), F32), -jnp.exp(dn_a_log[l].reshape(-1)), pad])[None, :]
        dt_b = jnp.concatenate([jnp.zeros((2 * DN_HEADS,), F32), dn_dt_bias[l].reshape(-1), pad])[None, :]
        dnw = jnp.tile(dn_norm[l], DN_HEADS)[None, :]
        hgw = jnp.tile(hg_norm[l], HG_HEADS)[None, :]
        router_t = moe_router[l].T

        streams = {}
        for name, xs, ms, rotate in (("ctx", x_ctx, m_ctx, False), ("lat", x_lat, m_lat, True)):
            t = xs.shape[1]
            q, kt, v, dn, hg, ba = _input_projection(xs, ms[0], ms[1], w_in_r[l], qn, kn,
                                                     cos[:t], sin[:t], g_head, rotate)
            qkv, gb = _dn_prep(dn, ba, conv_w, neg_a, dt_b, g_sum)
            streams[name] = dict(q=q, kv=(kt, v), dn=dn, hg=hg, qkv=qkv, gb=gb)

        sc, sl = streams["ctx"], streams["lat"]
        dcf, dcb, dn_state = _dn_scan(sc["qkv"], sc["gb"], s_zero, consts)
        dlf, dlb, _ = _dn_scan(sl["qkv"], sl["gb"], dn_state, consts)
        gcf, gcb, hg_state = _hg_scan(sc["hg"], hg_lb[l], s_zero, consts)
        glf, glb, _ = _hg_scan(sl["hg"], hg_lb[l], hg_state, consts)

        a_lat = _attention(sl["q"], [sc["kv"], sl["kv"]], t_lat)
        x1, h2, aff_t = _output_projection(
            a_lat, dlf, dlb, sl["dn"], glf, glb, sl["hg"], dnw, hgw, g_mean, w_out_b[l], x_lat,
            m_lat[2], m_lat[3], m_lat[4], router_t)
        x_lat = _expert_choice_ffn(x1, h2, aff_t, m_lat[5], wg_b[l], wu_b[l], wd_b[l])
        if ctx_out:
            a_ctx = _attention(sc["q"], [sc["kv"]], x_ctx.shape[1])
            x1, h2, aff_t = _output_projection(
                a_ctx, dcf, dcb, sc["dn"], gcf, gcb, sc["hg"], dnw, hgw, g_mean, w_out_b[l], x_ctx,
                m_ctx[2], m_ctx[3], m_ctx[4], router_t)
            x_ctx = _expert_choice_ffn(x1, h2, aff_t, m_ctx[5], wg_b[l], wu_b[l], wd_b[l])
    return x_lat
```

```python
import functools

import numpy as np
import jax
import jax.numpy as jnp
from jax import lax
from jax.experimental import pallas as pl
from jax.experimental.pallas import tpu as pltpu

F32 = jnp.float32
BF16 = jnp.bfloat16
HIGHEST = lax.Precision.HIGHEST

HEAD_DIM = 64
N_Q_HEADS = 8
N_KV_HEADS = 2
GQA_GROUP = N_Q_HEADS // N_KV_HEADS
DN_HEADS = 4
HG_HEADS = 4
GRID_W = 64
ROPE_THETA = 10000.0
CONV_K = 5
CHUNK = 64
N_EXPERTS = 16
CAPACITY_FACTOR = 2
EPS = 1e-6
LOG2E = 1.4426950408889634
ATTN_KV_GROUP = 1408
V_ROWS = HEAD_DIM + 16

ATTN_Q_W = N_Q_HEADS * HEAD_DIM
ATTN_KV_W = N_KV_HEADS * HEAD_DIM
REC_W = DN_HEADS * HEAD_DIM
LANES = 128
ROUTE_BLK = LANES
ROUTE_ALIGN = 16
ROUTE_WIN = ROUTE_BLK + ROUTE_ALIGN
COMBINE_BLOCKS = 4
ROUTE_SUB = ROUTE_WIN // 3
N_LEVELS = 6
SCAN_BATCH = 4
VMEM_LIMIT = 56 * 1024 * 1024

C_Q, C_K, C_V, C_DN, C_HG, C_BA, C_END = 0, 512, 640, 768, 1792, 3072, 3200


def _mm(a, b):
    return jnp.dot(a.astype(BF16), b.astype(BF16), preferred_element_type=F32)


def _mm_nt(a, b):
    return lax.dot_general(a.astype(BF16), b.astype(BF16), (((1,), (1,)), ((), ())),
                           preferred_element_type=F32)


def _mm_tn(a, b):
    return lax.dot_general(a.astype(BF16), b.astype(BF16), (((0,), (0,)), ((), ())),
                           preferred_element_type=F32)


def _mm_hi(a, b):
    return jnp.dot(a, b, precision=HIGHEST, preferred_element_type=F32)


def _mm_group(a, gmat):
    hi = a.astype(BF16)
    lo = (a - hi.astype(F32)).astype(BF16)
    t = jnp.dot(jnp.concatenate([hi, lo], axis=0), gmat.astype(BF16), preferred_element_type=F32)
    return t[:a.shape[0]] + t[a.shape[0]:]


def _sigmoid(x):
    return 1.0 / (1.0 + jnp.exp(-x))


def _silu(x):
    return x * _sigmoid(x)


def _cparams(sem):
    return pltpu.CompilerParams(dimension_semantics=sem, vmem_limit_bytes=VMEM_LIMIT)


def _tile(n, pref):
    return pref if n % pref == 0 else n


def _cast_kernel(x_ref, o_ref):
    o_ref[...] = x_ref[...].astype(BF16)


def _cast_bf16(w):
    r, c = w.shape[-2:]
    w3 = w.reshape(-1, r, c)
    rb = _tile(r, 512)
    out = pl.pallas_call(
        _cast_kernel,
        grid=(w3.shape[0], r // rb),
        in_specs=[pl.BlockSpec((1, rb, c), lambda i, j: (i, j, 0))],
        out_specs=pl.BlockSpec((1, rb, c), lambda i, j: (i, j, 0)),
        out_shape=jax.ShapeDtypeStruct(w3.shape, BF16),
        compiler_params=_cparams(("parallel", "parallel")),
        name="cast_bf16",
    )(w3)
    return out.reshape(w.shape)


def _mod_kernel(c_ref, w_ref, b_ref, o_ref):
    o_ref[0] = _mm_hi(_silu(c_ref[...]), w_ref[0]) + b_ref[0]


def _modulation(cond, w_mod, b_mod):
    depth, d, n = w_mod.shape
    rows = cond.shape[0]
    tn = _tile(n, 1536)
    return pl.pallas_call(
        _mod_kernel,
        grid=(depth, n // tn),
        in_specs=[pl.BlockSpec((rows, d), lambda l, j: (0, 0)),
                  pl.BlockSpec((1, d, tn), lambda l, j: (l, 0, j)),
                  pl.BlockSpec((1, 1, tn), lambda l, j: (l, 0, j))],
        out_specs=pl.BlockSpec((1, rows, tn), lambda l, j: (l, 0, j)),
        out_shape=jax.ShapeDtypeStruct((depth, rows, n), F32),
        compiler_params=_cparams(("parallel", "parallel")),
        name="modulation",
    )(cond, w_mod, b_mod.reshape(depth, 1, n))


def _inproj_kernel(x_ref, sh_ref, sc_ref, w_ref, qn_ref, kn_ref, cos_ref, sin_ref, g_ref,
                   q_ref, k_ref, v_ref, dn_ref, hg_ref, ba_ref, *, rotate):
    x = x_ref[0]
    ms = jnp.mean(x * x, axis=-1, keepdims=True)
    h = x * lax.rsqrt(ms + EPS) * (1.0 + sc_ref[0]) + sh_ref[0]
    hb = h.astype(BF16)
    gmat = g_ref[...]

    def proj(lo, hi):
        return jnp.dot(hb, w_ref[:, lo:hi], preferred_element_type=F32)

    def head_norm_rope(t, nw):
        t = t * lax.rsqrt(_mm_group(t * t, gmat) + EPS) * nw
        if rotate:
            lane = lax.broadcasted_iota(jnp.int32, t.shape, 1)
            first = (lane % 32) < 16
            rot = jnp.where(first, -pltpu.roll(t, LANES - 16, 1), pltpu.roll(t, 16, 1))
            t = t * cos_ref[...] + rot * sin_ref[...]
        return t

    def q_group(j):
        t = proj(C_Q + j * LANES, C_Q + (j + 1) * LANES)
        yield
        qj = head_norm_rope(t, qn_ref[...])
        qt = (qj * (HEAD_DIM ** -0.5 * LOG2E)).T.astype(BF16)
        q_ref[0, 2 * j] = qt[:HEAD_DIM]
        q_ref[0, 2 * j + 1] = qt[HEAD_DIM:]

    def k_group():
        t = proj(C_K, C_V)
        yield
        k = head_norm_rope(t, kn_ref[...]).astype(BF16)
        k_ref[0, 0] = k[:, :HEAD_DIM]
        k_ref[0, 1] = k[:, HEAD_DIM:]

    def v_group():
        v = proj(C_V, C_DN)
        yield
        vt = v.T
        ones = jnp.ones((V_ROWS - HEAD_DIM, vt.shape[1]), F32)
        v_ref[0, 0] = jnp.concatenate([vt[:HEAD_DIM], ones], axis=0).astype(BF16)
        v_ref[0, 1] = jnp.concatenate([vt[HEAD_DIM:], ones], axis=0).astype(BF16)

    def plain_group(o_ref, base, lo, hi):
        t = proj(base + lo, base + hi)
        yield
        o_ref[0, :, lo:hi] = t

    step = 4 * LANES
    plain = [plain_group(o_ref, base, lo, min(lo + step, width))
             for o_ref, base, width in ((dn_ref, C_DN, C_HG - C_DN), (hg_ref, C_HG, C_BA - C_HG),
                                        (ba_ref, C_BA, C_END - C_BA))
             for lo in range(0, width, step)]
    _lockstep([q_group(j) for j in range(ATTN_Q_W // LANES)] + [k_group(), v_group()] + plain, stagger=True)


def _input_projection(x, shift, scale, w, qn, kn, cos, sin, gmat, rotate):
    b, t, d = x.shape
    tm = _tile(t, 512)
    per_b = shift.shape[0] > 1
    mod_map = (lambda bi, i: (bi, 0, 0)) if per_b else (lambda bi, i: (0, 0, 0))
    const = lambda bi, i: (0, 0)
    out_shape = (
        jax.ShapeDtypeStruct((b, N_Q_HEADS, HEAD_DIM, t), BF16),
        jax.ShapeDtypeStruct((b, N_KV_HEADS, t, HEAD_DIM), BF16),
        jax.ShapeDtypeStruct((b, N_KV_HEADS, V_ROWS, t), BF16),
        jax.ShapeDtypeStruct((b, t, C_HG - C_DN), F32),
        jax.ShapeDtypeStruct((b, t, C_BA - C_HG), F32),
        jax.ShapeDtypeStruct((b, t, C_END - C_BA), F32),
    )
    return pl.pallas_call(
        functools.partial(_inproj_kernel, rotate=rotate),
        grid=(b, t // tm),
        in_specs=[pl.BlockSpec((1, tm, d), lambda bi, i: (bi, i, 0)),
                  pl.BlockSpec((1, 1, d), mod_map),
                  pl.BlockSpec((1, 1, d), mod_map),
                  pl.BlockSpec((d, C_END), const),
                  pl.BlockSpec((1, LANES), const),
                  pl.BlockSpec((1, LANES), const),
                  pl.BlockSpec((tm, LANES), lambda bi, i: (i, 0)),
                  pl.BlockSpec((tm, LANES), lambda bi, i: (i, 0)),
                  pl.BlockSpec((LANES, LANES), const)],
        out_specs=(pl.BlockSpec((1, N_Q_HEADS, HEAD_DIM, tm), lambda bi, i: (bi, 0, 0, i)),
                   pl.BlockSpec((1, N_KV_HEADS, tm, HEAD_DIM), lambda bi, i: (bi, 0, i, 0)),
                   pl.BlockSpec((1, N_KV_HEADS, V_ROWS, tm), lambda bi, i: (bi, 0, 0, i)),
                   pl.BlockSpec((1, tm, C_HG - C_DN), lambda bi, i: (bi, i, 0)),
                   pl.BlockSpec((1, tm, C_BA - C_HG), lambda bi, i: (bi, i, 0)),
                   pl.BlockSpec((1, tm, C_END - C_BA), lambda bi, i: (bi, i, 0))),
        out_shape=out_shape,
        compiler_params=_cparams(("parallel", "parallel")),
        name="input_projection",
    )(x, shift, scale, w, qn, kn, cos, sin, gmat)


def _lockstep(gens, stagger=False):
    results = [None] * len(gens)
    live, started = [], 0
    while live or started < len(gens):
        fresh = 1 if stagger else len(gens)
        live += list(range(started, min(started + fresh, len(gens))))
        started = min(started + fresh, len(gens))
        for i in reversed(list(live)):
            try:
                next(gens[i])
            except StopIteration as stop:
                results[i] = stop.value
                live.remove(i)
    return results


def _attn_part(qt, ks, vts):
    ss = [jnp.dot(k, qt, preferred_element_type=F32).astype(BF16) for k in ks]
    yield
    m = functools.reduce(jnp.maximum, [jnp.max(s, axis=0, keepdims=True) for s in ss])
    ps = [jnp.exp2(s - m) for s in ss]
    yield
    o = functools.reduce(jnp.add, [jnp.dot(vt, p, preferred_element_type=F32) for p, vt in zip(ps, vts)])
    return m.astype(F32), o


def _attn_kernel(*refs, n_src):
    q_ref, o_ref = refs[0], refs[-1]
    groups, cur, room = [], ([], []), ATTN_KV_GROUP
    for i in range(n_src):
        k_ref, vt_ref = refs[1 + 2 * i], refs[2 + 2 * i]
        lo, tk = 0, k_ref.shape[2]
        while lo < tk:
            n = min(room, tk - lo)
            cur[0].append(k_ref[0, 0, lo:lo + n, :])
            cur[1].append(vt_ref[0, 0, :, lo:lo + n])
            lo, room = lo + n, room - n
            if room == 0:
                groups.append(cur)
                cur, room = ([], []), ATTN_KV_GROUP
    if cur[0]:
        groups.append(cur)
    chains = [(h, g) for h in range(GQA_GROUP) for g in range(len(groups))]
    parts = _lockstep([_attn_part(q_ref[0, h], *groups[g]) for h, g in chains], stagger=True)
    outs = []
    for h in range(GQA_GROUP):
        mine = [parts[i] for i, (hh, _) in enumerate(chains) if hh == h]
        m = functools.reduce(jnp.maximum, [pm for pm, _ in mine])
        o = functools.reduce(jnp.add, [po * jnp.exp2(pm - m) for pm, po in mine])
        outs.append(o[:HEAD_DIM] / o[HEAD_DIM:HEAD_DIM + 1])
    o_ref[0] = jnp.concatenate(outs, axis=0).T.astype(BF16)


def _attention(qt, sources, t):
    b = qt.shape[0]
    tq = _tile(t, 256)
    gw = GQA_GROUP * HEAD_DIM
    in_specs = [pl.BlockSpec((1, GQA_GROUP, HEAD_DIM, tq), lambda bi, g, i: (bi, g, 0, i))]
    args = [qt]
    for k, vt in sources:
        tk = k.shape[2]
        in_specs.append(pl.BlockSpec((1, 1, tk, HEAD_DIM), lambda bi, g, i: (bi, g, 0, 0)))
        in_specs.append(pl.BlockSpec((1, 1, V_ROWS, tk), lambda bi, g, i: (bi, g, 0, 0)))
        args += [k, vt]
    return pl.pallas_call(
        functools.partial(_attn_kernel, n_src=len(sources)),
        grid=(b, N_KV_HEADS, t // tq),
        in_specs=in_specs,
        out_specs=pl.BlockSpec((1, tq, gw), lambda bi, g, i: (bi, i, g)),
        out_shape=jax.ShapeDtypeStruct((b, t, ATTN_Q_W), BF16),
        compiler_params=_cparams(("parallel", "parallel", "parallel")),
        name="attention",
    )(*args)


def _dn_prep_kernel(x_ref, xp_ref, xn_ref, ba_ref, cw_ref, na_ref, dtb_ref, g_ref, o_ref, gb_ref, buf):
    i = pl.program_id(1)
    n = pl.num_programs(1)
    tm = x_ref.shape[1]
    buf[0:8] = jnp.where(i > 0, xp_ref[0], 0.0)
    buf[8:8 + tm] = x_ref[0]
    buf[8 + tm:16 + tm] = jnp.where(i < n - 1, xn_ref[0], 0.0)
    half = CONV_K // 2
    y = cw_ref[0:1, :] * buf[8 - half:8 - half + tm]
    for j in range(1, CONV_K):
        y = y + cw_ref[j:j + 1, :] * buf[8 - half + j:8 - half + j + tm]
    y = _silu(y)
    gmat = g_ref[...]
    q = y[:, :REC_W]
    k = y[:, REC_W:2 * REC_W]
    o_ref[0, :, 0:REC_W] = q * lax.rsqrt(_mm_group(q * q, gmat) + EPS) * HEAD_DIM ** -0.5
    o_ref[0, :, REC_W:2 * REC_W] = k * lax.rsqrt(_mm_group(k * k, gmat) + EPS)
    o_ref[0, :, 2 * REC_W:] = y[:, 2 * REC_W:]
    ba = ba_ref[0]
    z = ba + dtb_ref[...]
    softplus = jnp.maximum(z, 0.0) + jnp.log1p(jnp.exp(-jnp.abs(z)))
    lane = lax.broadcasted_iota(jnp.int32, ba.shape, 1)
    gb_ref[0] = jnp.where(lane < 2 * DN_HEADS, _sigmoid(ba), na_ref[...] * softplus)


def _dn_prep(dn, ba, conv_w, neg_a, dt_bias, gsum):
    b, t, _ = dn.shape
    tm = _tile(t, 512)
    w3 = 3 * REC_W
    nb8 = t // 8
    return pl.pallas_call(
        _dn_prep_kernel,
        grid=(b, t // tm),
        in_specs=[pl.BlockSpec((1, tm, w3), lambda bi, i: (bi, i, 0)),
                  pl.BlockSpec((1, 8, w3), lambda bi, i: (bi, jnp.maximum(i * (tm // 8) - 1, 0), 0)),
                  pl.BlockSpec((1, 8, w3), lambda bi, i: (bi, jnp.minimum((i + 1) * (tm // 8), nb8 - 1), 0)),
                  pl.BlockSpec((1, tm, LANES), lambda bi, i: (bi, i, 0)),
                  pl.BlockSpec((8, w3), lambda bi, i: (0, 0)),
                  pl.BlockSpec((1, LANES), lambda bi, i: (0, 0)),
                  pl.BlockSpec((1, LANES), lambda bi, i: (0, 0)),
                  pl.BlockSpec((REC_W, REC_W), lambda bi, i: (0, 0))],
        out_specs=(pl.BlockSpec((1, tm, w3), lambda bi, i: (bi, i, 0)),
                   pl.BlockSpec((1, tm, LANES), lambda bi, i: (bi, i, 0))),
        out_shape=(jax.ShapeDtypeStruct((b, t, w3), F32), jax.ShapeDtypeStruct((b, t, LANES), F32)),
        scratch_shapes=[pltpu.VMEM((tm + 16, w3), F32)],
        compiler_params=_cparams(("parallel", "parallel")),
        name="deltanet_prep",
    )(dn, dn, dn, ba, conv_w, neg_a, dt_bias, gsum)


def _split2(a):
    hi = a.astype(BF16)
    return hi, (a - hi.astype(F32)).astype(BF16)


def _split3(a):
    hi = a.astype(BF16)
    r = a - hi.astype(F32)
    mid = r.astype(BF16)
    return hi, mid, (r - mid.astype(F32)).astype(BF16)


def _bdiag(a, hm):
    return jnp.concatenate([a] * DN_HEADS, axis=0) * hm


def _mm_bd(a, b, hm):
    return jnp.dot(a.astype(BF16), _bdiag(b.astype(BF16), hm), preferred_element_type=F32)


def _mm_sel_l(sel, b, terms=3):
    n = b.shape[1]
    parts = _split3(b) if terms == 3 else _split2(b)
    t = jnp.dot(sel.astype(BF16), jnp.concatenate(parts, axis=1), preferred_element_type=F32)
    return functools.reduce(jnp.add, [t[:, i * n:(i + 1) * n] for i in range(terms)])


def _mm_sel_r(a, sel):
    m = a.shape[0]
    t = jnp.dot(jnp.concatenate(_split3(a), axis=0), sel.astype(BF16), preferred_element_type=F32)
    return t[:m] + t[m:2 * m] + t[2 * m:]


def _dn_chunk(x, gb, st, ex, lmat, causal, lvl_ref, d, hm):
    n = REC_W
    lvl = lambda j: lvl_ref[d, j * CHUNK:(j + 1) * CHUNK, :]
    eye = lvl(N_LEVELS)
    ones8 = jnp.ones((8, CHUNK), F32)
    q, k, v = x[:, :n], x[:, n:2 * n], x[:, 2 * n:]
    gbx = _mm_sel_r(gb, ex)
    yield
    beta, g = gbx[:, :n], gbx[:, n:]
    cum = _mm_sel_l(lmat, g)
    last = 0 if d else CHUNK - 1
    clast = cum[last:last + 1]
    kb = k * beta
    ks = _bdiag(k.astype(BF16), hm)
    kk = lax.dot_general(kb.astype(BF16), ks, (((1,), (1,)), ((), ())), preferred_element_type=F32)
    qk = lax.dot_general(q.astype(BF16), ks, (((1,), (1,)), ((), ())), preferred_element_type=F32)
    yield
    cum_s = _mm_sel_l(ones8, cum * eye)[0:1]
    ecum = jnp.exp(cum)
    kdec = k * jnp.exp(clast - cum)
    yield
    decay = jnp.where(causal, jnp.exp(jnp.where(causal, cum - cum_s, 0.0)), 0.0)
    m = kk * decay
    attn = qk * decay
    inv = eye - lvl(N_LEVELS - 1) * m
    for lev in range(N_LEVELS - 2, -1, -1):
        half = _mm_bd(inv, lvl(lev) * m, hm)
        yield
        inv = inv - _mm_bd(half, inv, hm)
        yield
    rhs = v * beta - _mm_nt(kb * ecum, st)
    qs = _mm_nt(q * ecum, st)
    yield
    v_new = _mm_bd(inv, rhs, hm)
    yield
    out = qs + _mm_bd(attn, v_new, hm)
    st_new = st * jnp.exp(clast) + _mm_tn(v_new, kdec) * hm.astype(F32)
    return out, st_new


def _hg_chunk(qr, fr, v, lb, st, amat, lvl_ref, d, hm):
    lvl = lambda j: lvl_ref[d, j * CHUNK:(j + 1) * CHUNK, :]
    last = 0 if d else CHUNK - 1
    q = _silu(qr)
    f = lb + (1.0 - lb) * _sigmoid(fr)
    k = 1.0 - f
    g = jnp.log(f)
    ex = jnp.exp(_mm_sel_l(amat, g, terms=2))
    yield

    def level(xl, mask):
        kl = _bdiag((k * xl).astype(BF16), hm)
        return lax.dot_general((q * xl).astype(BF16), kl, (((1,), (1,)), ((), ())),
                               preferred_element_type=F32) * mask

    attn = level(1.0, lvl(N_LEVELS))
    for lev in range(N_LEVELS):
        attn = attn + level(ex[(2 + lev) * CHUNK:(3 + lev) * CHUNK], lvl(lev))
    qs = _mm_nt(q * ex[0:CHUNK], st)
    vk = _mm_tn(v, k * ex[CHUNK:2 * CHUNK])
    yield
    out = _mm_bd(attn, v, hm) + qs
    st_new = st * ex[last:last + 1] + vk * hm.astype(F32)
    return out, st_new


def _dn_scan_kernel(xf_ref, xb_ref, gf_ref, gb_ref, s0_ref, ex_ref, l_ref, cz_ref, lvl_ref, hm_ref,
                    of_ref, ob_ref, sf_ref, s_scr, *, nchunk):
    @pl.when(pl.program_id(1) == 0)
    def _():
        s_scr[...] = s0_ref[...]

    hm = hm_ref[...]
    refs = ((xf_ref, gf_ref, of_ref), (xb_ref, gb_ref, ob_ref))

    def body(c, carry):
        rows = [pl.ds(pl.multiple_of(cc * CHUNK, CHUNK), CHUNK) for cc in (c, nchunk - 1 - c)]
        insts = [(bi, d) for bi in range(s_scr.shape[0]) for d in range(2)]
        res = _lockstep([_dn_chunk(refs[d][0][bi, rows[d], :], refs[d][1][bi, rows[d], :], s_scr[bi, d],
                                   ex_ref[d], l_ref[d], cz_ref[d] > 0, lvl_ref, d, hm) for bi, d in insts])
        for (bi, d), (out, s_new) in zip(insts, res):
            refs[d][2][bi, rows[d], :] = out
            s_scr[bi, d] = s_new
        return carry

    lax.fori_loop(0, nchunk, body, 0)
    sf_ref[...] = s_scr[...]


def _hg_scan_kernel(qf_ref, qb_ref, ff_ref, fb_ref, if_ref, ib_ref, lb_ref, s0_ref, a_ref, lvl_ref, hm_ref,
                    of_ref, ob_ref, sf_ref, s_scr, *, nchunk):
    @pl.when(pl.program_id(1) == 0)
    def _():
        s_scr[...] = s0_ref[...]

    hm = hm_ref[...]
    refs = ((qf_ref, ff_ref, if_ref, of_ref), (qb_ref, fb_ref, ib_ref, ob_ref))

    def body(c, carry):
        rows = [pl.ds(pl.multiple_of(cc * CHUNK, CHUNK), CHUNK) for cc in (c, nchunk - 1 - c)]
        insts = [(bi, d) for bi in range(s_scr.shape[0]) for d in range(2)]
        res = _lockstep([_hg_chunk(refs[d][0][bi, rows[d], :], refs[d][1][bi, rows[d], :],
                                   refs[d][2][bi, rows[d], :], lb_ref[d:d + 1, :], s_scr[bi, d], a_ref[d],
                                   lvl_ref, d, hm) for bi, d in insts])
        for (bi, d), (out, s_new) in zip(insts, res):
            refs[d][3][bi, rows[d], :] = out
            s_scr[bi, d] = s_new
        return carry

    lax.fori_loop(0, nchunk, body, 0)
    sf_ref[...] = s_scr[...]


def _dn_scan(qkv, gb, s0, consts):
    b, t, w3 = qkv.shape
    tb = _tile(t, 512)
    nblk = t // tb
    n = REC_W
    nb = _tile(b, SCAN_BATCH)
    fwd = lambda bi, i: (bi, i, 0)
    bwd = lambda bi, i: (bi, nblk - 1 - i, 0)
    c3 = lambda bi, i: (0, 0, 0)
    state = pl.BlockSpec((nb, 2, n, n), lambda bi, i: (bi, 0, 0, 0))
    return pl.pallas_call(
        functools.partial(_dn_scan_kernel, nchunk=tb // CHUNK),
        grid=(b // nb, nblk),
        in_specs=[pl.BlockSpec((nb, tb, w3), fwd), pl.BlockSpec((nb, tb, w3), bwd),
                  pl.BlockSpec((nb, tb, LANES), fwd), pl.BlockSpec((nb, tb, LANES), bwd),
                  state,
                  pl.BlockSpec((2, LANES, 2 * n), c3),
                  pl.BlockSpec((2, CHUNK, CHUNK), c3),
                  pl.BlockSpec((2, CHUNK, n), c3),
                  pl.BlockSpec((2, (N_LEVELS + 1) * CHUNK, n), c3),
                  pl.BlockSpec((n, n), lambda bi, i: (0, 0))],
        out_specs=(pl.BlockSpec((nb, tb, n), fwd), pl.BlockSpec((nb, tb, n), bwd), state),
        out_shape=(jax.ShapeDtypeStruct((b, t, n), F32), jax.ShapeDtypeStruct((b, t, n), F32),
                   jax.ShapeDtypeStruct((b, 2, n, n), F32)),
        scratch_shapes=[pltpu.VMEM((nb, 2, n, n), F32)],
        compiler_params=_cparams(("parallel", "arbitrary")),
        name="deltanet_scan",
    )(qkv, qkv, gb, gb, s0, consts["ex"], consts["l"], consts["cz"], consts["lvl"], consts["hm"])


def _hg_scan(hg, lb, s0, consts):
    b, t, _ = hg.shape
    tb = _tile(t, 512)
    nblk = t // tb
    n = REC_W
    nb = _tile(b, SCAN_BATCH)
    fwd = lambda col: (lambda bi, i: (bi, i, col))
    bwd = lambda col: (lambda bi, i: (bi, nblk - 1 - i, col))
    c3 = lambda bi, i: (0, 0, 0)
    blk = lambda m: pl.BlockSpec((nb, tb, n), m)
    state = pl.BlockSpec((nb, 2, n, n), lambda bi, i: (bi, 0, 0, 0))
    return pl.pallas_call(
        functools.partial(_hg_scan_kernel, nchunk=tb // CHUNK),
        grid=(b // nb, nblk),
        in_specs=[blk(fwd(0)), blk(bwd(0)), blk(fwd(1)), blk(bwd(2)), blk(fwd(3)), blk(bwd(3)),
                  pl.BlockSpec((2, n), lambda bi, i: (0, 0)),
                  state,
                  pl.BlockSpec((2, (2 + N_LEVELS) * CHUNK, CHUNK), c3),
                  pl.BlockSpec((2, (N_LEVELS + 1) * CHUNK, n), c3),
                  pl.BlockSpec((n, n), lambda bi, i: (0, 0))],
        out_specs=(blk(fwd(0)), blk(bwd(0)), state),
        out_shape=(jax.ShapeDtypeStruct((b, t, n), F32), jax.ShapeDtypeStruct((b, t, n), F32),
                   jax.ShapeDtypeStruct((b, 2, n, n), F32)),
        scratch_shapes=[pltpu.VMEM((nb, 2, n, n), F32)],
        compiler_params=_cparams(("parallel", "arbitrary")),
        name="hgrn2_scan",
    )(hg, hg, hg, hg, hg, hg, lb, s0, consts["a"], consts["lvl"], consts["hm"])


def _scan_constants_dir(rev):
    t = np.arange(CHUNK)
    p = (CHUNK - 1 - t) if rev else t
    pt, pu = p[:, None], p[None, :]
    causal = (pu <= pt).astype(np.float32)
    head = np.arange(REC_W) // CHUNK
    hm = (head[:, None] == head[None, :]).astype(np.float32)
    rows = [causal, (pu > pt).astype(np.float32)]
    masks = []
    for lev in range(N_LEVELS):
        blk = CHUNK >> (lev + 1)
        bound = (pt // (2 * blk)) * (2 * blk) + blk - 1
        right = (pt % (2 * blk)) >= blk
        rows.append(np.where(right, (pu > bound) & (pu <= pt), (pu > pt) & (pu <= bound)).astype(np.float32))
        masks.append(((pt // (2 * blk) == pu // (2 * blk)) & right & ((pu % (2 * blk)) < blk)).astype(np.float32))
    masks.append(np.eye(CHUNK, dtype=np.float32))
    d = 1 if rev else 0
    ex = np.zeros((LANES, 2 * REC_W), np.float32)
    for h in range(DN_HEADS):
        ex[d * DN_HEADS + h, h * CHUNK:(h + 1) * CHUNK] = 1.0
        ex[2 * DN_HEADS + d * DN_HEADS + h, REC_W + h * CHUNK:REC_W + (h + 1) * CHUNK] = 1.0
    return {
        "l": causal, "cz": np.tile(causal, (1, DN_HEADS)), "hm": hm, "ex": ex,
        "a": np.concatenate(rows, axis=0),
        "lvl": np.concatenate([np.tile(mk, (1, DN_HEADS)) for mk in masks], axis=0),
    }


def _scan_constants():
    fwd, bwd = _scan_constants_dir(False), _scan_constants_dir(True)
    out = {key: jnp.asarray(np.stack([fwd[key], bwd[key]])) for key in ("l", "cz", "ex", "a", "lvl")}
    out["hm"] = jnp.asarray(fwd["hm"], BF16)
    return out


def _outproj_kernel(a_ref, dof_ref, dob_ref, z_ref, gof_ref, gob_ref, gg_ref, dnw_ref, hgw_ref, g_ref,
                    w_ref, x_ref, gate_ref, sh_ref, sc_ref, rt_ref, x1_ref, h2_ref, aff_ref):
    gmat = g_ref[...]

    def gated(o, z, nw):
        return o * lax.rsqrt(_mm_group(o * o, gmat) + EPS) * nw * _silu(z)

    dmix = gated(dof_ref[0] + dob_ref[0], z_ref[0], dnw_ref[...])
    gmix = gated(gof_ref[0] + gob_ref[0], gg_ref[0], hgw_ref[...])
    y = (jnp.dot(a_ref[0], w_ref[0:ATTN_Q_W, :], preferred_element_type=F32)
         + jnp.dot(dmix.astype(BF16), w_ref[ATTN_Q_W:ATTN_Q_W + REC_W, :], preferred_element_type=F32)
         + jnp.dot(gmix.astype(BF16), w_ref[ATTN_Q_W + REC_W:, :], preferred_element_type=F32))
    x1 = x_ref[0] + gate_ref[0] * y
    x1_ref[0] = x1
    ms = jnp.mean(x1 * x1, axis=-1, keepdims=True)
    h2 = x1 * lax.rsqrt(ms + EPS) * (1.0 + sc_ref[0]) + sh_ref[0]
    h2_ref[0] = h2.astype(BF16)
    logits = lax.dot_general(rt_ref[...], h2, (((1,), (1,)), ((), ())), precision=HIGHEST,
                             preferred_element_type=F32)
    e = jnp.exp(logits - jnp.max(logits, axis=0, keepdims=True))
    aff_ref[0] = e / jnp.sum(e, axis=0, keepdims=True)


def _output_projection(a, dof, dob, dn, gof, gob, hg, dnw, hgw, gavg, w_out, x, gate, shift, scale, router_t):
    b, t, d = x.shape
    tm = _tile(t, 512)
    n = REC_W
    per_b = gate.shape[0] > 1
    mod_map = (lambda bi, i: (bi, 0, 0)) if per_b else (lambda bi, i: (0, 0, 0))
    const = lambda bi, i: (0, 0)
    tok = lambda bi, i: (bi, i, 0)
    return pl.pallas_call(
        _outproj_kernel,
        grid=(b, t // tm),
        in_specs=[pl.BlockSpec((1, tm, ATTN_Q_W), tok),
                  pl.BlockSpec((1, tm, n), tok), pl.BlockSpec((1, tm, n), tok),
                  pl.BlockSpec((1, tm, n), lambda bi, i: (bi, i, 3)),
                  pl.BlockSpec((1, tm, n), tok), pl.BlockSpec((1, tm, n), tok),
                  pl.BlockSpec((1, tm, n), lambda bi, i: (bi, i, 4)),
                  pl.BlockSpec((1, n), const), pl.BlockSpec((1, n), const),
                  pl.BlockSpec((n, n), const),
                  pl.BlockSpec((d, d), const),
                  pl.BlockSpec((1, tm, d), tok),
                  pl.BlockSpec((1, 1, d), mod_map), pl.BlockSpec((1, 1, d), mod_map),
                  pl.BlockSpec((1, 1, d), mod_map),
                  pl.BlockSpec((N_EXPERTS, d), const)],
        out_specs=(pl.BlockSpec((1, tm, d), tok), pl.BlockSpec((1, tm, d), tok),
                   pl.BlockSpec((1, N_EXPERTS, tm), lambda bi, i: (bi, 0, i))),
        out_shape=(jax.ShapeDtypeStruct((b, t, d), F32), jax.ShapeDtypeStruct((b, t, d), BF16),
                   jax.ShapeDtypeStruct((b, N_EXPERTS, t), F32)),
        compiler_params=_cparams(("parallel", "parallel")),
        name="output_projection_router",
    )(a, dof, dob, dn, gof, gob, hg, dnw, hgw, gavg, w_out, x, gate, shift, scale, router_t)


def _select_kernel(aff_ref, u_ref, bs_ref, bst_ref, su_ref, slot_ref, base_ref, nsub_ref, inc_scr, *, cap, nblk):
    x = aff_ref[0]
    bits = pltpu.bitcast(x, jnp.int32)
    lo = jnp.zeros((N_EXPERTS, 1), jnp.int32)
    for bit in range(30, -1, -1):
        cand = lo | (1 << bit)
        cnt = jnp.sum((bits >= cand).astype(jnp.int32), axis=1, keepdims=True)
        lo = jnp.where(cnt >= cap, cand, lo)
    gt = bits > lo
    eq = bits == lo
    umat = u_ref[...]

    def prefix(mask):
        mb = mask.astype(BF16)
        for j in range(nblk):
            inc_scr[:, j * LANES:(j + 1) * LANES] = jnp.dot(mb[:, j * LANES:(j + 1) * LANES], umat,
                                                            preferred_element_type=F32)
        totals = jnp.dot(mb, bs_ref[...], preferred_element_type=F32)
        offs = _mm_hi(totals, su_ref[...])
        return inc_scr[...] + _mm_hi(offs, bst_ref[...]), offs, totals

    eqf = eq.astype(F32)
    n_gt = jnp.sum(gt.astype(F32), axis=1, keepdims=True)
    eq_before, _, _ = prefix(eqf)
    sel = gt | (eq & ((eq_before - eqf) < (cap - n_gt)))
    self_ = sel.astype(F32)
    pos, offs, totals = prefix(self_)
    slot_ref[0] = jnp.where(sel, pos - 1.0, -1.0).astype(jnp.int32)
    start = jnp.floor(offs * (1.0 / ROUTE_ALIGN)) * ROUTE_ALIGN
    span = offs + totals - start
    n_sub = functools.reduce(jnp.add, [jnp.where(span > j * ROUTE_SUB, 1.0, 0.0)
                                       for j in range(ROUTE_WIN // ROUTE_SUB)])
    base_ref[0] = start.astype(jnp.int32)
    nsub_ref[0] = jnp.max(n_sub, axis=0, keepdims=True).astype(jnp.int32)


def _moe_select(aff_t, cap):
    b, e, t = aff_t.shape
    nblk = t // ROUTE_BLK
    u = jnp.asarray(np.triu(np.ones((LANES, LANES), np.float32)), BF16)
    blk = np.arange(t) // ROUTE_BLK
    bs = (blk[:, None] == np.arange(nblk)[None, :]).astype(np.float32)
    su = np.triu(np.ones((nblk, nblk), np.float32), 1)
    const = lambda bi: (0, 0)
    return pl.pallas_call(
        functools.partial(_select_kernel, cap=cap, nblk=nblk),
        grid=(b,),
        in_specs=[pl.BlockSpec((1, e, t), lambda bi: (bi, 0, 0)),
                  pl.BlockSpec((LANES, LANES), const),
                  pl.BlockSpec((t, nblk), const),
                  pl.BlockSpec((nblk, t), const),
                  pl.BlockSpec((nblk, nblk), const)],
        out_specs=(pl.BlockSpec((1, e, t), lambda bi: (bi, 0, 0)),
                   pl.BlockSpec((1, e, nblk), lambda bi: (bi, 0, 0)),
                   pl.BlockSpec((1, 1, nblk), lambda bi: (bi, 0, 0))),
        out_shape=(jax.ShapeDtypeStruct((b, e, t), jnp.int32), jax.ShapeDtypeStruct((b, e, nblk), jnp.int32),
                   jax.ShapeDtypeStruct((b, 1, nblk), jnp.int32)),
        scratch_shapes=[pltpu.VMEM((e, t), F32)],
        compiler_params=_cparams(("parallel",)),
        name="moe_select",
    )(aff_t, u, jnp.asarray(bs, BF16), jnp.asarray(bs.T), jnp.asarray(su))


def _gather_kernel(start_sm, nsub_sm, slot_ref, h_ref, xe_ref, *, nblk):
    bi = pl.program_id(0)
    xe_ref[0] = jnp.zeros(xe_ref.shape[1:], BF16)
    row = lax.broadcasted_iota(jnp.int32, (ROUTE_SUB, ROUTE_BLK), 0)

    def body(k, carry):
        t0 = pl.multiple_of(k * ROUTE_BLK, ROUTE_BLK)
        hblk = h_ref[0, pl.ds(t0, ROUTE_BLK), :]
        slots = slot_ref[0, :, pl.ds(t0, ROUTE_BLK)]
        firsts = [start_sm[(bi * N_EXPERTS + e) * nblk + k] for e in range(N_EXPERTS)]

        def sub(j, inner):
            starts = [pl.multiple_of(first + j * ROUTE_SUB, ROUTE_ALIGN) for first in firsts]
            onehot = jnp.concatenate(
                [jnp.where(row == slots[e:e + 1, :] - starts[e], 1.0, 0.0).astype(BF16)
                 for e in range(N_EXPERTS)], axis=0)
            rows = jnp.dot(onehot, hblk, preferred_element_type=F32).astype(BF16)
            for e in range(N_EXPERTS):
                win = pl.ds(starts[e], ROUTE_SUB)
                xe_ref[0, e, win, :] = xe_ref[0, e, win, :] + rows[e * ROUTE_SUB:(e + 1) * ROUTE_SUB]
            return inner

        lax.fori_loop(0, nsub_sm[bi * nblk + k], sub, 0)
        return carry

    lax.fori_loop(0, nblk, body, 0)


def _moe_gather(h2, slot, start_flat, nsub_flat, cap):
    b, t, d = h2.shape
    nblk = t // ROUTE_BLK
    cp = cap + ROUTE_WIN
    dq = _tile(d, 256)
    grid_spec = pltpu.PrefetchScalarGridSpec(
        num_scalar_prefetch=2,
        grid=(b, d // dq),
        in_specs=[pl.BlockSpec((1, N_EXPERTS, t), lambda bi, j, s0, s1: (bi, 0, 0)),
                  pl.BlockSpec((1, t, dq), lambda bi, j, s0, s1: (bi, 0, j))],
        out_specs=pl.BlockSpec((1, N_EXPERTS, cp, dq), lambda bi, j, s0, s1: (bi, 0, 0, j)),
    )
    return pl.pallas_call(
        functools.partial(_gather_kernel, nblk=nblk),
        grid_spec=grid_spec,
        out_shape=jax.ShapeDtypeStruct((b, N_EXPERTS, cp, d), BF16),
        compiler_params=_cparams(("parallel", "parallel")),
        name="moe_gather",
    )(start_flat, nsub_flat, slot, h2)


def _ffn_kernel(x_ref, wg_ref, wu_ref, wd_ref, y_ref, *, cap):
    x = x_ref[0, 0, 0:cap, :]
    a = jnp.dot(x, wg_ref[0], preferred_element_type=F32)
    u = jnp.dot(x, wu_ref[0], preferred_element_type=F32)
    y_ref[0, 0, 0:cap, :] = jnp.dot((_silu(a) * u).astype(BF16), wd_ref[0],
                                    preferred_element_type=F32).astype(BF16)
    y_ref[0, 0, cap:, :] = jnp.zeros((y_ref.shape[2] - cap, y_ref.shape[3]), BF16)


def _moe_ffn(xe, wg, wu, wd, cap):
    b, e, cp, d = xe.shape
    f = wg.shape[-1]
    return pl.pallas_call(
        functools.partial(_ffn_kernel, cap=cap),
        grid=(e, b),
        in_specs=[pl.BlockSpec((1, 1, cp, d), lambda ei, bi: (bi, ei, 0, 0)),
                  pl.BlockSpec((1, d, f), lambda ei, bi: (ei, 0, 0)),
                  pl.BlockSpec((1, d, f), lambda ei, bi: (ei, 0, 0)),
                  pl.BlockSpec((1, f, d), lambda ei, bi: (ei, 0, 0))],
        out_specs=pl.BlockSpec((1, 1, cp, d), lambda ei, bi: (bi, ei, 0, 0)),
        out_shape=jax.ShapeDtypeStruct((b, e, cp, d), BF16),
        compiler_params=_cparams(("parallel", "parallel")),
        name="moe_ffn",
    )(xe, wg, wu, wd)


def _combine_kernel(start_sm, nsub_sm, ye_ref, x_ref, gate_ref, slot_ref, aff_ref, o_ref, *, nblk, per_step):
    bi = pl.program_id(0)
    row = lax.broadcasted_iota(jnp.int32, (ROUTE_SUB, ROUTE_BLK), 0)
    for c in range(per_step):
        k = pl.program_id(2) * per_step + c
        tok = slice(c * ROUTE_BLK, (c + 1) * ROUTE_BLK)
        slots = slot_ref[0, :, tok]
        aff = aff_ref[0, :, tok]
        firsts = [start_sm[(bi * N_EXPERTS + e) * nblk + k] for e in range(N_EXPERTS)]

        def sub(j, acc, slots=slots, aff=aff, firsts=firsts):
            starts = [pl.multiple_of(first + j * ROUTE_SUB, ROUTE_ALIGN) for first in firsts]
            wsel = jnp.concatenate([jnp.where(row == slots[e:e + 1, :] - starts[e], aff[e:e + 1, :], 0.0)
                                    for e in range(N_EXPERTS)], axis=0)
            ys = jnp.concatenate([ye_ref[0, e, pl.ds(starts[e], ROUTE_SUB), :] for e in range(N_EXPERTS)],
                                 axis=0)
            hi, lo = _split2(wsel)
            tn = (((0,), (0,)), ((), ()))
            return (acc + lax.dot_general(hi, ys, tn, preferred_element_type=F32)
                    + lax.dot_general(lo, ys, tn, preferred_element_type=F32))

        acc = lax.fori_loop(0, nsub_sm[bi * nblk + k], sub, jnp.zeros((ROUTE_BLK, x_ref.shape[2]), F32))
        o_ref[0, tok, :] = x_ref[0, tok, :] + gate_ref[0] * acc


def _moe_combine(ye, x1, gate, slot, aff_t, start_flat, nsub_flat):
    b, t, d = x1.shape
    cp = ye.shape[2]
    nblk = t // ROUTE_BLK
    dh = d // 2
    per_b = gate.shape[0] > 1
    per_step = _tile(nblk, COMBINE_BLOCKS)
    tok = per_step * ROUTE_BLK
    grid_spec = pltpu.PrefetchScalarGridSpec(
        num_scalar_prefetch=2,
        grid=(b, 2, nblk // per_step),
        in_specs=[pl.BlockSpec((1, N_EXPERTS, cp, dh), lambda bi, j, k, s0, s1: (bi, 0, 0, j)),
                  pl.BlockSpec((1, tok, dh), lambda bi, j, k, s0, s1: (bi, k, j)),
                  pl.BlockSpec((1, 1, dh), (lambda bi, j, k, s0, s1: (bi, 0, j)) if per_b
                               else (lambda bi, j, k, s0, s1: (0, 0, j))),
                  pl.BlockSpec((1, N_EXPERTS, tok), lambda bi, j, k, s0, s1: (bi, 0, k)),
                  pl.BlockSpec((1, N_EXPERTS, tok), lambda bi, j, k, s0, s1: (bi, 0, k))],
        out_specs=pl.BlockSpec((1, tok, dh), lambda bi, j, k, s0, s1: (bi, k, j)),
    )
    return pl.pallas_call(
        functools.partial(_combine_kernel, nblk=nblk, per_step=per_step),
        grid_spec=grid_spec,
        out_shape=jax.ShapeDtypeStruct((b, t, d), F32),
        compiler_params=_cparams(("parallel", "parallel", "arbitrary")),
        name="moe_combine",
    )(start_flat, nsub_flat, ye, x1, gate, slot, aff_t)


def _expert_choice_ffn(x1, h2, aff_t, gate, wg, wu, wd):
    b, t, _ = x1.shape
    cap = CAPACITY_FACTOR * t // N_EXPERTS
    slot, start, nsub = _moe_select(aff_t, cap)
    start_flat, nsub_flat = start.reshape(-1), nsub.reshape(-1)
    xe = _moe_gather(h2, slot, start_flat, nsub_flat, cap)
    ye = _moe_ffn(xe, wg, wu, wd, cap)
    return _moe_combine(ye, x1, gate, slot, aff_t, start_flat, nsub_flat)


def _rope_tables(n_tokens):
    rows = n_tokens // GRID_W
    row = jnp.repeat(jnp.arange(rows, dtype=F32), GRID_W)
    col = jnp.tile(jnp.arange(GRID_W, dtype=F32), rows)
    n_freq = HEAD_DIM // 4
    inv_freq = ROPE_THETA ** (-jnp.arange(n_freq, dtype=F32) / n_freq)
    ang_r = row[:, None] * inv_freq
    ang_c = col[:, None] * inv_freq
    cos = jnp.concatenate([jnp.cos(ang_r)] * 2 + [jnp.cos(ang_c)] * 2, axis=-1)
    sin = jnp.concatenate([jnp.sin(ang_r)] * 2 + [jnp.sin(ang_c)] * 2, axis=-1)
    return jnp.tile(cos, (1, LANES // HEAD_DIM)), jnp.tile(sin, (1, LANES // HEAD_DIM))


def _block_diag(width, block, value):
    idx = np.arange(width) // block
    return jnp.asarray((idx[:, None] == idx[None, :]).astype(np.float32) * value)


def kernel(x, c, ctx, c_ctx, w_mod, b_mod, w_in, w_out, attn_q_norm, attn_k_norm, dn_conv, dn_a_log,
           dn_dt_bias, dn_norm, hg_lower_bounds, hg_norm, moe_router, moe_w_gate, moe_w_up, moe_w_down):
    depth = w_mod.shape[0]
    b, t_lat, d = x.shape
    cos, sin = _rope_tables(t_lat)
    g_head = _block_diag(LANES, HEAD_DIM, 1.0 / HEAD_DIM)
    g_mean = _block_diag(REC_W, HEAD_DIM, 1.0 / HEAD_DIM)
    g_sum = _block_diag(REC_W, HEAD_DIM, 1.0)
    consts = _scan_constants()
    s_zero = jnp.zeros((b, 2, REC_W, REC_W), F32)

    lb_w = jax.nn.softmax(hg_lower_bounds.astype(F32), axis=0)
    hg_lb = jnp.cumsum(lb_w, axis=0) - lb_w[0]

    rows = ((b + 1 + 7) // 8) * 8
    cond = jnp.zeros((rows, d), F32).at[:b].set(c).at[b].set(c_ctx)
    mod = _modulation(cond, w_mod, b_mod)

    n_small = 4 * DN_HEADS
    w_in_r = jnp.concatenate(
        [w_in[:, :, :C_HG], w_in[:, :, C_HG + n_small:], w_in[:, :, C_HG:C_HG + n_small],
         jnp.zeros((depth, d, C_END - C_BA - n_small), w_in.dtype)], axis=-1).astype(BF16)
    w_out_b = _cast_bf16(w_out)
    wg_b, wu_b, wd_b = _cast_bf16(moe_w_gate), _cast_bf16(moe_w_up), _cast_bf16(moe_w_down)

    x_lat, x_ctx = x, ctx
    for l in range(depth):
        ctx_out = l < depth - 1
        m_lat = [mod[l, :b, j * d:(j + 1) * d][:, None, :] for j in range(6)]
        m_ctx = [mod[l, b:b + 1, j * d:(j + 1) * d][:, None, :] for j in range(6)]
        qn = jnp.tile(attn_q_norm[l], LANES // HEAD_DIM)[None, :]
        kn = jnp.tile(attn_k_norm[l], LANES // HEAD_DIM)[None, :]
        conv_w = jnp.zeros((8, 3 * REC_W), F32).at[:CONV_K].set(dn_conv[l])
        pad = jnp.zeros((LANES - 4 * DN_HEADS,), F32)
        neg_a = jnp.concatenate([jnp.zeros((2 * DN_HEADS,), F32), -jnp.exp(dn_a_log[l].reshape(-1)), pad])[None, :]
        dt_b = jnp.concatenate([jnp.zeros((2 * DN_HEADS,), F32), dn_dt_bias[l].reshape(-1), pad])[None, :]
        dnw = jnp.tile(dn_norm[l], DN_HEADS)[None, :]
        hgw = jnp.tile(hg_norm[l], HG_HEADS)[None, :]
        router_t = moe_router[l].T

        streams = {}
        for name, xs, ms, rotate in (("ctx", x_ctx, m_ctx, False), ("lat", x_lat, m_lat, True)):
            t = xs.shape[1]
            q, kt, v, dn, hg, ba = _input_projection(xs, ms[0], ms[1], w_in_r[l], qn, kn,
                                                     cos[:t], sin[:t], g_head, rotate)
            qkv, gb = _dn_prep(dn, ba, conv_w, neg_a, dt_b, g_sum)
            streams[name] = dict(q=q, kv=(kt, v), dn=dn, hg=hg, qkv=qkv, gb=gb)

        sc, sl = streams["ctx"], streams["lat"]
        dcf, dcb, dn_state = _dn_scan(sc["qkv"], sc["gb"], s_zero, consts)
        dlf, dlb, _ = _dn_scan(sl["qkv"], sl["gb"], dn_state, consts)
        gcf, gcb, hg_state = _hg_scan(sc["hg"], hg_lb[l], s_zero, consts)
        glf, glb, _ = _hg_scan(sl["hg"], hg_lb[l], hg_state, consts)

        a_lat = _attention(sl["q"], [sc["kv"], sl["kv"]], t_lat)
        x1, h2, aff_t = _output_projection(
            a_lat, dlf, dlb, sl["dn"], glf, glb, sl["hg"], dnw, hgw, g_mean, w_out_b[l], x_lat,
            m_lat[2], m_lat[3], m_lat[4], router_t)
        x_lat = _expert_choice_ffn(x1, h2, aff_t, m_lat[5], wg_b[l], wu_b[l], wd_b[l])
        if ctx_out:
            a_ctx = _attention(sc["q"], [sc["kv"]], x_ctx.shape[1])
            x1, h2, aff_t = _output_projection(
                a_ctx, dcf, dcb, sc["dn"], gcf, gcb, sc["hg"], dnw, hgw, g_mean, w_out_b[l], x_ctx,
                m_ctx[2], m_ctx[3], m_ctx[4], router_t)
            x_ctx = _expert_choice_ffn(x1, h2, aff_t, m_ctx[5], wg_b[l], wu_b[l], wd_b[l])
    return x_lat
```

```python
import functools

import numpy as np
import jax
import jax.numpy as jnp
from jax import lax
from jax.experimental import pallas as pl
from jax.experimental.pallas import tpu as pltpu

F32 = jnp.float32
BF16 = jnp.bfloat16
HIGHEST = lax.Precision.HIGHEST

HEAD_DIM = 64
N_Q_HEADS = 8
N_KV_HEADS = 2
GQA_GROUP = N_Q_HEADS // N_KV_HEADS
DN_HEADS = 4
HG_HEADS = 4
GRID_W = 64
ROPE_THETA = 10000.0
CONV_K = 5
CHUNK = 64
N_EXPERTS = 16
CAPACITY_FACTOR = 2
EPS = 1e-6
LOG2E = 1.4426950408889634
ATTN_KV_GROUP = 1408
V_ROWS = HEAD_DIM + 16

ATTN_Q_W = N_Q_HEADS * HEAD_DIM
ATTN_KV_W = N_KV_HEADS * HEAD_DIM
REC_W = DN_HEADS * HEAD_DIM
LANES = 128
ROUTE_BLK = LANES
ROUTE_ALIGN = 16
ROUTE_WIN = ROUTE_BLK + ROUTE_ALIGN
COMBINE_BLOCKS = 4
ROUTE_SUB = ROUTE_WIN // 3
N_LEVELS = 6
SCAN_BATCH = 4
VMEM_LIMIT = 56 * 1024 * 1024

C_Q, C_K, C_V, C_DN, C_HG, C_BA, C_END = 0, 512, 640, 768, 1792, 3072, 3200


def _mm(a, b):
    return jnp.dot(a.astype(BF16), b.astype(BF16), preferred_element_type=F32)


def _mm_nt(a, b):
    return lax.dot_general(a.astype(BF16), b.astype(BF16), (((1,), (1,)), ((), ())),
                           preferred_element_type=F32)


def _mm_tn(a, b):
    return lax.dot_general(a.astype(BF16), b.astype(BF16), (((0,), (0,)), ((), ())),
                           preferred_element_type=F32)


def _mm_hi(a, b):
    return jnp.dot(a, b, precision=HIGHEST, preferred_element_type=F32)


def _mm_group(a, gmat):
    hi = a.astype(BF16)
    lo = (a - hi.astype(F32)).astype(BF16)
    t = jnp.dot(jnp.concatenate([hi, lo], axis=0), gmat.astype(BF16), preferred_element_type=F32)
    return t[:a.shape[0]] + t[a.shape[0]:]


def _sigmoid(x):
    return 1.0 / (1.0 + jnp.exp(-x))


def _silu(x):
    return x * _sigmoid(x)


def _cparams(sem):
    return pltpu.CompilerParams(dimension_semantics=sem, vmem_limit_bytes=VMEM_LIMIT)


def _tile(n, pref):
    return pref if n % pref == 0 else n


def _cast_kernel(x_ref, o_ref):
    o_ref[...] = x_ref[...].astype(BF16)


def _cast_bf16(w):
    r, c = w.shape[-2:]
    w3 = w.reshape(-1, r, c)
    rb = _tile(r, 512)
    out = pl.pallas_call(
        _cast_kernel,
        grid=(w3.shape[0], r // rb),
        in_specs=[pl.BlockSpec((1, rb, c), lambda i, j: (i, j, 0))],
        out_specs=pl.BlockSpec((1, rb, c), lambda i, j: (i, j, 0)),
        out_shape=jax.ShapeDtypeStruct(w3.shape, BF16),
        compiler_params=_cparams(("parallel", "parallel")),
        name="cast_bf16",
    )(w3)
    return out.reshape(w.shape)


def _mod_kernel(c_ref, w_ref, b_ref, o_ref):
    o_ref[0] = _mm_hi(_silu(c_ref[...]), w_ref[0]) + b_ref[0]


def _modulation(cond, w_mod, b_mod):
    depth, d, n = w_mod.shape
    rows = cond.shape[0]
    tn = _tile(n, 1536)
    return pl.pallas_call(
        _mod_kernel,
        grid=(depth, n // tn),
        in_specs=[pl.BlockSpec((rows, d), lambda l, j: (0, 0)),
                  pl.BlockSpec((1, d, tn), lambda l, j: (l, 0, j)),
                  pl.BlockSpec((1, 1, tn), lambda l, j: (l, 0, j))],
        out_specs=pl.BlockSpec((1, rows, tn), lambda l, j: (l, 0, j)),
        out_shape=jax.ShapeDtypeStruct((depth, rows, n), F32),
        compiler_params=_cparams(("parallel", "parallel")),
        name="modulation",
    )(cond, w_mod, b_mod.reshape(depth, 1, n))


def _inproj_kernel(x_ref, sh_ref, sc_ref, w_ref, qn_ref, kn_ref, cos_ref, sin_ref, g_ref,
                   q_ref, k_ref, v_ref, dn_ref, hg_ref, ba_ref, *, rotate):
    x = x_ref[0]
    ms = jnp.mean(x * x, axis=-1, keepdims=True)
    h = x * lax.rsqrt(ms + EPS) * (1.0 + sc_ref[0]) + sh_ref[0]
    hb = h.astype(BF16)
    gmat = g_ref[...]

    def proj(lo, hi):
        return jnp.dot(hb, w_ref[:, lo:hi], preferred_element_type=F32)

    def head_norm_rope(t, nw):
        t = t * lax.rsqrt(_mm_group(t * t, gmat) + EPS) * nw
        if rotate:
            lane = lax.broadcasted_iota(jnp.int32, t.shape, 1)
            first = (lane % 32) < 16
            rot = jnp.where(first, -pltpu.roll(t, LANES - 16, 1), pltpu.roll(t, 16, 1))
            t = t * cos_ref[...] + rot * sin_ref[...]
        return t

    def q_group(j):
        t = proj(C_Q + j * LANES, C_Q + (j + 1) * LANES)
        yield
        qj = head_norm_rope(t, qn_ref[...])
        qt = (qj * (HEAD_DIM ** -0.5 * LOG2E)).T.astype(BF16)
        q_ref[0, 2 * j] = qt[:HEAD_DIM]
        q_ref[0, 2 * j + 1] = qt[HEAD_DIM:]

    def k_group():
        t = proj(C_K, C_V)
        yield
        k = head_norm_rope(t, kn_ref[...]).astype(BF16)
        k_ref[0, 0] = k[:, :HEAD_DIM]
        k_ref[0, 1] = k[:, HEAD_DIM:]

    def v_group():
        v = proj(C_V, C_DN)
        yield
        vt = v.T
        ones = jnp.ones((V_ROWS - HEAD_DIM, vt.shape[1]), F32)
        v_ref[0, 0] = jnp.concatenate([vt[:HEAD_DIM], ones], axis=0).astype(BF16)
        v_ref[0, 1] = jnp.concatenate([vt[HEAD_DIM:], ones], axis=0).astype(BF16)

    def plain_group(o_ref, base, lo, hi):
        t = proj(base + lo, base + hi)
        yield
        o_ref[0, :, lo:hi] = t

    step = 4 * LANES
    plain = [plain_group(o_ref, base, lo, min(lo + step, width))
             for o_ref, base, width in ((dn_ref, C_DN, C_HG - C_DN), (hg_ref, C_HG, C_BA - C_HG),
                                        (ba_ref, C_BA, C_END - C_BA))
             for lo in range(0, width, step)]
    _lockstep([q_group(j) for j in range(ATTN_Q_W // LANES)] + [k_group(), v_group()] + plain, stagger=True)


def _input_projection(x, shift, scale, w, qn, kn, cos, sin, gmat, rotate):
    b, t, d = x.shape
    tm = _tile(t, 512)
    per_b = shift.shape[0] > 1
    mod_map = (lambda bi, i: (bi, 0, 0)) if per_b else (lambda bi, i: (0, 0, 0))
    const = lambda bi, i: (0, 0)
    out_shape = (
        jax.ShapeDtypeStruct((b, N_Q_HEADS, HEAD_DIM, t), BF16),
        jax.ShapeDtypeStruct((b, N_KV_HEADS, t, HEAD_DIM), BF16),
        jax.ShapeDtypeStruct((b, N_KV_HEADS, V_ROWS, t), BF16),
        jax.ShapeDtypeStruct((b, t, C_HG - C_DN), F32),
        jax.ShapeDtypeStruct((b, t, C_BA - C_HG), F32),
        jax.ShapeDtypeStruct((b, t, C_END - C_BA), F32),
    )
    return pl.pallas_call(
        functools.partial(_inproj_kernel, rotate=rotate),
        grid=(b, t // tm),
        in_specs=[pl.BlockSpec((1, tm, d), lambda bi, i: (bi, i, 0)),
                  pl.BlockSpec((1, 1, d), mod_map),
                  pl.BlockSpec((1, 1, d), mod_map),
                  pl.BlockSpec((d, C_END), const),
                  pl.BlockSpec((1, LANES), const),
                  pl.BlockSpec((1, LANES), const),
                  pl.BlockSpec((tm, LANES), lambda bi, i: (i, 0)),
                  pl.BlockSpec((tm, LANES), lambda bi, i: (i, 0)),
                  pl.BlockSpec((LANES, LANES), const)],
        out_specs=(pl.BlockSpec((1, N_Q_HEADS, HEAD_DIM, tm), lambda bi, i: (bi, 0, 0, i)),
                   pl.BlockSpec((1, N_KV_HEADS, tm, HEAD_DIM), lambda bi, i: (bi, 0, i, 0)),
                   pl.BlockSpec((1, N_KV_HEADS, V_ROWS, tm), lambda bi, i: (bi, 0, 0, i)),
                   pl.BlockSpec((1, tm, C_HG - C_DN), lambda bi, i: (bi, i, 0)),
                   pl.BlockSpec((1, tm, C_BA - C_HG), lambda bi, i: (bi, i, 0)),
                   pl.BlockSpec((1, tm, C_END - C_BA), lambda bi, i: (bi, i, 0))),
        out_shape=out_shape,
        compiler_params=_cparams(("parallel", "parallel")),
        name="input_projection",
    )(x, shift, scale, w, qn, kn, cos, sin, gmat)


def _lockstep(gens, stagger=False):
    results = [None] * len(gens)
    live, started = [], 0
    while live or started < len(gens):
        fresh = 1 if stagger else len(gens)
        live += list(range(started, min(started + fresh, len(gens))))
        started = min(started + fresh, len(gens))
        for i in reversed(list(live)):
            try:
                next(gens[i])
            except StopIteration as stop:
                results[i] = stop.value
                live.remove(i)
    return results


def _attn_part(qt, ks, vts):
    ss = [jnp.dot(k, qt, preferred_element_type=F32).astype(BF16) for k in ks]
    yield
    m = functools.reduce(jnp.maximum, [jnp.max(s, axis=0, keepdims=True) for s in ss])
    ps = [jnp.exp2(s - m) for s in ss]
    yield
    o = functools.reduce(jnp.add, [jnp.dot(vt, p, preferred_element_type=F32) for p, vt in zip(ps, vts)])
    return m.astype(F32), o


def _attn_kernel(*refs, n_src):
    q_ref, o_ref = refs[0], refs[-1]
    groups, cur, room = [], ([], []), ATTN_KV_GROUP
    for i in range(n_src):
        k_ref, vt_ref = refs[1 + 2 * i], refs[2 + 2 * i]
        lo, tk = 0, k_ref.shape[2]
        while lo < tk:
            n = min(room, tk - lo)
            cur[0].append(k_ref[0, 0, lo:lo + n, :])
            cur[1].append(vt_ref[0, 0, :, lo:lo + n])
            lo, room = lo + n, room - n
            if room == 0:
                groups.append(cur)
                cur, room = ([], []), ATTN_KV_GROUP
    if cur[0]:
        groups.append(cur)
    chains = [(h, g) for h in range(GQA_GROUP) for g in range(len(groups))]
    parts = _lockstep([_attn_part(q_ref[0, h], *groups[g]) for h, g in chains], stagger=True)
    outs = []
    for h in range(GQA_GROUP):
        mine = [parts[i] for i, (hh, _) in enumerate(chains) if hh == h]
        m = functools.reduce(jnp.maximum, [pm for pm, _ in mine])
        o = functools.reduce(jnp.add, [po * jnp.exp2(pm - m) for pm, po in mine])
        outs.append(o[:HEAD_DIM] / o[HEAD_DIM:HEAD_DIM + 1])
    o_ref[0] = jnp.concatenate(outs, axis=0).T.astype(BF16)


def _attention(qt, sources, t):
    b = qt.shape[0]
    tq = _tile(t, 256)
    gw = GQA_GROUP * HEAD_DIM
    in_specs = [pl.BlockSpec((1, GQA_GROUP, HEAD_DIM, tq), lambda bi, g, i: (bi, g, 0, i))]
    args = [qt]
    for k, vt in sources:
        tk = k.shape[2]
        in_specs.append(pl.BlockSpec((1, 1, tk, HEAD_DIM), lambda bi, g, i: (bi, g, 0, 0)))
        in_specs.append(pl.BlockSpec((1, 1, V_ROWS, tk), lambda bi, g, i: (bi, g, 0, 0)))
        args += [k, vt]
    return pl.pallas_call(
        functools.partial(_attn_kernel, n_src=len(sources)),
        grid=(b, N_KV_HEADS, t // tq),
        in_specs=in_specs,
        out_specs=pl.BlockSpec((1, tq, gw), lambda bi, g, i: (bi, i, g)),
        out_shape=jax.ShapeDtypeStruct((b, t, ATTN_Q_W), BF16),
        compiler_params=_cparams(("parallel", "parallel", "parallel")),
        name="attention",
    )(*args)


def _dn_prep_kernel(x_ref, xp_ref, xn_ref, ba_ref, cw_ref, na_ref, dtb_ref, g_ref, o_ref, gb_ref, buf):
    i = pl.program_id(1)
    n = pl.num_programs(1)
    tm = x_ref.shape[1]
    buf[0:8] = jnp.where(i > 0, xp_ref[0], 0.0)
    buf[8:8 + tm] = x_ref[0]
    buf[8 + tm:16 + tm] = jnp.where(i < n - 1, xn_ref[0], 0.0)
    half = CONV_K // 2
    y = cw_ref[0:1, :] * buf[8 - half:8 - half + tm]
    for j in range(1, CONV_K):
        y = y + cw_ref[j:j + 1, :] * buf[8 - half + j:8 - half + j + tm]
    y = _silu(y)
    gmat = g_ref[...]
    q = y[:, :REC_W]
    k = y[:, REC_W:2 * REC_W]
    o_ref[0, :, 0:REC_W] = q * lax.rsqrt(_mm_group(q * q, gmat) + EPS) * HEAD_DIM ** -0.5
    o_ref[0, :, REC_W:2 * REC_W] = k * lax.rsqrt(_mm_group(k * k, gmat) + EPS)
    o_ref[0, :, 2 * REC_W:] = y[:, 2 * REC_W:]
    ba = ba_ref[0]
    z = ba + dtb_ref[...]
    softplus = jnp.maximum(z, 0.0) + jnp.log1p(jnp.exp(-jnp.abs(z)))
    lane = lax.broadcasted_iota(jnp.int32, ba.shape, 1)
    gb_ref[0] = jnp.where(lane < 2 * DN_HEADS, _sigmoid(ba), na_ref[...] * softplus)


def _dn_prep(dn, ba, conv_w, neg_a, dt_bias, gsum):
    b, t, _ = dn.shape
    tm = _tile(t, 512)
    w3 = 3 * REC_W
    nb8 = t // 8
    return pl.pallas_call(
        _dn_prep_kernel,
        grid=(b, t // tm),
        in_specs=[pl.BlockSpec((1, tm, w3), lambda bi, i: (bi, i, 0)),
                  pl.BlockSpec((1, 8, w3), lambda bi, i: (bi, jnp.maximum(i * (tm // 8) - 1, 0), 0)),
                  pl.BlockSpec((1, 8, w3), lambda bi, i: (bi, jnp.minimum((i + 1) * (tm // 8), nb8 - 1), 0)),
                  pl.BlockSpec((1, tm, LANES), lambda bi, i: (bi, i, 0)),
                  pl.BlockSpec((8, w3), lambda bi, i: (0, 0)),
                  pl.BlockSpec((1, LANES), lambda bi, i: (0, 0)),
                  pl.BlockSpec((1, LANES), lambda bi, i: (0, 0)),
                  pl.BlockSpec((REC_W, REC_W), lambda bi, i: (0, 0))],
        out_specs=(pl.BlockSpec((1, tm, w3), lambda bi, i: (bi, i, 0)),
                   pl.BlockSpec((1, tm, LANES), lambda bi, i: (bi, i, 0))),
        out_shape=(jax.ShapeDtypeStruct((b, t, w3), F32), jax.ShapeDtypeStruct((b, t, LANES), F32)),
        scratch_shapes=[pltpu.VMEM((tm + 16, w3), F32)],
        compiler_params=_cparams(("parallel", "parallel")),
        name="deltanet_prep",
    )(dn, dn, dn, ba, conv_w, neg_a, dt_bias, gsum)


def _split2(a):
    hi = a.astype(BF16)
    return hi, (a - hi.astype(F32)).astype(BF16)


def _split3(a):
    hi = a.astype(BF16)
    r = a - hi.astype(F32)
    mid = r.astype(BF16)
    return hi, mid, (r - mid.astype(F32)).astype(BF16)


def _bdiag(a, hm):
    return jnp.concatenate([a] * DN_HEADS, axis=0) * hm


def _mm_bd(a, b, hm):
    return jnp.dot(a.astype(BF16), _bdiag(b.astype(BF16), hm), preferred_element_type=F32)


def _mm_sel_l(sel, b, terms=3):
    n = b.shape[1]
    parts = _split3(b) if terms == 3 else _split2(b)
    t = jnp.dot(sel.astype(BF16), jnp.concatenate(parts, axis=1), preferred_element_type=F32)
    return functools.reduce(jnp.add, [t[:, i * n:(i + 1) * n] for i in range(terms)])


def _mm_sel_r(a, sel):
    m = a.shape[0]
    t = jnp.dot(jnp.concatenate(_split3(a), axis=0), sel.astype(BF16), preferred_element_type=F32)
    return t[:m] + t[m:2 * m] + t[2 * m:]


def _dn_chunk(x, gb, st, ex, lmat, causal, lvl_ref, d, hm):
    n = REC_W
    lvl = lambda j: lvl_ref[d, j * CHUNK:(j + 1) * CHUNK, :]
    eye = lvl(N_LEVELS)
    ones8 = jnp.ones((8, CHUNK), F32)
    q, k, v = x[:, :n], x[:, n:2 * n], x[:, 2 * n:]
    gbx = _mm_sel_r(gb, ex)
    yield
    beta, g = gbx[:, :n], gbx[:, n:]
    cum = _mm_sel_l(lmat, g)
    last = 0 if d else CHUNK - 1
    clast = cum[last:last + 1]
    kb = k * beta
    ks = _bdiag(k.astype(BF16), hm)
    kk = lax.dot_general(kb.astype(BF16), ks, (((1,), (1,)), ((), ())), preferred_element_type=F32)
    qk = lax.dot_general(q.astype(BF16), ks, (((1,), (1,)), ((), ())), preferred_element_type=F32)
    yield
    cum_s = _mm_sel_l(ones8, cum * eye)[0:1]
    ecum = jnp.exp(cum)
    kdec = k * jnp.exp(clast - cum)
    yield
    decay = jnp.where(causal, jnp.exp(jnp.where(causal, cum - cum_s, 0.0)), 0.0)
    m = kk * decay
    attn = qk * decay
    inv = eye - lvl(N_LEVELS - 1) * m
    for lev in range(N_LEVELS - 2, -1, -1):
        half = _mm_bd(inv, lvl(lev) * m, hm)
        yield
        inv = inv - _mm_bd(half, inv, hm)
        yield
    rhs = v * beta - _mm_nt(kb * ecum, st)
    qs = _mm_nt(q * ecum, st)
    yield
    v_new = _mm_bd(inv, rhs, hm)
    yield
    out = qs + _mm_bd(attn, v_new, hm)
    st_new = st * jnp.exp(clast) + _mm_tn(v_new, kdec) * hm.astype(F32)
    return out, st_new


def _hg_chunk(qr, fr, v, lb, st, amat, lvl_ref, d, hm):
    lvl = lambda j: lvl_ref[d, j * CHUNK:(j + 1) * CHUNK, :]
    last = 0 if d else CHUNK - 1
    q = _silu(qr)
    f = lb + (1.0 - lb) * _sigmoid(fr)
    k = 1.0 - f
    g = jnp.log(f)
    ex = jnp.exp(_mm_sel_l(amat, g, terms=2))
    yield

    def level(xl, mask):
        kl = _bdiag((k * xl).astype(BF16), hm)
        return lax.dot_general((q * xl).astype(BF16), kl, (((1,), (1,)), ((), ())),
                               preferred_element_type=F32) * mask

    attn = level(1.0, lvl(N_LEVELS))
    for lev in range(N_LEVELS):
        attn = attn + level(ex[(2 + lev) * CHUNK:(3 + lev) * CHUNK], lvl(lev))
    qs = _mm_nt(q * ex[0:CHUNK], st)
    vk = _mm_tn(v, k * ex[CHUNK:2 * CHUNK])
    yield
    out = _mm_bd(attn, v, hm) + qs
    st_new = st * ex[last:last + 1] + vk * hm.astype(F32)
    return out, st_new


def _dn_scan_kernel(xf_ref, xb_ref, gf_ref, gb_ref, s0_ref, ex_ref, l_ref, cz_ref, lvl_ref, hm_ref,
                    of_ref, ob_ref, sf_ref, s_scr, *, nchunk):
    @pl.when(pl.program_id(1) == 0)
    def _():
        s_scr[...] = s0_ref[...]

    hm = hm_ref[...]
    refs = ((xf_ref, gf_ref, of_ref), (xb_ref, gb_ref, ob_ref))

    def body(c, carry):
        rows = [pl.ds(pl.multiple_of(cc * CHUNK, CHUNK), CHUNK) for cc in (c, nchunk - 1 - c)]
        insts = [(bi, d) for bi in range(s_scr.shape[0]) for d in range(2)]
        res = _lockstep([_dn_chunk(refs[d][0][bi, rows[d], :], refs[d][1][bi, rows[d], :], s_scr[bi, d],
                                   ex_ref[d], l_ref[d], cz_ref[d] > 0, lvl_ref, d, hm) for bi, d in insts])
        for (bi, d), (out, s_new) in zip(insts, res):
            refs[d][2][bi, rows[d], :] = out
            s_scr[bi, d] = s_new
        return carry

    lax.fori_loop(0, nchunk, body, 0)
    sf_ref[...] = s_scr[...]


def _hg_scan_kernel(qf_ref, qb_ref, ff_ref, fb_ref, if_ref, ib_ref, lb_ref, s0_ref, a_ref, lvl_ref, hm_ref,
                    of_ref, ob_ref, sf_ref, s_scr, *, nchunk):
    @pl.when(pl.program_id(1) == 0)
    def _():
        s_scr[...] = s0_ref[...]

    hm = hm_ref[...]
    refs = ((qf_ref, ff_ref, if_ref, of_ref), (qb_ref, fb_ref, ib_ref, ob_ref))

    def body(c, carry):
        rows = [pl.ds(pl.multiple_of(cc * CHUNK, CHUNK), CHUNK) for cc in (c, nchunk - 1 - c)]
        insts = [(bi, d) for bi in range(s_scr.shape[0]) for d in range(2)]
        res = _lockstep([_hg_chunk(refs[d][0][bi, rows[d], :], refs[d][1][bi, rows[d], :],
                                   refs[d][2][bi, rows[d], :], lb_ref[d:d + 1, :], s_scr[bi, d], a_ref[d],
                                   lvl_ref, d, hm) for bi, d in insts])
        for (bi, d), (out, s_new) in zip(insts, res):
            refs[d][3][bi, rows[d], :] = out
            s_scr[bi, d] = s_new
        return carry

    lax.fori_loop(0, nchunk, body, 0)
    sf_ref[...] = s_scr[...]


def _dn_scan(qkv, gb, s0, consts):
    b, t, w3 = qkv.shape
    tb = _tile(t, 512)
    nblk = t // tb
    n = REC_W
    nb = _tile(b, SCAN_BATCH)
    fwd = lambda bi, i: (bi, i, 0)
    bwd = lambda bi, i: (bi, nblk - 1 - i, 0)
    c3 = lambda bi, i: (0, 0, 0)
    state = pl.BlockSpec((nb, 2, n, n), lambda bi, i: (bi, 0, 0, 0))
    return pl.pallas_call(
        functools.partial(_dn_scan_kernel, nchunk=tb // CHUNK),
        grid=(b // nb, nblk),
        in_specs=[pl.BlockSpec((nb, tb, w3), fwd), pl.BlockSpec((nb, tb, w3), bwd),
                  pl.BlockSpec((nb, tb, LANES), fwd), pl.BlockSpec((nb, tb, LANES), bwd),
                  state,
                  pl.BlockSpec((2, LANES, 2 * n), c3),
                  pl.BlockSpec((2, CHUNK, CHUNK), c3),
                  pl.BlockSpec((2, CHUNK, n), c3),
                  pl.BlockSpec((2, (N_LEVELS + 1) * CHUNK, n), c3),
                  pl.BlockSpec((n, n), lambda bi, i: (0, 0))],
        out_specs=(pl.BlockSpec((nb, tb, n), fwd), pl.BlockSpec((nb, tb, n), bwd), state),
        out_shape=(jax.ShapeDtypeStruct((b, t, n), F32), jax.ShapeDtypeStruct((b, t, n), F32),
                   jax.ShapeDtypeStruct((b, 2, n, n), F32)),
        scratch_shapes=[pltpu.VMEM((nb, 2, n, n), F32)],
        compiler_params=_cparams(("parallel", "arbitrary")),
        name="deltanet_scan",
    )(qkv, qkv, gb, gb, s0, consts["ex"], consts["l"], consts["cz"], consts["lvl"], consts["hm"])


def _hg_scan(hg, lb, s0, consts):
    b, t, _ = hg.shape
    tb = _tile(t, 512)
    nblk = t // tb
    n = REC_W
    nb = _tile(b, SCAN_BATCH)
    fwd = lambda col: (lambda bi, i: (bi, i, col))
    bwd = lambda col: (lambda bi, i: (bi, nblk - 1 - i, col))
    c3 = lambda bi, i: (0, 0, 0)
    blk = lambda m: pl.BlockSpec((nb, tb, n), m)
    state = pl.BlockSpec((nb, 2, n, n), lambda bi, i: (bi, 0, 0, 0))
    return pl.pallas_call(
        functools.partial(_hg_scan_kernel, nchunk=tb // CHUNK),
        grid=(b // nb, nblk),
        in_specs=[blk(fwd(0)), blk(bwd(0)), blk(fwd(1)), blk(bwd(2)), blk(fwd(3)), blk(bwd(3)),
                  pl.BlockSpec((2, n), lambda bi, i: (0, 0)),
                  state,
                  pl.BlockSpec((2, (2 + N_LEVELS) * CHUNK, CHUNK), c3),
                  pl.BlockSpec((2, (N_LEVELS + 1) * CHUNK, n), c3),
                  pl.BlockSpec((n, n), lambda bi, i: (0, 0))],
        out_specs=(blk(fwd(0)), blk(bwd(0)), state),
        out_shape=(jax.ShapeDtypeStruct((b, t, n), F32), jax.ShapeDtypeStruct((b, t, n), F32),
                   jax.ShapeDtypeStruct((b, 2, n, n), F32)),
        scratch_shapes=[pltpu.VMEM((nb, 2, n, n), F32)],
        compiler_params=_cparams(("parallel", "arbitrary")),
        name="hgrn2_scan",
    )(hg, hg, hg, hg, hg, hg, lb, s0, consts["a"], consts["lvl"], consts["hm"])


def _scan_constants_dir(rev):
    t = np.arange(CHUNK)
    p = (CHUNK - 1 - t) if rev else t
    pt, pu = p[:, None], p[None, :]
    causal = (pu <= pt).astype(np.float32)
    head = np.arange(REC_W) // CHUNK
    hm = (head[:, None] == head[None, :]).astype(np.float32)
    rows = [causal, (pu > pt).astype(np.float32)]
    masks = []
    for lev in range(N_LEVELS):
        blk = CHUNK >> (lev + 1)
        bound = (pt // (2 * blk)) * (2 * blk) + blk - 1
        right = (pt % (2 * blk)) >= blk
        rows.append(np.where(right, (pu > bound) & (pu <= pt), (pu > pt) & (pu <= bound)).astype(np.float32))
        masks.append(((pt // (2 * blk) == pu // (2 * blk)) & right & ((pu % (2 * blk)) < blk)).astype(np.float32))
    masks.append(np.eye(CHUNK, dtype=np.float32))
    d = 1 if rev else 0
    ex = np.zeros((LANES, 2 * REC_W), np.float32)
    for h in range(DN_HEADS):
        ex[d * DN_HEADS + h, h * CHUNK:(h + 1) * CHUNK] = 1.0
        ex[2 * DN_HEADS + d * DN_HEADS + h, REC_W + h * CHUNK:REC_W + (h + 1) * CHUNK] = 1.0
    return {
        "l": causal, "cz": np.tile(causal, (1, DN_HEADS)), "hm": hm, "ex": ex,
        "a": np.concatenate(rows, axis=0),
        "lvl": np.concatenate([np.tile(mk, (1, DN_HEADS)) for mk in masks], axis=0),
    }


def _scan_constants():
    fwd, bwd = _scan_constants_dir(False), _scan_constants_dir(True)
    out = {key: jnp.asarray(np.stack([fwd[key], bwd[key]])) for key in ("l", "cz", "ex", "a", "lvl")}
    out["hm"] = jnp.asarray(fwd["hm"], BF16)
    return out


def _outproj_kernel(a_ref, dof_ref, dob_ref, z_ref, gof_ref, gob_ref, gg_ref, dnw_ref, hgw_ref, g_ref,
                    w_ref, x_ref, gate_ref, sh_ref, sc_ref, rt_ref, x1_ref, h2_ref, aff_ref):
    gmat = g_ref[...]

    def gated(o, z, nw):
        return o * lax.rsqrt(_mm_group(o * o, gmat) + EPS) * nw * _silu(z)

    dmix = gated(dof_ref[0] + dob_ref[0], z_ref[0], dnw_ref[...])
    gmix = gated(gof_ref[0] + gob_ref[0], gg_ref[0], hgw_ref[...])
    y = (jnp.dot(a_ref[0], w_ref[0:ATTN_Q_W, :], preferred_element_type=F32)
         + jnp.dot(dmix.astype(BF16), w_ref[ATTN_Q_W:ATTN_Q_W + REC_W, :], preferred_element_type=F32)
         + jnp.dot(gmix.astype(BF16), w_ref[ATTN_Q_W + REC_W:, :], preferred_element_type=F32))
    x1 = x_ref[0] + gate_ref[0] * y
    x1_ref[0] = x1
    ms = jnp.mean(x1 * x1, axis=-1, keepdims=True)
    h2 = x1 * lax.rsqrt(ms + EPS) * (1.0 + sc_ref[0]) + sh_ref[0]
    h2_ref[0] = h2.astype(BF16)
    logits = lax.dot_general(rt_ref[...], h2, (((1,), (1,)), ((), ())), precision=HIGHEST,
                             preferred_element_type=F32)
    e = jnp.exp(logits - jnp.max(logits, axis=0, keepdims=True))
    aff_ref[0] = e / jnp.sum(e, axis=0, keepdims=True)


def _output_projection(a, dof, dob, dn, gof, gob, hg, dnw, hgw, gavg, w_out, x, gate, shift, scale, router_t):
    b, t, d = x.shape
    tm = _tile(t, 512)
    n = REC_W
    per_b = gate.shape[0] > 1
    mod_map = (lambda bi, i: (bi, 0, 0)) if per_b else (lambda bi, i: (0, 0, 0))
    const = lambda bi, i: (0, 0)
    tok = lambda bi, i: (bi, i, 0)
    return pl.pallas_call(
        _outproj_kernel,
        grid=(b, t // tm),
        in_specs=[pl.BlockSpec((1, tm, ATTN_Q_W), tok),
                  pl.BlockSpec((1, tm, n), tok), pl.BlockSpec((1, tm, n), tok),
                  pl.BlockSpec((1, tm, n), lambda bi, i: (bi, i, 3)),
                  pl.BlockSpec((1, tm, n), tok), pl.BlockSpec((1, tm, n), tok),
                  pl.BlockSpec((1, tm, n), lambda bi, i: (bi, i, 4)),
                  pl.BlockSpec((1, n), const), pl.BlockSpec((1, n), const),
                  pl.BlockSpec((n, n), const),
                  pl.BlockSpec((d, d), const),
                  pl.BlockSpec((1, tm, d), tok),
                  pl.BlockSpec((1, 1, d), mod_map), pl.BlockSpec((1, 1, d), mod_map),
                  pl.BlockSpec((1, 1, d), mod_map),
                  pl.BlockSpec((N_EXPERTS, d), const)],
        out_specs=(pl.BlockSpec((1, tm, d), tok), pl.BlockSpec((1, tm, d), tok),
                   pl.BlockSpec((1, N_EXPERTS, tm), lambda bi, i: (bi, 0, i))),
        out_shape=(jax.ShapeDtypeStruct((b, t, d), F32), jax.ShapeDtypeStruct((b, t, d), BF16),
                   jax.ShapeDtypeStruct((b, N_EXPERTS, t), F32)),
        compiler_params=_cparams(("parallel", "parallel")),
        name="output_projection_router",
    )(a, dof, dob, dn, gof, gob, hg, dnw, hgw, gavg, w_out, x, gate, shift, scale, router_t)


def _select_kernel(aff_ref, u_ref, bs_ref, bst_ref, su_ref, slot_ref, base_ref, nsub_ref, inc_scr, *, cap, nblk):
    x = aff_ref[0]
    bits = pltpu.bitcast(x, jnp.int32)
    lo = jnp.zeros((N_EXPERTS, 1), jnp.int32)
    for bit in range(30, -1, -1):
        cand = lo | (1 << bit)
        cnt = jnp.sum((bits >= cand).astype(jnp.int32), axis=1, keepdims=True)
        lo = jnp.where(cnt >= cap, cand, lo)
    gt = bits > lo
    eq = bits == lo
    umat = u_ref[...]

    def prefix(mask):
        mb = mask.astype(BF16)
        for j in range(nblk):
            inc_scr[:, j * LANES:(j + 1) * LANES] = jnp.dot(mb[:, j * LANES:(j + 1) * LANES], umat,
                                                            preferred_element_type=F32)
        totals = jnp.dot(mb, bs_ref[...], preferred_element_type=F32)
        offs = _mm_hi(totals, su_ref[...])
        return inc_scr[...] + _mm_hi(offs, bst_ref[...]), offs, totals

    eqf = eq.astype(F32)
    n_gt = jnp.sum(gt.astype(F32), axis=1, keepdims=True)
    eq_before, _, _ = prefix(eqf)
    sel = gt | (eq & ((eq_before - eqf) < (cap - n_gt)))
    self_ = sel.astype(F32)
    pos, offs, totals = prefix(self_)
    slot_ref[0] = jnp.where(sel, pos - 1.0, -1.0).astype(jnp.int32)
    start = jnp.floor(offs * (1.0 / ROUTE_ALIGN)) * ROUTE_ALIGN
    span = offs + totals - start
    n_sub = functools.reduce(jnp.add, [jnp.where(span > j * ROUTE_SUB, 1.0, 0.0)
                                       for j in range(ROUTE_WIN // ROUTE_SUB)])
    base_ref[0] = start.astype(jnp.int32)
    nsub_ref[0] = jnp.max(n_sub, axis=0, keepdims=True).astype(jnp.int32)


def _moe_select(aff_t, cap):
    b, e, t = aff_t.shape
    nblk = t // ROUTE_BLK
    u = jnp.asarray(np.triu(np.ones((LANES, LANES), np.float32)), BF16)
    blk = np.arange(t) // ROUTE_BLK
    bs = (blk[:, None] == np.arange(nblk)[None, :]).astype(np.float32)
    su = np.triu(np.ones((nblk, nblk), np.float32), 1)
    const = lambda bi: (0, 0)
    return pl.pallas_call(
        functools.partial(_select_kernel, cap=cap, nblk=nblk),
        grid=(b,),
        in_specs=[pl.BlockSpec((1, e, t), lambda bi: (bi, 0, 0)),
                  pl.BlockSpec((LANES, LANES), const),
                  pl.BlockSpec((t, nblk), const),
                  pl.BlockSpec((nblk, t), const),
                  pl.BlockSpec((nblk, nblk), const)],
        out_specs=(pl.BlockSpec((1, e, t), lambda bi: (bi, 0, 0)),
                   pl.BlockSpec((1, e, nblk), lambda bi: (bi, 0, 0)),
                   pl.BlockSpec((1, 1, nblk), lambda bi: (bi, 0, 0))),
        out_shape=(jax.ShapeDtypeStruct((b, e, t), jnp.int32), jax.ShapeDtypeStruct((b, e, nblk), jnp.int32),
                   jax.ShapeDtypeStruct((b, 1, nblk), jnp.int32)),
        scratch_shapes=[pltpu.VMEM((e, t), F32)],
        compiler_params=_cparams(("parallel",)),
        name="moe_select",
    )(aff_t, u, jnp.asarray(bs, BF16), jnp.asarray(bs.T), jnp.asarray(su))


def _gather_kernel(start_sm, nsub_sm, slot_ref, h_ref, xe_ref, *, nblk):
    bi = pl.program_id(0)
    xe_ref[0] = jnp.zeros(xe_ref.shape[1:], BF16)
    row = lax.broadcasted_iota(jnp.int32, (ROUTE_SUB, ROUTE_BLK), 0)

    def body(k, carry):
        t0 = pl.multiple_of(k * ROUTE_BLK, ROUTE_BLK)
        hblk = h_ref[0, pl.ds(t0, ROUTE_BLK), :]
        slots = slot_ref[0, :, pl.ds(t0, ROUTE_BLK)]
        firsts = [start_sm[(bi * N_EXPERTS + e) * nblk + k] for e in range(N_EXPERTS)]

        def sub(j, inner):
            starts = [pl.multiple_of(first + j * ROUTE_SUB, ROUTE_ALIGN) for first in firsts]
            onehot = jnp.concatenate(
                [jnp.where(row == slots[e:e + 1, :] - starts[e], 1.0, 0.0).astype(BF16)
                 for e in range(N_EXPERTS)], axis=0)
            rows = jnp.dot(onehot, hblk, preferred_element_type=F32).astype(BF16)
            for e in range(N_EXPERTS):
                win = pl.ds(starts[e], ROUTE_SUB)
                xe_ref[0, e, win, :] = xe_ref[0, e, win, :] + rows[e * ROUTE_SUB:(e + 1) * ROUTE_SUB]
            return inner

        lax.fori_loop(0, nsub_sm[bi * nblk + k], sub, 0)
        return carry

    lax.fori_loop(0, nblk, body, 0)


def _moe_gather(h2, slot, start_flat, nsub_flat, cap):
    b, t, d = h2.shape
    nblk = t // ROUTE_BLK
    cp = cap + ROUTE_WIN
    dq = _tile(d, 256)
    grid_spec = pltpu.PrefetchScalarGridSpec(
        num_scalar_prefetch=2,
        grid=(b, d // dq),
        in_specs=[pl.BlockSpec((1, N_EXPERTS, t), lambda bi, j, s0, s1: (bi, 0, 0)),
                  pl.BlockSpec((1, t, dq), lambda bi, j, s0, s1: (bi, 0, j))],
        out_specs=pl.BlockSpec((1, N_EXPERTS, cp, dq), lambda bi, j, s0, s1: (bi, 0, 0, j)),
    )
    return pl.pallas_call(
        functools.partial(_gather_kernel, nblk=nblk),
        grid_spec=grid_spec,
        out_shape=jax.ShapeDtypeStruct((b, N_EXPERTS, cp, d), BF16),
        compiler_params=_cparams(("parallel", "parallel")),
        name="moe_gather",
    )(start_flat, nsub_flat, slot, h2)


def _ffn_kernel(x_ref, wg_ref, wu_ref, wd_ref, y_ref, *, cap):
    x = x_ref[0, 0, 0:cap, :]
    a = jnp.dot(x, wg_ref[0], preferred_element_type=F32)
    u = jnp.dot(x, wu_ref[0], preferred_element_type=F32)
    y_ref[0, 0, 0:cap, :] = jnp.dot((_silu(a) * u).astype(BF16), wd_ref[0],
                                    preferred_element_type=F32).astype(BF16)
    y_ref[0, 0, cap:, :] = jnp.zeros((y_ref.shape[2] - cap, y_ref.shape[3]), BF16)


def _moe_ffn(xe, wg, wu, wd, cap):
    b, e, cp, d = xe.shape
    f = wg.shape[-1]
    return pl.pallas_call(
        functools.partial(_ffn_kernel, cap=cap),
        grid=(e, b),
        in_specs=[pl.BlockSpec((1, 1, cp, d), lambda ei, bi: (bi, ei, 0, 0)),
                  pl.BlockSpec((1, d, f), lambda ei, bi: (ei, 0, 0)),
                  pl.BlockSpec((1, d, f), lambda ei, bi: (ei, 0, 0)),
                  pl.BlockSpec((1, f, d), lambda ei, bi: (ei, 0, 0))],
        out_specs=pl.BlockSpec((1, 1, cp, d), lambda ei, bi: (bi, ei, 0, 0)),
        out_shape=jax.ShapeDtypeStruct((b, e, cp, d), BF16),
        compiler_params=_cparams(("parallel", "parallel")),
        name="moe_ffn",
    )(xe, wg, wu, wd)


def _combine_kernel(start_sm, nsub_sm, ye_ref, x_ref, gate_ref, slot_ref, aff_ref, o_ref, *, nblk, per_step):
    bi = pl.program_id(0)
    row = lax.broadcasted_iota(jnp.int32, (ROUTE_SUB, ROUTE_BLK), 0)
    for c in range(per_step):
        k = pl.program_id(2) * per_step + c
        tok = slice(c * ROUTE_BLK, (c + 1) * ROUTE_BLK)
        slots = slot_ref[0, :, tok]
        aff = aff_ref[0, :, tok]
        firsts = [start_sm[(bi * N_EXPERTS + e) * nblk + k] for e in range(N_EXPERTS)]

        def sub(j, acc, slots=slots, aff=aff, firsts=firsts):
            starts = [pl.multiple_of(first + j * ROUTE_SUB, ROUTE_ALIGN) for first in firsts]
            wsel = jnp.concatenate([jnp.where(row == slots[e:e + 1, :] - starts[e], aff[e:e + 1, :], 0.0)
                                    for e in range(N_EXPERTS)], axis=0)
            ys = jnp.concatenate([ye_ref[0, e, pl.ds(starts[e], ROUTE_SUB), :] for e in range(N_EXPERTS)],
                                 axis=0)
            tn = (((0,), (0,)), ((), ()))
            return acc + lax.dot_general(wsel.astype(BF16), ys, tn, preferred_element_type=F32)

        acc = lax.fori_loop(0, nsub_sm[bi * nblk + k], sub, jnp.zeros((ROUTE_BLK, x_ref.shape[2]), F32))
        o_ref[0, tok, :] = x_ref[0, tok, :] + gate_ref[0] * acc


def _moe_combine(ye, x1, gate, slot, aff_t, start_flat, nsub_flat):
    b, t, d = x1.shape
    cp = ye.shape[2]
    nblk = t // ROUTE_BLK
    dh = d // 2
    per_b = gate.shape[0] > 1
    per_step = _tile(nblk, COMBINE_BLOCKS)
    tok = per_step * ROUTE_BLK
    grid_spec = pltpu.PrefetchScalarGridSpec(
        num_scalar_prefetch=2,
        grid=(b, 2, nblk // per_step),
        in_specs=[pl.BlockSpec((1, N_EXPERTS, cp, dh), lambda bi, j, k, s0, s1: (bi, 0, 0, j)),
                  pl.BlockSpec((1, tok, dh), lambda bi, j, k, s0, s1: (bi, k, j)),
                  pl.BlockSpec((1, 1, dh), (lambda bi, j, k, s0, s1: (bi, 0, j)) if per_b
                               else (lambda bi, j, k, s0, s1: (0, 0, j))),
                  pl.BlockSpec((1, N_EXPERTS, tok), lambda bi, j, k, s0, s1: (bi, 0, k)),
                  pl.BlockSpec((1, N_EXPERTS, tok), lambda bi, j, k, s0, s1: (bi, 0, k))],
        out_specs=pl.BlockSpec((1, tok, dh), lambda bi, j, k, s0, s1: (bi, k, j)),
    )
    return pl.pallas_call(
        functools.partial(_combine_kernel, nblk=nblk, per_step=per_step),
        grid_spec=grid_spec,
        out_shape=jax.ShapeDtypeStruct((b, t, d), F32),
        compiler_params=_cparams(("parallel", "parallel", "arbitrary")),
        name="moe_combine",
    )(start_flat, nsub_flat, ye, x1, gate, slot, aff_t)


def _expert_choice_ffn(x1, h2, aff_t, gate, wg, wu, wd):
    b, t, _ = x1.shape
    cap = CAPACITY_FACTOR * t // N_EXPERTS
    slot, start, nsub = _moe_select(aff_t, cap)
    start_flat, nsub_flat = start.reshape(-1), nsub.reshape(-1)
    xe = _moe_gather(h2, slot, start_flat, nsub_flat, cap)
    ye = _moe_ffn(xe, wg, wu, wd, cap)
    return _moe_combine(ye, x1, gate, slot, aff_t, start_flat, nsub_flat)


def _rope_tables(n_tokens):
    rows = n_tokens // GRID_W
    row = jnp.repeat(jnp.arange(rows, dtype=F32), GRID_W)
    col = jnp.tile(jnp.arange(GRID_W, dtype=F32), rows)
    n_freq = HEAD_DIM // 4
    inv_freq = ROPE_THETA ** (-jnp.arange(n_freq, dtype=F32) / n_freq)
    ang_r = row[:, None] * inv_freq
    ang_c = col[:, None] * inv_freq
    cos = jnp.concatenate([jnp.cos(ang_r)] * 2 + [jnp.cos(ang_c)] * 2, axis=-1)
    sin = jnp.concatenate([jnp.sin(ang_r)] * 2 + [jnp.sin(ang_c)] * 2, axis=-1)
    return jnp.tile(cos, (1, LANES // HEAD_DIM)), jnp.tile(sin, (1, LANES // HEAD_DIM))


def _block_diag(width, block, value):
    idx = np.arange(width) // block
    return jnp.asarray((idx[:, None] == idx[None, :]).astype(np.float32) * value)


def kernel(x, c, ctx, c_ctx, w_mod, b_mod, w_in, w_out, attn_q_norm, attn_k_norm, dn_conv, dn_a_log,
           dn_dt_bias, dn_norm, hg_lower_bounds, hg_norm, moe_router, moe_w_gate, moe_w_up, moe_w_down):
    depth = w_mod.shape[0]
    b, t_lat, d = x.shape
    cos, sin = _rope_tables(t_lat)
    g_head = _block_diag(LANES, HEAD_DIM, 1.0 / HEAD_DIM)
    g_mean = _block_diag(REC_W, HEAD_DIM, 1.0 / HEAD_DIM)
    g_sum = _block_diag(REC_W, HEAD_DIM, 1.0)
    consts = _scan_constants()
    s_zero = jnp.zeros((b, 2, REC_W, REC_W), F32)

    lb_w = jax.nn.softmax(hg_lower_bounds.astype(F32), axis=0)
    hg_lb = jnp.cumsum(lb_w, axis=0) - lb_w[0]

    rows = ((b + 1 + 7) // 8) * 8
    cond = jnp.zeros((rows, d), F32).at[:b].set(c).at[b].set(c_ctx)
    mod = _modulation(cond, w_mod, b_mod)

    n_small = 4 * DN_HEADS
    w_in_r = jnp.concatenate(
        [w_in[:, :, :C_HG], w_in[:, :, C_HG + n_small:], w_in[:, :, C_HG:C_HG + n_small],
         jnp.zeros((depth, d, C_END - C_BA - n_small), w_in.dtype)], axis=-1).astype(BF16)
    w_out_b = _cast_bf16(w_out)
    wg_b, wu_b, wd_b = _cast_bf16(moe_w_gate), _cast_bf16(moe_w_up), _cast_bf16(moe_w_down)

    x_lat, x_ctx = x, ctx
    for l in range(depth):
        ctx_out = l < depth - 1
        m_lat = [mod[l, :b, j * d:(j + 1) * d][:, None, :] for j in range(6)]
        m_ctx = [mod[l, b:b + 1, j * d:(j + 1) * d][:, None, :] for j in range(6)]
        qn = jnp.tile(attn_q_norm[l], LANES // HEAD_DIM)[None, :]
        kn = jnp.tile(attn_k_norm[l], LANES // HEAD_DIM)[None, :]
        conv_w = jnp.zeros((8, 3 * REC_W), F32).at[:CONV_K].set(dn_conv[l])
        pad = jnp.zeros((LANES - 4 * DN_HEADS,), F32)
        neg_a = jnp.concatenate([jnp.zeros((2 * DN_HEADS,), F32), -jnp.exp(dn_a_log[l].reshape(-1)), pad])[None, :]
        dt_b = jnp.concatenate([jnp.zeros((2 * DN_HEADS,), F32), dn_dt_bias[l].reshape(-1), pad])[None, :]
        dnw = jnp.tile(dn_norm[l], DN_HEADS)[None, :]
        hgw = jnp.tile(hg_norm[l], HG_HEADS)[None, :]
        router_t = moe_router[l].T

        streams = {}
        for name, xs, ms, rotate in (("ctx", x_ctx, m_ctx, False), ("lat", x_lat, m_lat, True)):
            t = xs.shape[1]
            q, kt, v, dn, hg, ba = _input_projection(xs, ms[0], ms[1], w_in_r[l], qn, kn,
                                                     cos[:t], sin[:t], g_head, rotate)
            qkv, gb = _dn_prep(dn, ba, conv_w, neg_a, dt_b, g_sum)
            streams[name] = dict(q=q, kv=(kt, v), dn=dn, hg=hg, qkv=qkv, gb=gb)

        sc, sl = streams["ctx"], streams["lat"]
        dcf, dcb, dn_state = _dn_scan(sc["qkv"], sc["gb"], s_zero, consts)
        dlf, dlb, _ = _dn_scan(sl["qkv"], sl["gb"], dn_state, consts)
        gcf, gcb, hg_state = _hg_scan(sc["hg"], hg_lb[l], s_zero, consts)
        glf, glb, _ = _hg_scan(sl["hg"], hg_lb[l], hg_state, consts)

        a_lat = _attention(sl["q"], [sc["kv"], sl["kv"]], t_lat)
        x1, h2, aff_t = _output_projection(
            a_lat, dlf, dlb, sl["dn"], glf, glb, sl["hg"], dnw, hgw, g_mean, w_out_b[l], x_lat,
            m_lat[2], m_lat[3], m_lat[4], router_t)
        x_lat = _expert_choice_ffn(x1, h2, aff_t, m_lat[5], wg_b[l], wu_b[l], wd_b[l])
        if ctx_out:
            a_ctx = _attention(sc["q"], [sc["kv"]], x_ctx.shape[1])
            x1, h2, aff_t = _output_projection(
                a_ctx, dcf, dcb, sc["dn"], gcf, gcb, sc["hg"], dnw, hgw, g_mean, w_out_b[l], x_ctx,
                m_ctx[2], m_ctx[3], m_ctx[4], router_t)
            x_ctx = _expert_choice_ffn(x1, h2, aff_t, m_ctx[5], wg_b[l], wu_b[l], wd_b[l])
    return x_lat
```

```python
import functools

import numpy as np
import jax
import jax.numpy as jnp
from jax import lax
from jax.experimental import pallas as pl
from jax.experimental.pallas import tpu as pltpu

F32 = jnp.float32
BF16 = jnp.bfloat16
HIGHEST = lax.Precision.HIGHEST

HEAD_DIM = 64
N_Q_HEADS = 8
N_KV_HEADS = 2
GQA_GROUP = N_Q_HEADS // N_KV_HEADS
DN_HEADS = 4
HG_HEADS = 4
GRID_W = 64
ROPE_THETA = 10000.0
CONV_K = 5
CHUNK = 64
N_EXPERTS = 16
CAPACITY_FACTOR = 2
EPS = 1e-6
LOG2E = 1.4426950408889634
ATTN_KV_GROUP = 1408
V_ROWS = HEAD_DIM + 16

ATTN_Q_W = N_Q_HEADS * HEAD_DIM
ATTN_KV_W = N_KV_HEADS * HEAD_DIM
REC_W = DN_HEADS * HEAD_DIM
LANES = 128
ROUTE_BLK = LANES
ROUTE_ALIGN = 16
ROUTE_WIN = ROUTE_BLK + ROUTE_ALIGN
COMBINE_BLOCKS = 4
GATHER_BLOCKS = 4
ROUTE_SUB = ROUTE_WIN // 3
N_LEVELS = 6
SCAN_BATCH = 4
VMEM_LIMIT = 56 * 1024 * 1024

C_Q, C_K, C_V, C_DN, C_HG, C_BA, C_END = 0, 512, 640, 768, 1792, 3072, 3200


def _mm(a, b):
    return jnp.dot(a.astype(BF16), b.astype(BF16), preferred_element_type=F32)


def _mm_nt(a, b):
    return lax.dot_general(a.astype(BF16), b.astype(BF16), (((1,), (1,)), ((), ())),
                           preferred_element_type=F32)


def _mm_tn(a, b):
    return lax.dot_general(a.astype(BF16), b.astype(BF16), (((0,), (0,)), ((), ())),
                           preferred_element_type=F32)


def _mm_hi(a, b):
    return jnp.dot(a, b, precision=HIGHEST, preferred_element_type=F32)


def _mm_group(a, gmat):
    hi = a.astype(BF16)
    lo = (a - hi.astype(F32)).astype(BF16)
    t = jnp.dot(jnp.concatenate([hi, lo], axis=0), gmat.astype(BF16), preferred_element_type=F32)
    return t[:a.shape[0]] + t[a.shape[0]:]


def _sigmoid(x):
    return 1.0 / (1.0 + jnp.exp(-x))


def _silu(x):
    return x * _sigmoid(x)


def _cparams(sem):
    return pltpu.CompilerParams(dimension_semantics=sem, vmem_limit_bytes=VMEM_LIMIT)


def _tile(n, pref):
    return pref if n % pref == 0 else n


def _cast_kernel(x_ref, o_ref):
    o_ref[...] = x_ref[...].astype(BF16)


def _cast_bf16(w):
    r, c = w.shape[-2:]
    w3 = w.reshape(-1, r, c)
    rb = _tile(r, 512)
    out = pl.pallas_call(
        _cast_kernel,
        grid=(w3.shape[0], r // rb),
        in_specs=[pl.BlockSpec((1, rb, c), lambda i, j: (i, j, 0))],
        out_specs=pl.BlockSpec((1, rb, c), lambda i, j: (i, j, 0)),
        out_shape=jax.ShapeDtypeStruct(w3.shape, BF16),
        compiler_params=_cparams(("parallel", "parallel")),
        name="cast_bf16",
    )(w3)
    return out.reshape(w.shape)


def _mod_kernel(c_ref, w_ref, b_ref, o_ref):
    o_ref[0] = _mm_hi(_silu(c_ref[...]), w_ref[0]) + b_ref[0]


def _modulation(cond, w_mod, b_mod):
    depth, d, n = w_mod.shape
    rows = cond.shape[0]
    tn = _tile(n, 1536)
    return pl.pallas_call(
        _mod_kernel,
        grid=(depth, n // tn),
        in_specs=[pl.BlockSpec((rows, d), lambda l, j: (0, 0)),
                  pl.BlockSpec((1, d, tn), lambda l, j: (l, 0, j)),
                  pl.BlockSpec((1, 1, tn), lambda l, j: (l, 0, j))],
        out_specs=pl.BlockSpec((1, rows, tn), lambda l, j: (l, 0, j)),
        out_shape=jax.ShapeDtypeStruct((depth, rows, n), F32),
        compiler_params=_cparams(("parallel", "parallel")),
        name="modulation",
    )(cond, w_mod, b_mod.reshape(depth, 1, n))


def _inproj_kernel(x_ref, sh_ref, sc_ref, w_ref, qn_ref, kn_ref, cos_ref, sin_ref, g_ref,
                   q_ref, k_ref, v_ref, dn_ref, hg_ref, ba_ref, *, rotate):
    x = x_ref[0]
    ms = jnp.mean(x * x, axis=-1, keepdims=True)
    h = x * lax.rsqrt(ms + EPS) * (1.0 + sc_ref[0]) + sh_ref[0]
    hb = h.astype(BF16)
    gmat = g_ref[...]

    def proj(lo, hi):
        return jnp.dot(hb, w_ref[:, lo:hi], preferred_element_type=F32)

    def head_norm_rope(t, nw):
        t = t * lax.rsqrt(_mm_group(t * t, gmat) + EPS) * nw
        if rotate:
            lane = lax.broadcasted_iota(jnp.int32, t.shape, 1)
            first = (lane % 32) < 16
            rot = jnp.where(first, -pltpu.roll(t, LANES - 16, 1), pltpu.roll(t, 16, 1))
            t = t * cos_ref[...] + rot * sin_ref[...]
        return t

    def q_group(j):
        t = proj(C_Q + j * LANES, C_Q + (j + 1) * LANES)
        yield
        qj = head_norm_rope(t, qn_ref[...])
        qt = (qj * (HEAD_DIM ** -0.5 * LOG2E)).T.astype(BF16)
        q_ref[0, 2 * j] = qt[:HEAD_DIM]
        q_ref[0, 2 * j + 1] = qt[HEAD_DIM:]

    def k_group():
        t = proj(C_K, C_V)
        yield
        k = head_norm_rope(t, kn_ref[...]).astype(BF16)
        k_ref[0, 0] = k[:, :HEAD_DIM]
        k_ref[0, 1] = k[:, HEAD_DIM:]

    def v_group():
        v = proj(C_V, C_DN)
        yield
        vt = v.T
        ones = jnp.ones((V_ROWS - HEAD_DIM, vt.shape[1]), F32)
        v_ref[0, 0] = jnp.concatenate([vt[:HEAD_DIM], ones], axis=0).astype(BF16)
        v_ref[0, 1] = jnp.concatenate([vt[HEAD_DIM:], ones], axis=0).astype(BF16)

    def plain_group(o_ref, base, lo, hi):
        t = proj(base + lo, base + hi)
        yield
        o_ref[0, :, lo:hi] = t

    step = 4 * LANES
    plain = [plain_group(o_ref, base, lo, min(lo + step, width))
             for o_ref, base, width in ((dn_ref, C_DN, C_HG - C_DN), (hg_ref, C_HG, C_BA - C_HG),
                                        (ba_ref, C_BA, C_END - C_BA))
             for lo in range(0, width, step)]
    _lockstep([q_group(j) for j in range(ATTN_Q_W // LANES)] + [k_group(), v_group()] + plain, stagger=True)


def _input_projection(x, shift, scale, w, qn, kn, cos, sin, gmat, rotate):
    b, t, d = x.shape
    tm = _tile(t, 512)
    per_b = shift.shape[0] > 1
    mod_map = (lambda bi, i: (bi, 0, 0)) if per_b else (lambda bi, i: (0, 0, 0))
    const = lambda bi, i: (0, 0)
    out_shape = (
        jax.ShapeDtypeStruct((b, N_Q_HEADS, HEAD_DIM, t), BF16),
        jax.ShapeDtypeStruct((b, N_KV_HEADS, t, HEAD_DIM), BF16),
        jax.ShapeDtypeStruct((b, N_KV_HEADS, V_ROWS, t), BF16),
        jax.ShapeDtypeStruct((b, t, C_HG - C_DN), F32),
        jax.ShapeDtypeStruct((b, t, C_BA - C_HG), F32),
        jax.ShapeDtypeStruct((b, t, C_END - C_BA), F32),
    )
    return pl.pallas_call(
        functools.partial(_inproj_kernel, rotate=rotate),
        grid=(b, t // tm),
        in_specs=[pl.BlockSpec((1, tm, d), lambda bi, i: (bi, i, 0)),
                  pl.BlockSpec((1, 1, d), mod_map),
                  pl.BlockSpec((1, 1, d), mod_map),
                  pl.BlockSpec((d, C_END), const),
                  pl.BlockSpec((1, LANES), const),
                  pl.BlockSpec((1, LANES), const),
                  pl.BlockSpec((tm, LANES), lambda bi, i: (i, 0)),
                  pl.BlockSpec((tm, LANES), lambda bi, i: (i, 0)),
                  pl.BlockSpec((LANES, LANES), const)],
        out_specs=(pl.BlockSpec((1, N_Q_HEADS, HEAD_DIM, tm), lambda bi, i: (bi, 0, 0, i)),
                   pl.BlockSpec((1, N_KV_HEADS, tm, HEAD_DIM), lambda bi, i: (bi, 0, i, 0)),
                   pl.BlockSpec((1, N_KV_HEADS, V_ROWS, tm), lambda bi, i: (bi, 0, 0, i)),
                   pl.BlockSpec((1, tm, C_HG - C_DN), lambda bi, i: (bi, i, 0)),
                   pl.BlockSpec((1, tm, C_BA - C_HG), lambda bi, i: (bi, i, 0)),
                   pl.BlockSpec((1, tm, C_END - C_BA), lambda bi, i: (bi, i, 0))),
        out_shape=out_shape,
        compiler_params=_cparams(("parallel", "parallel")),
        name="input_projection",
    )(x, shift, scale, w, qn, kn, cos, sin, gmat)


def _lockstep(gens, stagger=False):
    results = [None] * len(gens)
    live, started = [], 0
    while live or started < len(gens):
        fresh = 1 if stagger else len(gens)
        live += list(range(started, min(started + fresh, len(gens))))
        started = min(started + fresh, len(gens))
        for i in reversed(list(live)):
            try:
                next(gens[i])
            except StopIteration as stop:
                results[i] = stop.value
                live.remove(i)
    return results


def _attn_part(qt, ks, vts):
    ss = [jnp.dot(k, qt, preferred_element_type=F32).astype(BF16) for k in ks]
    yield
    m = functools.reduce(jnp.maximum, [jnp.max(s, axis=0, keepdims=True) for s in ss])
    ps = [jnp.exp2(s - m) for s in ss]
    yield
    o = functools.reduce(jnp.add, [jnp.dot(vt, p, preferred_element_type=F32) for p, vt in zip(ps, vts)])
    return m.astype(F32), o


def _attn_kernel(*refs, n_src):
    q_ref, o_ref = refs[0], refs[-1]
    groups, cur, room = [], ([], []), ATTN_KV_GROUP
    for i in range(n_src):
        k_ref, vt_ref = refs[1 + 2 * i], refs[2 + 2 * i]
        lo, tk = 0, k_ref.shape[2]
        while lo < tk:
            n = min(room, tk - lo)
            cur[0].append(k_ref[0, 0, lo:lo + n, :])
            cur[1].append(vt_ref[0, 0, :, lo:lo + n])
            lo, room = lo + n, room - n
            if room == 0:
                groups.append(cur)
                cur, room = ([], []), ATTN_KV_GROUP
    if cur[0]:
        groups.append(cur)
    chains = [(h, g) for h in range(GQA_GROUP) for g in range(len(groups))]
    parts = _lockstep([_attn_part(q_ref[0, h], *groups[g]) for h, g in chains], stagger=True)
    outs = []
    for h in range(GQA_GROUP):
        mine = [parts[i] for i, (hh, _) in enumerate(chains) if hh == h]
        m = functools.reduce(jnp.maximum, [pm for pm, _ in mine])
        o = functools.reduce(jnp.add, [po * jnp.exp2(pm - m) for pm, po in mine])
        outs.append(o[:HEAD_DIM] / o[HEAD_DIM:HEAD_DIM + 1])
    o_ref[0] = jnp.concatenate(outs, axis=0).T.astype(BF16)


def _attention(qt, sources, t):
    b = qt.shape[0]
    tq = _tile(t, 256)
    gw = GQA_GROUP * HEAD_DIM
    in_specs = [pl.BlockSpec((1, GQA_GROUP, HEAD_DIM, tq), lambda bi, g, i: (bi, g, 0, i))]
    args = [qt]
    for k, vt in sources:
        tk = k.shape[2]
        in_specs.append(pl.BlockSpec((1, 1, tk, HEAD_DIM), lambda bi, g, i: (bi, g, 0, 0)))
        in_specs.append(pl.BlockSpec((1, 1, V_ROWS, tk), lambda bi, g, i: (bi, g, 0, 0)))
        args += [k, vt]
    return pl.pallas_call(
        functools.partial(_attn_kernel, n_src=len(sources)),
        grid=(b, N_KV_HEADS, t // tq),
        in_specs=in_specs,
        out_specs=pl.BlockSpec((1, tq, gw), lambda bi, g, i: (bi, i, g)),
        out_shape=jax.ShapeDtypeStruct((b, t, ATTN_Q_W), BF16),
        compiler_params=_cparams(("parallel", "parallel", "parallel")),
        name="attention",
    )(*args)


def _dn_prep_kernel(x_ref, xp_ref, xn_ref, ba_ref, cw_ref, na_ref, dtb_ref, g_ref, o_ref, gb_ref, buf):
    i = pl.program_id(1)
    n = pl.num_programs(1)
    tm = x_ref.shape[1]
    buf[0:8] = jnp.where(i > 0, xp_ref[0], 0.0)
    buf[8:8 + tm] = x_ref[0]
    buf[8 + tm:16 + tm] = jnp.where(i < n - 1, xn_ref[0], 0.0)
    half = CONV_K // 2
    y = cw_ref[0:1, :] * buf[8 - half:8 - half + tm]
    for j in range(1, CONV_K):
        y = y + cw_ref[j:j + 1, :] * buf[8 - half + j:8 - half + j + tm]
    y = _silu(y)
    gmat = g_ref[...]
    q = y[:, :REC_W]
    k = y[:, REC_W:2 * REC_W]
    o_ref[0, :, 0:REC_W] = q * lax.rsqrt(_mm_group(q * q, gmat) + EPS) * HEAD_DIM ** -0.5
    o_ref[0, :, REC_W:2 * REC_W] = k * lax.rsqrt(_mm_group(k * k, gmat) + EPS)
    o_ref[0, :, 2 * REC_W:] = y[:, 2 * REC_W:]
    ba = ba_ref[0]
    z = ba + dtb_ref[...]
    softplus = jnp.maximum(z, 0.0) + jnp.log1p(jnp.exp(-jnp.abs(z)))
    lane = lax.broadcasted_iota(jnp.int32, ba.shape, 1)
    gb_ref[0] = jnp.where(lane < 2 * DN_HEADS, _sigmoid(ba), na_ref[...] * softplus)


def _dn_prep(dn, ba, conv_w, neg_a, dt_bias, gsum):
    b, t, _ = dn.shape
    tm = _tile(t, 512)
    w3 = 3 * REC_W
    nb8 = t // 8
    return pl.pallas_call(
        _dn_prep_kernel,
        grid=(b, t // tm),
        in_specs=[pl.BlockSpec((1, tm, w3), lambda bi, i: (bi, i, 0)),
                  pl.BlockSpec((1, 8, w3), lambda bi, i: (bi, jnp.maximum(i * (tm // 8) - 1, 0), 0)),
                  pl.BlockSpec((1, 8, w3), lambda bi, i: (bi, jnp.minimum((i + 1) * (tm // 8), nb8 - 1), 0)),
                  pl.BlockSpec((1, tm, LANES), lambda bi, i: (bi, i, 0)),
                  pl.BlockSpec((8, w3), lambda bi, i: (0, 0)),
                  pl.BlockSpec((1, LANES), lambda bi, i: (0, 0)),
                  pl.BlockSpec((1, LANES), lambda bi, i: (0, 0)),
                  pl.BlockSpec((REC_W, REC_W), lambda bi, i: (0, 0))],
        out_specs=(pl.BlockSpec((1, tm, w3), lambda bi, i: (bi, i, 0)),
                   pl.BlockSpec((1, tm, LANES), lambda bi, i: (bi, i, 0))),
        out_shape=(jax.ShapeDtypeStruct((b, t, w3), F32), jax.ShapeDtypeStruct((b, t, LANES), F32)),
        scratch_shapes=[pltpu.VMEM((tm + 16, w3), F32)],
        compiler_params=_cparams(("parallel", "parallel")),
        name="deltanet_prep",
    )(dn, dn, dn, ba, conv_w, neg_a, dt_bias, gsum)


def _split2(a):
    hi = a.astype(BF16)
    return hi, (a - hi.astype(F32)).astype(BF16)


def _split3(a):
    hi = a.astype(BF16)
    r = a - hi.astype(F32)
    mid = r.astype(BF16)
    return hi, mid, (r - mid.astype(F32)).astype(BF16)


def _bdiag(a, hm):
    return jnp.concatenate([a] * DN_HEADS, axis=0) * hm


def _mm_bd(a, b, hm):
    return jnp.dot(a.astype(BF16), _bdiag(b.astype(BF16), hm), preferred_element_type=F32)


def _mm_sel_l(sel, b, terms=3):
    n = b.shape[1]
    parts = _split3(b) if terms == 3 else _split2(b)
    t = jnp.dot(sel.astype(BF16), jnp.concatenate(parts, axis=1), preferred_element_type=F32)
    return functools.reduce(jnp.add, [t[:, i * n:(i + 1) * n] for i in range(terms)])


def _mm_sel_r(a, sel):
    m = a.shape[0]
    t = jnp.dot(jnp.concatenate(_split3(a), axis=0), sel.astype(BF16), preferred_element_type=F32)
    return t[:m] + t[m:2 * m] + t[2 * m:]


def _dn_chunk(x, gb, st, ex, lmat, causal, lvl_ref, d, hm):
    n = REC_W
    lvl = lambda j: lvl_ref[d, j * CHUNK:(j + 1) * CHUNK, :]
    eye = lvl(N_LEVELS)
    ones8 = jnp.ones((8, CHUNK), F32)
    q, k, v = x[:, :n], x[:, n:2 * n], x[:, 2 * n:]
    gbx = _mm_sel_r(gb, ex)
    yield
    beta, g = gbx[:, :n], gbx[:, n:]
    cum = _mm_sel_l(lmat, g)
    last = 0 if d else CHUNK - 1
    clast = cum[last:last + 1]
    kb = k * beta
    ks = _bdiag(k.astype(BF16), hm)
    kk = lax.dot_general(kb.astype(BF16), ks, (((1,), (1,)), ((), ())), preferred_element_type=F32)
    qk = lax.dot_general(q.astype(BF16), ks, (((1,), (1,)), ((), ())), preferred_element_type=F32)
    yield
    cum_s = _mm_sel_l(ones8, cum * eye)[0:1]
    ecum = jnp.exp(cum)
    kdec = k * jnp.exp(clast - cum)
    yield
    decay = jnp.where(causal, jnp.exp(jnp.where(causal, cum - cum_s, 0.0)), 0.0)
    m = kk * decay
    attn = qk * decay
    inv = eye - lvl(N_LEVELS - 1) * m
    for lev in range(N_LEVELS - 2, -1, -1):
        half = _mm_bd(inv, lvl(lev) * m, hm)
        yield
        inv = inv - _mm_bd(half, inv, hm)
        yield
    rhs = v * beta - _mm_nt(kb * ecum, st)
    qs = _mm_nt(q * ecum, st)
    yield
    v_new = _mm_bd(inv, rhs, hm)
    yield
    out = qs + _mm_bd(attn, v_new, hm)
    st_new = st * jnp.exp(clast) + _mm_tn(v_new, kdec) * hm.astype(F32)
    return out, st_new


def _hg_chunk(qr, fr, v, lb, st, amat, lvl_ref, d, hm):
    lvl = lambda j: lvl_ref[d, j * CHUNK:(j + 1) * CHUNK, :]
    last = 0 if d else CHUNK - 1
    q = _silu(qr)
    f = lb + (1.0 - lb) * _sigmoid(fr)
    k = 1.0 - f
    g = jnp.log(f)
    ex = jnp.exp(_mm_sel_l(amat, g, terms=2))
    yield

    def level(xl, mask):
        kl = _bdiag((k * xl).astype(BF16), hm)
        return lax.dot_general((q * xl).astype(BF16), kl, (((1,), (1,)), ((), ())),
                               preferred_element_type=F32) * mask

    attn = level(1.0, lvl(N_LEVELS))
    for lev in range(N_LEVELS):
        attn = attn + level(ex[(2 + lev) * CHUNK:(3 + lev) * CHUNK], lvl(lev))
    qs = _mm_nt(q * ex[0:CHUNK], st)
    vk = _mm_tn(v, k * ex[CHUNK:2 * CHUNK])
    yield
    out = _mm_bd(attn, v, hm) + qs
    st_new = st * ex[last:last + 1] + vk * hm.astype(F32)
    return out, st_new


def _dn_scan_kernel(xf_ref, xb_ref, gf_ref, gb_ref, s0_ref, ex_ref, l_ref, cz_ref, lvl_ref, hm_ref,
                    of_ref, ob_ref, sf_ref, s_scr, *, nchunk):
    @pl.when(pl.program_id(1) == 0)
    def _():
        s_scr[...] = s0_ref[...]

    hm = hm_ref[...]
    refs = ((xf_ref, gf_ref, of_ref), (xb_ref, gb_ref, ob_ref))

    def body(c, carry):
        rows = [pl.ds(pl.multiple_of(cc * CHUNK, CHUNK), CHUNK) for cc in (c, nchunk - 1 - c)]
        insts = [(bi, d) for bi in range(s_scr.shape[0]) for d in range(2)]
        res = _lockstep([_dn_chunk(refs[d][0][bi, rows[d], :], refs[d][1][bi, rows[d], :], s_scr[bi, d],
                                   ex_ref[d], l_ref[d], cz_ref[d] > 0, lvl_ref, d, hm) for bi, d in insts])
        for (bi, d), (out, s_new) in zip(insts, res):
            refs[d][2][bi, rows[d], :] = out
            s_scr[bi, d] = s_new
        return carry

    lax.fori_loop(0, nchunk, body, 0)
    sf_ref[...] = s_scr[...]


def _hg_scan_kernel(qf_ref, qb_ref, ff_ref, fb_ref, if_ref, ib_ref, lb_ref, s0_ref, a_ref, lvl_ref, hm_ref,
                    of_ref, ob_ref, sf_ref, s_scr, *, nchunk):
    @pl.when(pl.program_id(1) == 0)
    def _():
        s_scr[...] = s0_ref[...]

    hm = hm_ref[...]
    refs = ((qf_ref, ff_ref, if_ref, of_ref), (qb_ref, fb_ref, ib_ref, ob_ref))

    def body(c, carry):
        rows = [pl.ds(pl.multiple_of(cc * CHUNK, CHUNK), CHUNK) for cc in (c, nchunk - 1 - c)]
        insts = [(bi, d) for bi in range(s_scr.shape[0]) for d in range(2)]
        res = _lockstep([_hg_chunk(refs[d][0][bi, rows[d], :], refs[d][1][bi, rows[d], :],
                                   refs[d][2][bi, rows[d], :], lb_ref[d:d + 1, :], s_scr[bi, d], a_ref[d],
                                   lvl_ref, d, hm) for bi, d in insts])
        for (bi, d), (out, s_new) in zip(insts, res):
            refs[d][3][bi, rows[d], :] = out
            s_scr[bi, d] = s_new
        return carry

    lax.fori_loop(0, nchunk, body, 0)
    sf_ref[...] = s_scr[...]


def _dn_scan(qkv, gb, s0, consts):
    b, t, w3 = qkv.shape
    tb = _tile(t, 512)
    nblk = t // tb
    n = REC_W
    nb = _tile(b, SCAN_BATCH)
    fwd = lambda bi, i: (bi, i, 0)
    bwd = lambda bi, i: (bi, nblk - 1 - i, 0)
    c3 = lambda bi, i: (0, 0, 0)
    state = pl.BlockSpec((nb, 2, n, n), lambda bi, i: (bi, 0, 0, 0))
    return pl.pallas_call(
        functools.partial(_dn_scan_kernel, nchunk=tb // CHUNK),
        grid=(b // nb, nblk),
        in_specs=[pl.BlockSpec((nb, tb, w3), fwd), pl.BlockSpec((nb, tb, w3), bwd),
                  pl.BlockSpec((nb, tb, LANES), fwd), pl.BlockSpec((nb, tb, LANES), bwd),
                  state,
                  pl.BlockSpec((2, LANES, 2 * n), c3),
                  pl.BlockSpec((2, CHUNK, CHUNK), c3),
                  pl.BlockSpec((2, CHUNK, n), c3),
                  pl.BlockSpec((2, (N_LEVELS + 1) * CHUNK, n), c3),
                  pl.BlockSpec((n, n), lambda bi, i: (0, 0))],
        out_specs=(pl.BlockSpec((nb, tb, n), fwd), pl.BlockSpec((nb, tb, n), bwd), state),
        out_shape=(jax.ShapeDtypeStruct((b, t, n), F32), jax.ShapeDtypeStruct((b, t, n), F32),
                   jax.ShapeDtypeStruct((b, 2, n, n), F32)),
        scratch_shapes=[pltpu.VMEM((nb, 2, n, n), F32)],
        compiler_params=_cparams(("parallel", "arbitrary")),
        name="deltanet_scan",
    )(qkv, qkv, gb, gb, s0, consts["ex"], consts["l"], consts["cz"], consts["lvl"], consts["hm"])


def _hg_scan(hg, lb, s0, consts):
    b, t, _ = hg.shape
    tb = _tile(t, 512)
    nblk = t // tb
    n = REC_W
    nb = _tile(b, SCAN_BATCH)
    fwd = lambda col: (lambda bi, i: (bi, i, col))
    bwd = lambda col: (lambda bi, i: (bi, nblk - 1 - i, col))
    c3 = lambda bi, i: (0, 0, 0)
    blk = lambda m: pl.BlockSpec((nb, tb, n), m)
    state = pl.BlockSpec((nb, 2, n, n), lambda bi, i: (bi, 0, 0, 0))
    return pl.pallas_call(
        functools.partial(_hg_scan_kernel, nchunk=tb // CHUNK),
        grid=(b // nb, nblk),
        in_specs=[blk(fwd(0)), blk(bwd(0)), blk(fwd(1)), blk(bwd(2)), blk(fwd(3)), blk(bwd(3)),
                  pl.BlockSpec((2, n), lambda bi, i: (0, 0)),
                  state,
                  pl.BlockSpec((2, (2 + N_LEVELS) * CHUNK, CHUNK), c3),
                  pl.BlockSpec((2, (N_LEVELS + 1) * CHUNK, n), c3),
                  pl.BlockSpec((n, n), lambda bi, i: (0, 0))],
        out_specs=(blk(fwd(0)), blk(bwd(0)), state),
        out_shape=(jax.ShapeDtypeStruct((b, t, n), F32), jax.ShapeDtypeStruct((b, t, n), F32),
                   jax.ShapeDtypeStruct((b, 2, n, n), F32)),
        scratch_shapes=[pltpu.VMEM((nb, 2, n, n), F32)],
        compiler_params=_cparams(("parallel", "arbitrary")),
        name="hgrn2_scan",
    )(hg, hg, hg, hg, hg, hg, lb, s0, consts["a"], consts["lvl"], consts["hm"])


def _scan_constants_dir(rev):
    t = np.arange(CHUNK)
    p = (CHUNK - 1 - t) if rev else t
    pt, pu = p[:, None], p[None, :]
    causal = (pu <= pt).astype(np.float32)
    head = np.arange(REC_W) // CHUNK
    hm = (head[:, None] == head[None, :]).astype(np.float32)
    rows = [causal, (pu > pt).astype(np.float32)]
    masks = []
    for lev in range(N_LEVELS):
        blk = CHUNK >> (lev + 1)
        bound = (pt // (2 * blk)) * (2 * blk) + blk - 1
        right = (pt % (2 * blk)) >= blk
        rows.append(np.where(right, (pu > bound) & (pu <= pt), (pu > pt) & (pu <= bound)).astype(np.float32))
        masks.append(((pt // (2 * blk) == pu // (2 * blk)) & right & ((pu % (2 * blk)) < blk)).astype(np.float32))
    masks.append(np.eye(CHUNK, dtype=np.float32))
    d = 1 if rev else 0
    ex = np.zeros((LANES, 2 * REC_W), np.float32)
    for h in range(DN_HEADS):
        ex[d * DN_HEADS + h, h * CHUNK:(h + 1) * CHUNK] = 1.0
        ex[2 * DN_HEADS + d * DN_HEADS + h, REC_W + h * CHUNK:REC_W + (h + 1) * CHUNK] = 1.0
    return {
        "l": causal, "cz": np.tile(causal, (1, DN_HEADS)), "hm": hm, "ex": ex,
        "a": np.concatenate(rows, axis=0),
        "lvl": np.concatenate([np.tile(mk, (1, DN_HEADS)) for mk in masks], axis=0),
    }


def _scan_constants():
    fwd, bwd = _scan_constants_dir(False), _scan_constants_dir(True)
    out = {key: jnp.asarray(np.stack([fwd[key], bwd[key]])) for key in ("l", "cz", "ex", "a", "lvl")}
    out["hm"] = jnp.asarray(fwd["hm"], BF16)
    return out


def _outproj_kernel(a_ref, dof_ref, dob_ref, z_ref, gof_ref, gob_ref, gg_ref, dnw_ref, hgw_ref, g_ref,
                    w_ref, x_ref, gate_ref, sh_ref, sc_ref, rt_ref, x1_ref, h2_ref, aff_ref):
    gmat = g_ref[...]

    def gated(o, z, nw):
        return o * lax.rsqrt(_mm_group(o * o, gmat) + EPS) * nw * _silu(z)

    dmix = gated(dof_ref[0] + dob_ref[0], z_ref[0], dnw_ref[...])
    gmix = gated(gof_ref[0] + gob_ref[0], gg_ref[0], hgw_ref[...])
    y = (jnp.dot(a_ref[0], w_ref[0:ATTN_Q_W, :], preferred_element_type=F32)
         + jnp.dot(dmix.astype(BF16), w_ref[ATTN_Q_W:ATTN_Q_W + REC_W, :], preferred_element_type=F32)
         + jnp.dot(gmix.astype(BF16), w_ref[ATTN_Q_W + REC_W:, :], preferred_element_type=F32))
    x1 = x_ref[0] + gate_ref[0] * y
    x1_ref[0] = x1
    ms = jnp.mean(x1 * x1, axis=-1, keepdims=True)
    h2 = x1 * lax.rsqrt(ms + EPS) * (1.0 + sc_ref[0]) + sh_ref[0]
    h2_ref[0] = h2.astype(BF16)
    logits = lax.dot_general(rt_ref[...], h2, (((1,), (1,)), ((), ())), precision=HIGHEST,
                             preferred_element_type=F32)
    e = jnp.exp(logits - jnp.max(logits, axis=0, keepdims=True))
    aff_ref[0] = e / jnp.sum(e, axis=0, keepdims=True)


def _output_projection(a, dof, dob, dn, gof, gob, hg, dnw, hgw, gavg, w_out, x, gate, shift, scale, router_t):
    b, t, d = x.shape
    tm = _tile(t, 512)
    n = REC_W
    per_b = gate.shape[0] > 1
    mod_map = (lambda bi, i: (bi, 0, 0)) if per_b else (lambda bi, i: (0, 0, 0))
    const = lambda bi, i: (0, 0)
    tok = lambda bi, i: (bi, i, 0)
    return pl.pallas_call(
        _outproj_kernel,
        grid=(b, t // tm),
        in_specs=[pl.BlockSpec((1, tm, ATTN_Q_W), tok),
                  pl.BlockSpec((1, tm, n), tok), pl.BlockSpec((1, tm, n), tok),
                  pl.BlockSpec((1, tm, n), lambda bi, i: (bi, i, 3)),
                  pl.BlockSpec((1, tm, n), tok), pl.BlockSpec((1, tm, n), tok),
                  pl.BlockSpec((1, tm, n), lambda bi, i: (bi, i, 4)),
                  pl.BlockSpec((1, n), const), pl.BlockSpec((1, n), const),
                  pl.BlockSpec((n, n), const),
                  pl.BlockSpec((d, d), const),
                  pl.BlockSpec((1, tm, d), tok),
                  pl.BlockSpec((1, 1, d), mod_map), pl.BlockSpec((1, 1, d), mod_map),
                  pl.BlockSpec((1, 1, d), mod_map),
                  pl.BlockSpec((N_EXPERTS, d), const)],
        out_specs=(pl.BlockSpec((1, tm, d), tok), pl.BlockSpec((1, tm, d), tok),
                   pl.BlockSpec((1, N_EXPERTS, tm), lambda bi, i: (bi, 0, i))),
        out_shape=(jax.ShapeDtypeStruct((b, t, d), F32), jax.ShapeDtypeStruct((b, t, d), BF16),
                   jax.ShapeDtypeStruct((b, N_EXPERTS, t), F32)),
        compiler_params=_cparams(("parallel", "parallel")),
        name="output_projection_router",
    )(a, dof, dob, dn, gof, gob, hg, dnw, hgw, gavg, w_out, x, gate, shift, scale, router_t)


def _select_kernel(aff_ref, u_ref, bs_ref, bst_ref, su_ref, slot_ref, base_ref, nsub_ref, inc_scr, *, cap, nblk):
    x = aff_ref[0]
    bits = pltpu.bitcast(x, jnp.int32)
    lo = jnp.zeros((N_EXPERTS, 1), jnp.int32)
    for bit in range(30, -1, -1):
        cand = lo | (1 << bit)
        cnt = jnp.sum((bits >= cand).astype(jnp.int32), axis=1, keepdims=True)
        lo = jnp.where(cnt >= cap, cand, lo)
    gt = bits > lo
    eq = bits == lo
    umat = u_ref[...]

    def prefix(mask):
        mb = mask.astype(BF16)
        for j in range(nblk):
            inc_scr[:, j * LANES:(j + 1) * LANES] = jnp.dot(mb[:, j * LANES:(j + 1) * LANES], umat,
                                                            preferred_element_type=F32)
        totals = jnp.dot(mb, bs_ref[...], preferred_element_type=F32)
        offs = _mm_hi(totals, su_ref[...])
        return inc_scr[...] + _mm_hi(offs, bst_ref[...]), offs, totals

    eqf = eq.astype(F32)
    n_gt = jnp.sum(gt.astype(F32), axis=1, keepdims=True)
    eq_before, _, _ = prefix(eqf)
    sel = gt | (eq & ((eq_before - eqf) < (cap - n_gt)))
    self_ = sel.astype(F32)
    pos, offs, totals = prefix(self_)
    slot_ref[0] = jnp.where(sel, pos - 1.0, -1.0).astype(jnp.int32)
    start = jnp.floor(offs * (1.0 / ROUTE_ALIGN)) * ROUTE_ALIGN
    span = offs + totals - start
    n_sub = functools.reduce(jnp.add, [jnp.where(span > j * ROUTE_SUB, 1.0, 0.0)
                                       for j in range(ROUTE_WIN // ROUTE_SUB)])
    base_ref[0] = start.astype(jnp.int32)
    nsub_ref[0] = jnp.max(n_sub, axis=0, keepdims=True).astype(jnp.int32)


def _moe_select(aff_t, cap):
    b, e, t = aff_t.shape
    nblk = t // ROUTE_BLK
    u = jnp.asarray(np.triu(np.ones((LANES, LANES), np.float32)), BF16)
    blk = np.arange(t) // ROUTE_BLK
    bs = (blk[:, None] == np.arange(nblk)[None, :]).astype(np.float32)
    su = np.triu(np.ones((nblk, nblk), np.float32), 1)
    const = lambda bi: (0, 0)
    return pl.pallas_call(
        functools.partial(_select_kernel, cap=cap, nblk=nblk),
        grid=(b,),
        in_specs=[pl.BlockSpec((1, e, t), lambda bi: (bi, 0, 0)),
                  pl.BlockSpec((LANES, LANES), const),
                  pl.BlockSpec((t, nblk), const),
                  pl.BlockSpec((nblk, t), const),
                  pl.BlockSpec((nblk, nblk), const)],
        out_specs=(pl.BlockSpec((1, e, t), lambda bi: (bi, 0, 0)),
                   pl.BlockSpec((1, e, nblk), lambda bi: (bi, 0, 0)),
                   pl.BlockSpec((1, 1, nblk), lambda bi: (bi, 0, 0))),
        out_shape=(jax.ShapeDtypeStruct((b, e, t), jnp.int32), jax.ShapeDtypeStruct((b, e, nblk), jnp.int32),
                   jax.ShapeDtypeStruct((b, 1, nblk), jnp.int32)),
        scratch_shapes=[pltpu.VMEM((e, t), F32)],
        compiler_params=_cparams(("parallel",)),
        name="moe_select",
    )(aff_t, u, jnp.asarray(bs, BF16), jnp.asarray(bs.T), jnp.asarray(su))


def _gather_kernel(start_sm, nsub_sm, slot_ref, h_ref, xe_ref, *, nblk):
    bi = pl.program_id(0)
    xe_ref[0] = jnp.zeros(xe_ref.shape[1:], BF16)
    row = lax.broadcasted_iota(jnp.int32, (ROUTE_SUB, ROUTE_BLK), 0)

    per_iter = _tile(nblk, GATHER_BLOCKS)

    def body(kk, carry):
        blocks, n_sub = [], 0
        for c in range(per_iter):
            k = kk * per_iter + c
            t0 = pl.multiple_of(k * ROUTE_BLK, ROUTE_BLK)
            firsts = [start_sm[(bi * N_EXPERTS + e) * nblk + k] for e in range(N_EXPERTS)]
            blocks.append((h_ref[0, pl.ds(t0, ROUTE_BLK), :], slot_ref[0, :, pl.ds(t0, ROUTE_BLK)], firsts))
            n_sub = jnp.maximum(n_sub, nsub_sm[bi * nblk + k])

        def part(j, hblk, slots, firsts):
            starts = [pl.multiple_of(first + j * ROUTE_SUB, ROUTE_ALIGN) for first in firsts]
            onehot = jnp.concatenate(
                [jnp.where(row == slots[e:e + 1, :] - starts[e], 1.0, 0.0).astype(BF16)
                 for e in range(N_EXPERTS)], axis=0)
            yield
            return starts, jnp.dot(onehot, hblk, preferred_element_type=F32).astype(BF16)

        def sub(j, inner):
            for starts, rows in _lockstep([part(j, *blk) for blk in blocks]):
                for e in range(N_EXPERTS):
                    win = pl.ds(starts[e], ROUTE_SUB)
                    xe_ref[0, e, win, :] = xe_ref[0, e, win, :] + rows[e * ROUTE_SUB:(e + 1) * ROUTE_SUB]
            return inner

        lax.fori_loop(0, n_sub, sub, 0)
        return carry

    lax.fori_loop(0, nblk // per_iter, body, 0)


def _moe_gather(h2, slot, start_flat, nsub_flat, cap):
    b, t, d = h2.shape
    nblk = t // ROUTE_BLK
    cp = cap + ROUTE_WIN
    dq = _tile(d, 256)
    grid_spec = pltpu.PrefetchScalarGridSpec(
        num_scalar_prefetch=2,
        grid=(b, d // dq),
        in_specs=[pl.BlockSpec((1, N_EXPERTS, t), lambda bi, j, s0, s1: (bi, 0, 0)),
                  pl.BlockSpec((1, t, dq), lambda bi, j, s0, s1: (bi, 0, j))],
        out_specs=pl.BlockSpec((1, N_EXPERTS, cp, dq), lambda bi, j, s0, s1: (bi, 0, 0, j)),
    )
    return pl.pallas_call(
        functools.partial(_gather_kernel, nblk=nblk),
        grid_spec=grid_spec,
        out_shape=jax.ShapeDtypeStruct((b, N_EXPERTS, cp, d), BF16),
        compiler_params=_cparams(("parallel", "parallel")),
        name="moe_gather",
    )(start_flat, nsub_flat, slot, h2)


def _ffn_kernel(x_ref, wg_ref, wu_ref, wd_ref, y_ref, *, cap):
    x = x_ref[0, 0, 0:cap, :]
    a = jnp.dot(x, wg_ref[0], preferred_element_type=F32)
    u = jnp.dot(x, wu_ref[0], preferred_element_type=F32)
    y_ref[0, 0, 0:cap, :] = jnp.dot((_silu(a) * u).astype(BF16), wd_ref[0],
                                    preferred_element_type=F32).astype(BF16)
    y_ref[0, 0, cap:, :] = jnp.zeros((y_ref.shape[2] - cap, y_ref.shape[3]), BF16)


def _moe_ffn(xe, wg, wu, wd, cap):
    b, e, cp, d = xe.shape
    f = wg.shape[-1]
    return pl.pallas_call(
        functools.partial(_ffn_kernel, cap=cap),
        grid=(e, b),
        in_specs=[pl.BlockSpec((1, 1, cp, d), lambda ei, bi: (bi, ei, 0, 0)),
                  pl.BlockSpec((1, d, f), lambda ei, bi: (ei, 0, 0)),
                  pl.BlockSpec((1, d, f), lambda ei, bi: (ei, 0, 0)),
                  pl.BlockSpec((1, f, d), lambda ei, bi: (ei, 0, 0))],
        out_specs=pl.BlockSpec((1, 1, cp, d), lambda ei, bi: (bi, ei, 0, 0)),
        out_shape=jax.ShapeDtypeStruct((b, e, cp, d), BF16),
        compiler_params=_cparams(("parallel", "parallel")),
        name="moe_ffn",
    )(xe, wg, wu, wd)


def _combine_kernel(start_sm, nsub_sm, ye_ref, x_ref, gate_ref, slot_ref, aff_ref, o_ref, *, nblk, per_step):
    bi = pl.program_id(0)
    row = lax.broadcasted_iota(jnp.int32, (ROUTE_SUB, ROUTE_BLK), 0)
    blocks, n_sub = [], 0
    for c in range(per_step):
        k = pl.program_id(2) * per_step + c
        tok = slice(c * ROUTE_BLK, (c + 1) * ROUTE_BLK)
        firsts = [start_sm[(bi * N_EXPERTS + e) * nblk + k] for e in range(N_EXPERTS)]
        blocks.append((tok, slot_ref[0, :, tok], aff_ref[0, :, tok], firsts))
        n_sub = jnp.maximum(n_sub, nsub_sm[bi * nblk + k])

    def part(j, slots, aff, firsts):
        starts = [pl.multiple_of(first + j * ROUTE_SUB, ROUTE_ALIGN) for first in firsts]
        wsel = jnp.concatenate([jnp.where(row == slots[e:e + 1, :] - starts[e], aff[e:e + 1, :], 0.0)
                                for e in range(N_EXPERTS)], axis=0)
        ys = jnp.concatenate([ye_ref[0, e, pl.ds(starts[e], ROUTE_SUB), :] for e in range(N_EXPERTS)], axis=0)
        yield
        return lax.dot_general(wsel.astype(BF16), ys, (((0,), (0,)), ((), ())), preferred_element_type=F32)

    def sub(j, accs):
        adds = _lockstep([part(j, slots, aff, firsts) for _, slots, aff, firsts in blocks])
        return tuple(acc + add for acc, add in zip(accs, adds))

    zero = jnp.zeros((ROUTE_BLK, x_ref.shape[2]), F32)
    accs = lax.fori_loop(0, n_sub, sub, (zero,) * per_step)
    for (tok, _, _, _), acc in zip(blocks, accs):
        o_ref[0, tok, :] = x_ref[0, tok, :] + gate_ref[0] * acc


def _moe_combine(ye, x1, gate, slot, aff_t, start_flat, nsub_flat):
    b, t, d = x1.shape
    cp = ye.shape[2]
    nblk = t // ROUTE_BLK
    dh = d // 2
    per_b = gate.shape[0] > 1
    per_step = _tile(nblk, COMBINE_BLOCKS)
    tok = per_step * ROUTE_BLK
    grid_spec = pltpu.PrefetchScalarGridSpec(
        num_scalar_prefetch=2,
        grid=(b, 2, nblk // per_step),
        in_specs=[pl.BlockSpec((1, N_EXPERTS, cp, dh), lambda bi, j, k, s0, s1: (bi, 0, 0, j)),
                  pl.BlockSpec((1, tok, dh), lambda bi, j, k, s0, s1: (bi, k, j)),
                  pl.BlockSpec((1, 1, dh), (lambda bi, j, k, s0, s1: (bi, 0, j)) if per_b
                               else (lambda bi, j, k, s0, s1: (0, 0, j))),
                  pl.BlockSpec((1, N_EXPERTS, tok), lambda bi, j, k, s0, s1: (bi, 0, k)),
                  pl.BlockSpec((1, N_EXPERTS, tok), lambda bi, j, k, s0, s1: (bi, 0, k))],
        out_specs=pl.BlockSpec((1, tok, dh), lambda bi, j, k, s0, s1: (bi, k, j)),
    )
    return pl.pallas_call(
        functools.partial(_combine_kernel, nblk=nblk, per_step=per_step),
        grid_spec=grid_spec,
        out_shape=jax.ShapeDtypeStruct((b, t, d), F32),
        compiler_params=_cparams(("parallel", "parallel", "arbitrary")),
        name="moe_combine",
    )(start_flat, nsub_flat, ye, x1, gate, slot, aff_t)


def _expert_choice_ffn(x1, h2, aff_t, gate, wg, wu, wd):
    b, t, _ = x1.shape
    cap = CAPACITY_FACTOR * t // N_EXPERTS
    slot, start, nsub = _moe_select(aff_t, cap)
    start_flat, nsub_flat = start.reshape(-1), nsub.reshape(-1)
    xe = _moe_gather(h2, slot, start_flat, nsub_flat, cap)
    ye = _moe_ffn(xe, wg, wu, wd, cap)
    return _moe_combine(ye, x1, gate, slot, aff_t, start_flat, nsub_flat)


def _rope_tables(n_tokens):
    rows = n_tokens // GRID_W
    row = jnp.repeat(jnp.arange(rows, dtype=F32), GRID_W)
    col = jnp.tile(jnp.arange(GRID_W, dtype=F32), rows)
    n_freq = HEAD_DIM // 4
    inv_freq = ROPE_THETA ** (-jnp.arange(n_freq, dtype=F32) / n_freq)
    ang_r = row[:, None] * inv_freq
    ang_c = col[:, None] * inv_freq
    cos = jnp.concatenate([jnp.cos(ang_r)] * 2 + [jnp.cos(ang_c)] * 2, axis=-1)
    sin = jnp.concatenate([jnp.sin(ang_r)] * 2 + [jnp.sin(ang_c)] * 2, axis=-1)
    return jnp.tile(cos, (1, LANES // HEAD_DIM)), jnp.tile(sin, (1, LANES // HEAD_DIM))


def _block_diag(width, block, value):
    idx = np.arange(width) // block
    return jnp.asarray((idx[:, None] == idx[None, :]).astype(np.float32) * value)


def kernel(x, c, ctx, c_ctx, w_mod, b_mod, w_in, w_out, attn_q_norm, attn_k_norm, dn_conv, dn_a_log,
           dn_dt_bias, dn_norm, hg_lower_bounds, hg_norm, moe_router, moe_w_gate, moe_w_up, moe_w_down):
    depth = w_mod.shape[0]
    b, t_lat, d = x.shape
    cos, sin = _rope_tables(t_lat)
    g_head = _block_diag(LANES, HEAD_DIM, 1.0 / HEAD_DIM)
    g_mean = _block_diag(REC_W, HEAD_DIM, 1.0 / HEAD_DIM)
    g_sum = _block_diag(REC_W, HEAD_DIM, 1.0)
    consts = _scan_constants()
    s_zero = jnp.zeros((b, 2, REC_W, REC_W), F32)

    lb_w = jax.nn.softmax(hg_lower_bounds.astype(F32), axis=0)
    hg_lb = jnp.cumsum(lb_w, axis=0) - lb_w[0]

    rows = ((b + 1 + 7) // 8) * 8
    cond = jnp.zeros((rows, d), F32).at[:b].set(c).at[b].set(c_ctx)
    mod = _modulation(cond, w_mod, b_mod)

    n_small = 4 * DN_HEADS
    w_in_r = jnp.concatenate(
        [w_in[:, :, :C_HG], w_in[:, :, C_HG + n_small:], w_in[:, :, C_HG:C_HG + n_small],
         jnp.zeros((depth, d, C_END - C_BA - n_small), w_in.dtype)], axis=-1).astype(BF16)
    w_out_b = _cast_bf16(w_out)
    wg_b, wu_b, wd_b = _cast_bf16(moe_w_gate), _cast_bf16(moe_w_up), _cast_bf16(moe_w_down)

    x_lat, x_ctx = x, ctx
    for l in range(depth):
        ctx_out = l < depth - 1
        m_lat = [mod[l, :b, j * d:(j + 1) * d][:, None, :] for j in range(6)]
        m_ctx = [mod[l, b:b + 1, j * d:(j + 1) * d][:, None, :] for j in range(6)]
        qn = jnp.tile(attn_q_norm[l], LANES // HEAD_DIM)[None, :]
        kn = jnp.tile(attn_k_norm[l], LANES // HEAD_DIM)[None, :]
        conv_w = jnp.zeros((8, 3 * REC_W), F32).at[:CONV_K].set(dn_conv[l])
        pad = jnp.zeros((LANES - 4 * DN_HEADS,), F32)
        neg_a = jnp.concatenate([jnp.zeros((2 * DN_HEADS,), F32), -jnp.exp(dn_a_log[l].reshape(-1)), pad])[None, :]
        dt_b = jnp.concatenate([jnp.zeros((2 * DN_HEADS,), F32), dn_dt_bias[l].reshape(-1), pad])[None, :]
        dnw = jnp.tile(dn_norm[l], DN_HEADS)[None, :]
        hgw = jnp.tile(hg_norm[l], HG_HEADS)[None, :]
        router_t = moe_router[l].T

        streams = {}
        for name, xs, ms, rotate in (("ctx", x_ctx, m_ctx, False), ("lat", x_lat, m_lat, True)):
            t = xs.shape[1]
            q, kt, v, dn, hg, ba = _input_projection(xs, ms[0], ms[1], w_in_r[l], qn, kn,
                                                     cos[:t], sin[:t], g_head, rotate)
            qkv, gb = _dn_prep(dn, ba, conv_w, neg_a, dt_b, g_sum)
            streams[name] = dict(q=q, kv=(kt, v), dn=dn, hg=hg, qkv=qkv, gb=gb)

        sc, sl = streams["ctx"], streams["lat"]
        dcf, dcb, dn_state = _dn_scan(sc["qkv"], sc["gb"], s_zero, consts)
        dlf, dlb, _ = _dn_scan(sl["qkv"], sl["gb"], dn_state, consts)
        gcf, gcb, hg_state = _hg_scan(sc["hg"], hg_lb[l], s_zero, consts)
        glf, glb, _ = _hg_scan(sl["hg"], hg_lb[l], hg_state, consts)

        a_lat = _attention(sl["q"], [sc["kv"], sl["kv"]], t_lat)
        x1, h2, aff_t = _output_projection(
            a_lat, dlf, dlb, sl["dn"], glf, glb, sl["hg"], dnw, hgw, g_mean, w_out_b[l], x_lat,
            m_lat[2], m_lat[3], m_lat[4], router_t)
        x_lat = _expert_choice_ffn(x1, h2, aff_t, m_lat[5], wg_b[l], wu_b[l], wd_b[l])
        if ctx_out:
            a_ctx = _attention(sc["q"], [sc["kv"]], x_ctx.shape[1])
            x1, h2, aff_t = _output_projection(
                a_ctx, dcf, dcb, sc["dn"], gcf, gcb, sc["hg"], dnw, hgw, g_mean, w_out_b[l], x_ctx,
                m_ctx[2], m_ctx[3], m_ctx[4], router_t)
            x_ctx = _expert_choice_ffn(x1, h2, aff_t, m_ctx[5], wg_b[l], wu_b[l], wd_b[l])
    return x_lat
```

```python
import functools

import numpy as np
import jax
import jax.numpy as jnp
from jax import lax
from jax.experimental import pallas as pl
from jax.experimental.pallas import tpu as pltpu

F32 = jnp.float32
BF16 = jnp.bfloat16
HIGHEST = lax.Precision.HIGHEST

HEAD_DIM = 64
N_Q_HEADS = 8
N_KV_HEADS = 2
GQA_GROUP = N_Q_HEADS // N_KV_HEADS
DN_HEADS = 4
HG_HEADS = 4
GRID_W = 64
ROPE_THETA = 10000.0
CONV_K = 5
CHUNK = 64
N_EXPERTS = 16
CAPACITY_FACTOR = 2
EPS = 1e-6
LOG2E = 1.4426950408889634
ATTN_KV_GROUP = 1408
V_ROWS = HEAD_DIM + 16

ATTN_Q_W = N_Q_HEADS * HEAD_DIM
ATTN_KV_W = N_KV_HEADS * HEAD_DIM
REC_W = DN_HEADS * HEAD_DIM
LANES = 128
ROUTE_BLK = LANES
ROUTE_ALIGN = 16
ROUTE_WIN = ROUTE_BLK + ROUTE_ALIGN
COMBINE_BLOCKS = 8
GATHER_BLOCKS = 8
ROUTE_SUB = ROUTE_WIN // 3
N_LEVELS = 6
SCAN_BATCH = 4
VMEM_LIMIT = 56 * 1024 * 1024

C_Q, C_K, C_V, C_DN, C_HG, C_BA, C_END = 0, 512, 640, 768, 1792, 3072, 3200


def _mm(a, b):
    return jnp.dot(a.astype(BF16), b.astype(BF16), preferred_element_type=F32)


def _mm_nt(a, b):
    return lax.dot_general(a.astype(BF16), b.astype(BF16), (((1,), (1,)), ((), ())),
                           preferred_element_type=F32)


def _mm_tn(a, b):
    return lax.dot_general(a.astype(BF16), b.astype(BF16), (((0,), (0,)), ((), ())),
                           preferred_element_type=F32)


def _mm_hi(a, b):
    return jnp.dot(a, b, precision=HIGHEST, preferred_element_type=F32)


def _mm_group(a, gmat):
    hi = a.astype(BF16)
    lo = (a - hi.astype(F32)).astype(BF16)
    t = jnp.dot(jnp.concatenate([hi, lo], axis=0), gmat.astype(BF16), preferred_element_type=F32)
    return t[:a.shape[0]] + t[a.shape[0]:]


def _sigmoid(x):
    return 1.0 / (1.0 + jnp.exp(-x))


def _silu(x):
    return x * _sigmoid(x)


def _cparams(sem):
    return pltpu.CompilerParams(dimension_semantics=sem, vmem_limit_bytes=VMEM_LIMIT)


def _tile(n, pref):
    return pref if n % pref == 0 else n


def _cast_kernel(x_ref, o_ref):
    o_ref[...] = x_ref[...].astype(BF16)


def _cast_bf16(w):
    r, c = w.shape[-2:]
    w3 = w.reshape(-1, r, c)
    rb = _tile(r, 512)
    out = pl.pallas_call(
        _cast_kernel,
        grid=(w3.shape[0], r // rb),
        in_specs=[pl.BlockSpec((1, rb, c), lambda i, j: (i, j, 0))],
        out_specs=pl.BlockSpec((1, rb, c), lambda i, j: (i, j, 0)),
        out_shape=jax.ShapeDtypeStruct(w3.shape, BF16),
        compiler_params=_cparams(("parallel", "parallel")),
        name="cast_bf16",
    )(w3)
    return out.reshape(w.shape)


def _mod_kernel(c_ref, w_ref, b_ref, o_ref):
    o_ref[0] = _mm_hi(_silu(c_ref[...]), w_ref[0]) + b_ref[0]


def _modulation(cond, w_mod, b_mod):
    depth, d, n = w_mod.shape
    rows = cond.shape[0]
    tn = _tile(n, 1536)
    return pl.pallas_call(
        _mod_kernel,
        grid=(depth, n // tn),
        in_specs=[pl.BlockSpec((rows, d), lambda l, j: (0, 0)),
                  pl.BlockSpec((1, d, tn), lambda l, j: (l, 0, j)),
                  pl.BlockSpec((1, 1, tn), lambda l, j: (l, 0, j))],
        out_specs=pl.BlockSpec((1, rows, tn), lambda l, j: (l, 0, j)),
        out_shape=jax.ShapeDtypeStruct((depth, rows, n), F32),
        compiler_params=_cparams(("parallel", "parallel")),
        name="modulation",
    )(cond, w_mod, b_mod.reshape(depth, 1, n))


def _inproj_kernel(x_ref, sh_ref, sc_ref, w_ref, qn_ref, kn_ref, cos_ref, sin_ref, g_ref,
                   q_ref, k_ref, v_ref, dn_ref, hg_ref, ba_ref, *, rotate):
    x = x_ref[0]
    ms = jnp.mean(x * x, axis=-1, keepdims=True)
    h = x * lax.rsqrt(ms + EPS) * (1.0 + sc_ref[0]) + sh_ref[0]
    hb = h.astype(BF16)
    gmat = g_ref[...]

    def proj(lo, hi):
        return jnp.dot(hb, w_ref[:, lo:hi], preferred_element_type=F32)

    def head_norm_rope(t, nw):
        t = t * lax.rsqrt(_mm_group(t * t, gmat) + EPS) * nw
        if rotate:
            lane = lax.broadcasted_iota(jnp.int32, t.shape, 1)
            first = (lane % 32) < 16
            rot = jnp.where(first, -pltpu.roll(t, LANES - 16, 1), pltpu.roll(t, 16, 1))
            t = t * cos_ref[...] + rot * sin_ref[...]
        return t

    def q_group(j):
        t = proj(C_Q + j * LANES, C_Q + (j + 1) * LANES)
        yield
        qj = head_norm_rope(t, qn_ref[...])
        qt = (qj * (HEAD_DIM ** -0.5 * LOG2E)).T.astype(BF16)
        q_ref[0, 2 * j] = qt[:HEAD_DIM]
        q_ref[0, 2 * j + 1] = qt[HEAD_DIM:]

    def k_group():
        t = proj(C_K, C_V)
        yield
        k = head_norm_rope(t, kn_ref[...]).astype(BF16)
        k_ref[0, 0] = k[:, :HEAD_DIM]
        k_ref[0, 1] = k[:, HEAD_DIM:]

    def v_group():
        v = proj(C_V, C_DN)
        yield
        vt = v.T
        ones = jnp.ones((V_ROWS - HEAD_DIM, vt.shape[1]), F32)
        v_ref[0, 0] = jnp.concatenate([vt[:HEAD_DIM], ones], axis=0).astype(BF16)
        v_ref[0, 1] = jnp.concatenate([vt[HEAD_DIM:], ones], axis=0).astype(BF16)

    def plain_group(o_ref, base, lo, hi):
        t = proj(base + lo, base + hi)
        yield
        o_ref[0, :, lo:hi] = t

    step = 4 * LANES
    plain = [plain_group(o_ref, base, lo, min(lo + step, width))
             for o_ref, base, width in ((dn_ref, C_DN, C_HG - C_DN), (hg_ref, C_HG, C_BA - C_HG),
                                        (ba_ref, C_BA, C_END - C_BA))
             for lo in range(0, width, step)]
    _lockstep([q_group(j) for j in range(ATTN_Q_W // LANES)] + [k_group(), v_group()] + plain, stagger=True)


def _input_projection(x, shift, scale, w, qn, kn, cos, sin, gmat, rotate):
    b, t, d = x.shape
    tm = _tile(t, 512)
    per_b = shift.shape[0] > 1
    mod_map = (lambda bi, i: (bi, 0, 0)) if per_b else (lambda bi, i: (0, 0, 0))
    const = lambda bi, i: (0, 0)
    out_shape = (
        jax.ShapeDtypeStruct((b, N_Q_HEADS, HEAD_DIM, t), BF16),
        jax.ShapeDtypeStruct((b, N_KV_HEADS, t, HEAD_DIM), BF16),
        jax.ShapeDtypeStruct((b, N_KV_HEADS, V_ROWS, t), BF16),
        jax.ShapeDtypeStruct((b, t, C_HG - C_DN), F32),
        jax.ShapeDtypeStruct((b, t, C_BA - C_HG), F32),
        jax.ShapeDtypeStruct((b, t, C_END - C_BA), F32),
    )
    return pl.pallas_call(
        functools.partial(_inproj_kernel, rotate=rotate),
        grid=(b, t // tm),
        in_specs=[pl.BlockSpec((1, tm, d), lambda bi, i: (bi, i, 0)),
                  pl.BlockSpec((1, 1, d), mod_map),
                  pl.BlockSpec((1, 1, d), mod_map),
                  pl.BlockSpec((d, C_END), const),
                  pl.BlockSpec((1, LANES), const),
                  pl.BlockSpec((1, LANES), const),
                  pl.BlockSpec((tm, LANES), lambda bi, i: (i, 0)),
                  pl.BlockSpec((tm, LANES), lambda bi, i: (i, 0)),
                  pl.BlockSpec((LANES, LANES), const)],
        out_specs=(pl.BlockSpec((1, N_Q_HEADS, HEAD_DIM, tm), lambda bi, i: (bi, 0, 0, i)),
                   pl.BlockSpec((1, N_KV_HEADS, tm, HEAD_DIM), lambda bi, i: (bi, 0, i, 0)),
                   pl.BlockSpec((1, N_KV_HEADS, V_ROWS, tm), lambda bi, i: (bi, 0, 0, i)),
                   pl.BlockSpec((1, tm, C_HG - C_DN), lambda bi, i: (bi, i, 0)),
                   pl.BlockSpec((1, tm, C_BA - C_HG), lambda bi, i: (bi, i, 0)),
                   pl.BlockSpec((1, tm, C_END - C_BA), lambda bi, i: (bi, i, 0))),
        out_shape=out_shape,
        compiler_params=_cparams(("parallel", "parallel")),
        name="input_projection",
    )(x, shift, scale, w, qn, kn, cos, sin, gmat)


def _lockstep(gens, stagger=False):
    results = [None] * len(gens)
    live, started = [], 0
    while live or started < len(gens):
        fresh = 1 if stagger else len(gens)
        live += list(range(started, min(started + fresh, len(gens))))
        started = min(started + fresh, len(gens))
        for i in reversed(list(live)):
            try:
                next(gens[i])
            except StopIteration as stop:
                results[i] = stop.value
                live.remove(i)
    return results


def _attn_part(qt, ks, vts):
    ss = [jnp.dot(k, qt, preferred_element_type=F32).astype(BF16) for k in ks]
    yield
    m = functools.reduce(jnp.maximum, [jnp.max(s, axis=0, keepdims=True) for s in ss])
    ps = [jnp.exp2(s - m) for s in ss]
    yield
    o = functools.reduce(jnp.add, [jnp.dot(vt, p, preferred_element_type=F32) for p, vt in zip(ps, vts)])
    return m.astype(F32), o


def _attn_kernel(*refs, n_src):
    q_ref, o_ref = refs[0], refs[-1]
    groups, cur, room = [], ([], []), ATTN_KV_GROUP
    for i in range(n_src):
        k_ref, vt_ref = refs[1 + 2 * i], refs[2 + 2 * i]
        lo, tk = 0, k_ref.shape[2]
        while lo < tk:
            n = min(room, tk - lo)
            cur[0].append(k_ref[0, 0, lo:lo + n, :])
            cur[1].append(vt_ref[0, 0, :, lo:lo + n])
            lo, room = lo + n, room - n
            if room == 0:
                groups.append(cur)
                cur, room = ([], []), ATTN_KV_GROUP
    if cur[0]:
        groups.append(cur)
    chains = [(h, g) for h in range(GQA_GROUP) for g in range(len(groups))]
    parts = _lockstep([_attn_part(q_ref[0, h], *groups[g]) for h, g in chains], stagger=True)
    outs = []
    for h in range(GQA_GROUP):
        mine = [parts[i] for i, (hh, _) in enumerate(chains) if hh == h]
        m = functools.reduce(jnp.maximum, [pm for pm, _ in mine])
        o = functools.reduce(jnp.add, [po * jnp.exp2(pm - m) for pm, po in mine])
        outs.append(o[:HEAD_DIM] / o[HEAD_DIM:HEAD_DIM + 1])
    o_ref[0] = jnp.concatenate(outs, axis=0).T.astype(BF16)


def _attention(qt, sources, t):
    b = qt.shape[0]
    tq = _tile(t, 256)
    gw = GQA_GROUP * HEAD_DIM
    in_specs = [pl.BlockSpec((1, GQA_GROUP, HEAD_DIM, tq), lambda bi, g, i: (bi, g, 0, i))]
    args = [qt]
    for k, vt in sources:
        tk = k.shape[2]
        in_specs.append(pl.BlockSpec((1, 1, tk, HEAD_DIM), lambda bi, g, i: (bi, g, 0, 0)))
        in_specs.append(pl.BlockSpec((1, 1, V_ROWS, tk), lambda bi, g, i: (bi, g, 0, 0)))
        args += [k, vt]
    return pl.pallas_call(
        functools.partial(_attn_kernel, n_src=len(sources)),
        grid=(b, N_KV_HEADS, t // tq),
        in_specs=in_specs,
        out_specs=pl.BlockSpec((1, tq, gw), lambda bi, g, i: (bi, i, g)),
        out_shape=jax.ShapeDtypeStruct((b, t, ATTN_Q_W), BF16),
        compiler_params=_cparams(("parallel", "parallel", "parallel")),
        name="attention",
    )(*args)


def _dn_prep_kernel(x_ref, xp_ref, xn_ref, ba_ref, cw_ref, na_ref, dtb_ref, g_ref, o_ref, gb_ref, buf):
    i = pl.program_id(1)
    n = pl.num_programs(1)
    tm = x_ref.shape[1]
    buf[0:8] = jnp.where(i > 0, xp_ref[0], 0.0)
    buf[8:8 + tm] = x_ref[0]
    buf[8 + tm:16 + tm] = jnp.where(i < n - 1, xn_ref[0], 0.0)
    half = CONV_K // 2
    y = cw_ref[0:1, :] * buf[8 - half:8 - half + tm]
    for j in range(1, CONV_K):
        y = y + cw_ref[j:j + 1, :] * buf[8 - half + j:8 - half + j + tm]
    y = _silu(y)
    gmat = g_ref[...]
    q = y[:, :REC_W]
    k = y[:, REC_W:2 * REC_W]
    o_ref[0, :, 0:REC_W] = q * lax.rsqrt(_mm_group(q * q, gmat) + EPS) * HEAD_DIM ** -0.5
    o_ref[0, :, REC_W:2 * REC_W] = k * lax.rsqrt(_mm_group(k * k, gmat) + EPS)
    o_ref[0, :, 2 * REC_W:] = y[:, 2 * REC_W:]
    ba = ba_ref[0]
    z = ba + dtb_ref[...]
    softplus = jnp.maximum(z, 0.0) + jnp.log1p(jnp.exp(-jnp.abs(z)))
    lane = lax.broadcasted_iota(jnp.int32, ba.shape, 1)
    gb_ref[0] = jnp.where(lane < 2 * DN_HEADS, _sigmoid(ba), na_ref[...] * softplus)


def _dn_prep(dn, ba, conv_w, neg_a, dt_bias, gsum):
    b, t, _ = dn.shape
    tm = _tile(t, 512)
    w3 = 3 * REC_W
    nb8 = t // 8
    return pl.pallas_call(
        _dn_prep_kernel,
        grid=(b, t // tm),
        in_specs=[pl.BlockSpec((1, tm, w3), lambda bi, i: (bi, i, 0)),
                  pl.BlockSpec((1, 8, w3), lambda bi, i: (bi, jnp.maximum(i * (tm // 8) - 1, 0), 0)),
                  pl.BlockSpec((1, 8, w3), lambda bi, i: (bi, jnp.minimum((i + 1) * (tm // 8), nb8 - 1), 0)),
                  pl.BlockSpec((1, tm, LANES), lambda bi, i: (bi, i, 0)),
                  pl.BlockSpec((8, w3), lambda bi, i: (0, 0)),
                  pl.BlockSpec((1, LANES), lambda bi, i: (0, 0)),
                  pl.BlockSpec((1, LANES), lambda bi, i: (0, 0)),
                  pl.BlockSpec((REC_W, REC_W), lambda bi, i: (0, 0))],
        out_specs=(pl.BlockSpec((1, tm, w3), lambda bi, i: (bi, i, 0)),
                   pl.BlockSpec((1, tm, LANES), lambda bi, i: (bi, i, 0))),
        out_shape=(jax.ShapeDtypeStruct((b, t, w3), F32), jax.ShapeDtypeStruct((b, t, LANES), F32)),
        scratch_shapes=[pltpu.VMEM((tm + 16, w3), F32)],
        compiler_params=_cparams(("parallel", "parallel")),
        name="deltanet_prep",
    )(dn, dn, dn, ba, conv_w, neg_a, dt_bias, gsum)


def _split2(a):
    hi = a.astype(BF16)
    return hi, (a - hi.astype(F32)).astype(BF16)


def _split3(a):
    hi = a.astype(BF16)
    r = a - hi.astype(F32)
    mid = r.astype(BF16)
    return hi, mid, (r - mid.astype(F32)).astype(BF16)


def _bdiag(a, hm):
    return jnp.concatenate([a] * DN_HEADS, axis=0) * hm


def _mm_bd(a, b, hm):
    return jnp.dot(a.astype(BF16), _bdiag(b.astype(BF16), hm), preferred_element_type=F32)


def _mm_sel_l(sel, b, terms=3):
    n = b.shape[1]
    parts = _split3(b) if terms == 3 else _split2(b)
    t = jnp.dot(sel.astype(BF16), jnp.concatenate(parts, axis=1), preferred_element_type=F32)
    return functools.reduce(jnp.add, [t[:, i * n:(i + 1) * n] for i in range(terms)])


def _mm_sel_r(a, sel):
    m = a.shape[0]
    t = jnp.dot(jnp.concatenate(_split3(a), axis=0), sel.astype(BF16), preferred_element_type=F32)
    return t[:m] + t[m:2 * m] + t[2 * m:]


def _dn_chunk(x, gb, st, ex, lmat, causal, lvl_ref, d, hm):
    n = REC_W
    lvl = lambda j: lvl_ref[d, j * CHUNK:(j + 1) * CHUNK, :]
    eye = lvl(N_LEVELS)
    ones8 = jnp.ones((8, CHUNK), F32)
    q, k, v = x[:, :n], x[:, n:2 * n], x[:, 2 * n:]
    gbx = _mm_sel_r(gb, ex)
    yield
    beta, g = gbx[:, :n], gbx[:, n:]
    cum = _mm_sel_l(lmat, g)
    last = 0 if d else CHUNK - 1
    clast = cum[last:last + 1]
    kb = k * beta
    ks = _bdiag(k.astype(BF16), hm)
    kk = lax.dot_general(kb.astype(BF16), ks, (((1,), (1,)), ((), ())), preferred_element_type=F32)
    qk = lax.dot_general(q.astype(BF16), ks, (((1,), (1,)), ((), ())), preferred_element_type=F32)
    yield
    cum_s = _mm_sel_l(ones8, cum * eye)[0:1]
    ecum = jnp.exp(cum)
    kdec = k * jnp.exp(clast - cum)
    yield
    decay = jnp.where(causal, jnp.exp(jnp.where(causal, cum - cum_s, 0.0)), 0.0)
    m = kk * decay
    attn = qk * decay
    inv = eye - lvl(N_LEVELS - 1) * m
    for lev in range(N_LEVELS - 2, -1, -1):
        half = _mm_bd(inv, lvl(lev) * m, hm)
        yield
        inv = inv - _mm_bd(half, inv, hm)
        yield
    rhs = v * beta - _mm_nt(kb * ecum, st)
    qs = _mm_nt(q * ecum, st)
    yield
    v_new = _mm_bd(inv, rhs, hm)
    yield
    out = qs + _mm_bd(attn, v_new, hm)
    st_new = st * jnp.exp(clast) + _mm_tn(v_new, kdec) * hm.astype(F32)
    return out, st_new


def _hg_chunk(qr, fr, v, lb, st, amat, lvl_ref, d, hm):
    lvl = lambda j: lvl_ref[d, j * CHUNK:(j + 1) * CHUNK, :]
    last = 0 if d else CHUNK - 1
    q = _silu(qr)
    f = lb + (1.0 - lb) * _sigmoid(fr)
    k = 1.0 - f
    g = jnp.log(f)
    ex = jnp.exp(_mm_sel_l(amat, g, terms=2))
    yield

    def level(xl, mask):
        kl = _bdiag((k * xl).astype(BF16), hm)
        return lax.dot_general((q * xl).astype(BF16), kl, (((1,), (1,)), ((), ())),
                               preferred_element_type=F32) * mask

    attn = level(1.0, lvl(N_LEVELS))
    for lev in range(N_LEVELS):
        attn = attn + level(ex[(2 + lev) * CHUNK:(3 + lev) * CHUNK], lvl(lev))
    qs = _mm_nt(q * ex[0:CHUNK], st)
    vk = _mm_tn(v, k * ex[CHUNK:2 * CHUNK])
    yield
    out = _mm_bd(attn, v, hm) + qs
    st_new = st * ex[last:last + 1] + vk * hm.astype(F32)
    return out, st_new


def _dn_scan_kernel(xf_ref, xb_ref, gf_ref, gb_ref, s0_ref, ex_ref, l_ref, cz_ref, lvl_ref, hm_ref,
                    of_ref, ob_ref, sf_ref, s_scr, *, nchunk):
    @pl.when(pl.program_id(1) == 0)
    def _():
        s_scr[...] = s0_ref[...]

    hm = hm_ref[...]
    refs = ((xf_ref, gf_ref, of_ref), (xb_ref, gb_ref, ob_ref))

    def body(c, carry):
        rows = [pl.ds(pl.multiple_of(cc * CHUNK, CHUNK), CHUNK) for cc in (c, nchunk - 1 - c)]
        insts = [(bi, d) for bi in range(s_scr.shape[0]) for d in range(2)]
        res = _lockstep([_dn_chunk(refs[d][0][bi, rows[d], :], refs[d][1][bi, rows[d], :], s_scr[bi, d],
                                   ex_ref[d], l_ref[d], cz_ref[d] > 0, lvl_ref, d, hm) for bi, d in insts])
        for (bi, d), (out, s_new) in zip(insts, res):
            refs[d][2][bi, rows[d], :] = out
            s_scr[bi, d] = s_new
        return carry

    lax.fori_loop(0, nchunk, body, 0)
    sf_ref[...] = s_scr[...]


def _hg_scan_kernel(qf_ref, qb_ref, ff_ref, fb_ref, if_ref, ib_ref, lb_ref, s0_ref, a_ref, lvl_ref, hm_ref,
                    of_ref, ob_ref, sf_ref, s_scr, *, nchunk):
    @pl.when(pl.program_id(1) == 0)
    def _():
        s_scr[...] = s0_ref[...]

    hm = hm_ref[...]
    refs = ((qf_ref, ff_ref, if_ref, of_ref), (qb_ref, fb_ref, ib_ref, ob_ref))

    def body(c, carry):
        rows = [pl.ds(pl.multiple_of(cc * CHUNK, CHUNK), CHUNK) for cc in (c, nchunk - 1 - c)]
        insts = [(bi, d) for bi in range(s_scr.shape[0]) for d in range(2)]
        res = _lockstep([_hg_chunk(refs[d][0][bi, rows[d], :], refs[d][1][bi, rows[d], :],
                                   refs[d][2][bi, rows[d], :], lb_ref[d:d + 1, :], s_scr[bi, d], a_ref[d],
                                   lvl_ref, d, hm) for bi, d in insts])
        for (bi, d), (out, s_new) in zip(insts, res):
            refs[d][3][bi, rows[d], :] = out
            s_scr[bi, d] = s_new
        return carry

    lax.fori_loop(0, nchunk, body, 0)
    sf_ref[...] = s_scr[...]


def _dn_scan(qkv, gb, s0, consts):
    b, t, w3 = qkv.shape
    tb = _tile(t, 512)
    nblk = t // tb
    n = REC_W
    nb = _tile(b, SCAN_BATCH)
    fwd = lambda bi, i: (bi, i, 0)
    bwd = lambda bi, i: (bi, nblk - 1 - i, 0)
    c3 = lambda bi, i: (0, 0, 0)
    state = pl.BlockSpec((nb, 2, n, n), lambda bi, i: (bi, 0, 0, 0))
    return pl.pallas_call(
        functools.partial(_dn_scan_kernel, nchunk=tb // CHUNK),
        grid=(b // nb, nblk),
        in_specs=[pl.BlockSpec((nb, tb, w3), fwd), pl.BlockSpec((nb, tb, w3), bwd),
                  pl.BlockSpec((nb, tb, LANES), fwd), pl.BlockSpec((nb, tb, LANES), bwd),
                  state,
                  pl.BlockSpec((2, LANES, 2 * n), c3),
                  pl.BlockSpec((2, CHUNK, CHUNK), c3),
                  pl.BlockSpec((2, CHUNK, n), c3),
                  pl.BlockSpec((2, (N_LEVELS + 1) * CHUNK, n), c3),
                  pl.BlockSpec((n, n), lambda bi, i: (0, 0))],
        out_specs=(pl.BlockSpec((nb, tb, n), fwd), pl.BlockSpec((nb, tb, n), bwd), state),
        out_shape=(jax.ShapeDtypeStruct((b, t, n), F32), jax.ShapeDtypeStruct((b, t, n), F32),
                   jax.ShapeDtypeStruct((b, 2, n, n), F32)),
        scratch_shapes=[pltpu.VMEM((nb, 2, n, n), F32)],
        compiler_params=_cparams(("parallel", "arbitrary")),
        name="deltanet_scan",
    )(qkv, qkv, gb, gb, s0, consts["ex"], consts["l"], consts["cz"], consts["lvl"], consts["hm"])


def _hg_scan(hg, lb, s0, consts):
    b, t, _ = hg.shape
    tb = _tile(t, 512)
    nblk = t // tb
    n = REC_W
    nb = _tile(b, SCAN_BATCH)
    fwd = lambda col: (lambda bi, i: (bi, i, col))
    bwd = lambda col: (lambda bi, i: (bi, nblk - 1 - i, col))
    c3 = lambda bi, i: (0, 0, 0)
    blk = lambda m: pl.BlockSpec((nb, tb, n), m)
    state = pl.BlockSpec((nb, 2, n, n), lambda bi, i: (bi, 0, 0, 0))
    return pl.pallas_call(
        functools.partial(_hg_scan_kernel, nchunk=tb // CHUNK),
        grid=(b // nb, nblk),
        in_specs=[blk(fwd(0)), blk(bwd(0)), blk(fwd(1)), blk(bwd(2)), blk(fwd(3)), blk(bwd(3)),
                  pl.BlockSpec((2, n), lambda bi, i: (0, 0)),
                  state,
                  pl.BlockSpec((2, (2 + N_LEVELS) * CHUNK, CHUNK), c3),
                  pl.BlockSpec((2, (N_LEVELS + 1) * CHUNK, n), c3),
                  pl.BlockSpec((n, n), lambda bi, i: (0, 0))],
        out_specs=(blk(fwd(0)), blk(bwd(0)), state),
        out_shape=(jax.ShapeDtypeStruct((b, t, n), F32), jax.ShapeDtypeStruct((b, t, n), F32),
                   jax.ShapeDtypeStruct((b, 2, n, n), F32)),
        scratch_shapes=[pltpu.VMEM((nb, 2, n, n), F32)],
        compiler_params=_cparams(("parallel", "arbitrary")),
        name="hgrn2_scan",
    )(hg, hg, hg, hg, hg, hg, lb, s0, consts["a"], consts["lvl"], consts["hm"])


def _scan_constants_dir(rev):
    t = np.arange(CHUNK)
    p = (CHUNK - 1 - t) if rev else t
    pt, pu = p[:, None], p[None, :]
    causal = (pu <= pt).astype(np.float32)
    head = np.arange(REC_W) // CHUNK
    hm = (head[:, None] == head[None, :]).astype(np.float32)
    rows = [causal, (pu > pt).astype(np.float32)]
    masks = []
    for lev in range(N_LEVELS):
        blk = CHUNK >> (lev + 1)
        bound = (pt // (2 * blk)) * (2 * blk) + blk - 1
        right = (pt % (2 * blk)) >= blk
        rows.append(np.where(right, (pu > bound) & (pu <= pt), (pu > pt) & (pu <= bound)).astype(np.float32))
        masks.append(((pt // (2 * blk) == pu // (2 * blk)) & right & ((pu % (2 * blk)) < blk)).astype(np.float32))
    masks.append(np.eye(CHUNK, dtype=np.float32))
    d = 1 if rev else 0
    ex = np.zeros((LANES, 2 * REC_W), np.float32)
    for h in range(DN_HEADS):
        ex[d * DN_HEADS + h, h * CHUNK:(h + 1) * CHUNK] = 1.0
        ex[2 * DN_HEADS + d * DN_HEADS + h, REC_W + h * CHUNK:REC_W + (h + 1) * CHUNK] = 1.0
    return {
        "l": causal, "cz": np.tile(causal, (1, DN_HEADS)), "hm": hm, "ex": ex,
        "a": np.concatenate(rows, axis=0),
        "lvl": np.concatenate([np.tile(mk, (1, DN_HEADS)) for mk in masks], axis=0),
    }


def _scan_constants():
    fwd, bwd = _scan_constants_dir(False), _scan_constants_dir(True)
    out = {key: jnp.asarray(np.stack([fwd[key], bwd[key]])) for key in ("l", "cz", "ex", "a", "lvl")}
    out["hm"] = jnp.asarray(fwd["hm"], BF16)
    return out


def _outproj_kernel(a_ref, dof_ref, dob_ref, z_ref, gof_ref, gob_ref, gg_ref, dnw_ref, hgw_ref, g_ref,
                    w_ref, x_ref, gate_ref, sh_ref, sc_ref, rt_ref, x1_ref, h2_ref, aff_ref):
    gmat = g_ref[...]

    def gated(o, z, nw):
        return o * lax.rsqrt(_mm_group(o * o, gmat) + EPS) * nw * _silu(z)

    dmix = gated(dof_ref[0] + dob_ref[0], z_ref[0], dnw_ref[...])
    gmix = gated(gof_ref[0] + gob_ref[0], gg_ref[0], hgw_ref[...])
    y = (jnp.dot(a_ref[0], w_ref[0:ATTN_Q_W, :], preferred_element_type=F32)
         + jnp.dot(dmix.astype(BF16), w_ref[ATTN_Q_W:ATTN_Q_W + REC_W, :], preferred_element_type=F32)
         + jnp.dot(gmix.astype(BF16), w_ref[ATTN_Q_W + REC_W:, :], preferred_element_type=F32))
    x1 = x_ref[0] + gate_ref[0] * y
    x1_ref[0] = x1
    ms = jnp.mean(x1 * x1, axis=-1, keepdims=True)
    h2 = x1 * lax.rsqrt(ms + EPS) * (1.0 + sc_ref[0]) + sh_ref[0]
    h2_ref[0] = h2.astype(BF16)
    logits = lax.dot_general(rt_ref[...], h2, (((1,), (1,)), ((), ())), precision=HIGHEST,
                             preferred_element_type=F32)
    e = jnp.exp(logits - jnp.max(logits, axis=0, keepdims=True))
    aff_ref[0] = e / jnp.sum(e, axis=0, keepdims=True)


def _output_projection(a, dof, dob, dn, gof, gob, hg, dnw, hgw, gavg, w_out, x, gate, shift, scale, router_t):
    b, t, d = x.shape
    tm = _tile(t, 512)
    n = REC_W
    per_b = gate.shape[0] > 1
    mod_map = (lambda bi, i: (bi, 0, 0)) if per_b else (lambda bi, i: (0, 0, 0))
    const = lambda bi, i: (0, 0)
    tok = lambda bi, i: (bi, i, 0)
    return pl.pallas_call(
        _outproj_kernel,
        grid=(b, t // tm),
        in_specs=[pl.BlockSpec((1, tm, ATTN_Q_W), tok),
                  pl.BlockSpec((1, tm, n), tok), pl.BlockSpec((1, tm, n), tok),
                  pl.BlockSpec((1, tm, n), lambda bi, i: (bi, i, 3)),
                  pl.BlockSpec((1, tm, n), tok), pl.BlockSpec((1, tm, n), tok),
                  pl.BlockSpec((1, tm, n), lambda bi, i: (bi, i, 4)),
                  pl.BlockSpec((1, n), const), pl.BlockSpec((1, n), const),
                  pl.BlockSpec((n, n), const),
                  pl.BlockSpec((d, d), const),
                  pl.BlockSpec((1, tm, d), tok),
                  pl.BlockSpec((1, 1, d), mod_map), pl.BlockSpec((1, 1, d), mod_map),
                  pl.BlockSpec((1, 1, d), mod_map),
                  pl.BlockSpec((N_EXPERTS, d), const)],
        out_specs=(pl.BlockSpec((1, tm, d), tok), pl.BlockSpec((1, tm, d), tok),
                   pl.BlockSpec((1, N_EXPERTS, tm), lambda bi, i: (bi, 0, i))),
        out_shape=(jax.ShapeDtypeStruct((b, t, d), F32), jax.ShapeDtypeStruct((b, t, d), BF16),
                   jax.ShapeDtypeStruct((b, N_EXPERTS, t), F32)),
        compiler_params=_cparams(("parallel", "parallel")),
        name="output_projection_router",
    )(a, dof, dob, dn, gof, gob, hg, dnw, hgw, gavg, w_out, x, gate, shift, scale, router_t)


def _select_kernel(aff_ref, u_ref, bs_ref, bst_ref, su_ref, slot_ref, base_ref, nsub_ref, inc_scr, *, cap, nblk):
    x = aff_ref[0]
    bits = pltpu.bitcast(x, jnp.int32)
    lo = jnp.zeros((N_EXPERTS, 1), jnp.int32)
    for bit in range(30, -1, -1):
        cand = lo | (1 << bit)
        cnt = jnp.sum((bits >= cand).astype(jnp.int32), axis=1, keepdims=True)
        lo = jnp.where(cnt >= cap, cand, lo)
    gt = bits > lo
    eq = bits == lo
    umat = u_ref[...]

    def prefix(mask):
        mb = mask.astype(BF16)
        for j in range(nblk):
            inc_scr[:, j * LANES:(j + 1) * LANES] = jnp.dot(mb[:, j * LANES:(j + 1) * LANES], umat,
                                                            preferred_element_type=F32)
        totals = jnp.dot(mb, bs_ref[...], preferred_element_type=F32)
        offs = _mm_hi(totals, su_ref[...])
        return inc_scr[...] + _mm_hi(offs, bst_ref[...]), offs, totals

    eqf = eq.astype(F32)
    n_gt = jnp.sum(gt.astype(F32), axis=1, keepdims=True)
    eq_before, _, _ = prefix(eqf)
    sel = gt | (eq & ((eq_before - eqf) < (cap - n_gt)))
    self_ = sel.astype(F32)
    pos, offs, totals = prefix(self_)
    slot_ref[0] = jnp.where(sel, pos - 1.0, -1.0).astype(jnp.int32)
    start = jnp.floor(offs * (1.0 / ROUTE_ALIGN)) * ROUTE_ALIGN
    span = offs + totals - start
    n_sub = functools.reduce(jnp.add, [jnp.where(span > j * ROUTE_SUB, 1.0, 0.0)
                                       for j in range(ROUTE_WIN // ROUTE_SUB)])
    base_ref[0] = start.astype(jnp.int32)
    nsub_ref[0] = jnp.max(n_sub, axis=0, keepdims=True).astype(jnp.int32)


def _moe_select(aff_t, cap):
    b, e, t = aff_t.shape
    nblk = t // ROUTE_BLK
    u = jnp.asarray(np.triu(np.ones((LANES, LANES), np.float32)), BF16)
    blk = np.arange(t) // ROUTE_BLK
    bs = (blk[:, None] == np.arange(nblk)[None, :]).astype(np.float32)
    su = np.triu(np.ones((nblk, nblk), np.float32), 1)
    const = lambda bi: (0, 0)
    return pl.pallas_call(
        functools.partial(_select_kernel, cap=cap, nblk=nblk),
        grid=(b,),
        in_specs=[pl.BlockSpec((1, e, t), lambda bi: (bi, 0, 0)),
                  pl.BlockSpec((LANES, LANES), const),
                  pl.BlockSpec((t, nblk), const),
                  pl.BlockSpec((nblk, t), const),
                  pl.BlockSpec((nblk, nblk), const)],
        out_specs=(pl.BlockSpec((1, e, t), lambda bi: (bi, 0, 0)),
                   pl.BlockSpec((1, e, nblk), lambda bi: (bi, 0, 0)),
                   pl.BlockSpec((1, 1, nblk), lambda bi: (bi, 0, 0))),
        out_shape=(jax.ShapeDtypeStruct((b, e, t), jnp.int32), jax.ShapeDtypeStruct((b, e, nblk), jnp.int32),
                   jax.ShapeDtypeStruct((b, 1, nblk), jnp.int32)),
        scratch_shapes=[pltpu.VMEM((e, t), F32)],
        compiler_params=_cparams(("parallel",)),
        name="moe_select",
    )(aff_t, u, jnp.asarray(bs, BF16), jnp.asarray(bs.T), jnp.asarray(su))


def _gather_kernel(start_sm, nsub_sm, slot_ref, h_ref, xe_ref, *, nblk):
    bi = pl.program_id(0)
    xe_ref[0] = jnp.zeros(xe_ref.shape[1:], BF16)
    row = lax.broadcasted_iota(jnp.int32, (ROUTE_SUB, ROUTE_BLK), 0)

    per_iter = _tile(nblk, GATHER_BLOCKS)

    def body(kk, carry):
        blocks, n_sub = [], 0
        for c in range(per_iter):
            k = kk * per_iter + c
            t0 = pl.multiple_of(k * ROUTE_BLK, ROUTE_BLK)
            firsts = [start_sm[(bi * N_EXPERTS + e) * nblk + k] for e in range(N_EXPERTS)]
            blocks.append((h_ref[0, pl.ds(t0, ROUTE_BLK), :], slot_ref[0, :, pl.ds(t0, ROUTE_BLK)], firsts))
            n_sub = jnp.maximum(n_sub, nsub_sm[bi * nblk + k])

        def part(j, hblk, slots, firsts):
            starts = [pl.multiple_of(first + j * ROUTE_SUB, ROUTE_ALIGN) for first in firsts]
            onehot = jnp.concatenate(
                [jnp.where(row == slots[e:e + 1, :] - starts[e], 1.0, 0.0).astype(BF16)
                 for e in range(N_EXPERTS)], axis=0)
            yield
            return starts, jnp.dot(onehot, hblk, preferred_element_type=F32).astype(BF16)

        def sub(j, inner):
            for starts, rows in _lockstep([part(j, *blk) for blk in blocks]):
                for e in range(N_EXPERTS):
                    win = pl.ds(starts[e], ROUTE_SUB)
                    xe_ref[0, e, win, :] = xe_ref[0, e, win, :] + rows[e * ROUTE_SUB:(e + 1) * ROUTE_SUB]
            return inner

        lax.fori_loop(0, n_sub, sub, 0)
        return carry

    lax.fori_loop(0, nblk // per_iter, body, 0)


def _moe_gather(h2, slot, start_flat, nsub_flat, cap):
    b, t, d = h2.shape
    nblk = t // ROUTE_BLK
    cp = cap + ROUTE_WIN
    dq = _tile(d, 256)
    grid_spec = pltpu.PrefetchScalarGridSpec(
        num_scalar_prefetch=2,
        grid=(b, d // dq),
        in_specs=[pl.BlockSpec((1, N_EXPERTS, t), lambda bi, j, s0, s1: (bi, 0, 0)),
                  pl.BlockSpec((1, t, dq), lambda bi, j, s0, s1: (bi, 0, j))],
        out_specs=pl.BlockSpec((1, N_EXPERTS, cp, dq), lambda bi, j, s0, s1: (bi, 0, 0, j)),
    )
    return pl.pallas_call(
        functools.partial(_gather_kernel, nblk=nblk),
        grid_spec=grid_spec,
        out_shape=jax.ShapeDtypeStruct((b, N_EXPERTS, cp, d), BF16),
        compiler_params=_cparams(("parallel", "parallel")),
        name="moe_gather",
    )(start_flat, nsub_flat, slot, h2)


def _ffn_kernel(x_ref, wg_ref, wu_ref, wd_ref, y_ref, *, cap):
    x = x_ref[0, 0, 0:cap, :]
    a = jnp.dot(x, wg_ref[0], preferred_element_type=F32)
    u = jnp.dot(x, wu_ref[0], preferred_element_type=F32)
    y_ref[0, 0, 0:cap, :] = jnp.dot((_silu(a) * u).astype(BF16), wd_ref[0],
                                    preferred_element_type=F32).astype(BF16)
    y_ref[0, 0, cap:, :] = jnp.zeros((y_ref.shape[2] - cap, y_ref.shape[3]), BF16)


def _moe_ffn(xe, wg, wu, wd, cap):
    b, e, cp, d = xe.shape
    f = wg.shape[-1]
    return pl.pallas_call(
        functools.partial(_ffn_kernel, cap=cap),
        grid=(e, b),
        in_specs=[pl.BlockSpec((1, 1, cp, d), lambda ei, bi: (bi, ei, 0, 0)),
                  pl.BlockSpec((1, d, f), lambda ei, bi: (ei, 0, 0)),
                  pl.BlockSpec((1, d, f), lambda ei, bi: (ei, 0, 0)),
                  pl.BlockSpec((1, f, d), lambda ei, bi: (ei, 0, 0))],
        out_specs=pl.BlockSpec((1, 1, cp, d), lambda ei, bi: (bi, ei, 0, 0)),
        out_shape=jax.ShapeDtypeStruct((b, e, cp, d), BF16),
        compiler_params=_cparams(("parallel", "parallel")),
        name="moe_ffn",
    )(xe, wg, wu, wd)


def _combine_kernel(start_sm, nsub_sm, ye_ref, x_ref, gate_ref, slot_ref, aff_ref, o_ref, *, nblk, per_step):
    bi = pl.program_id(0)
    row = lax.broadcasted_iota(jnp.int32, (ROUTE_SUB, ROUTE_BLK), 0)
    blocks, n_sub = [], 0
    for c in range(per_step):
        k = pl.program_id(2) * per_step + c
        tok = slice(c * ROUTE_BLK, (c + 1) * ROUTE_BLK)
        firsts = [start_sm[(bi * N_EXPERTS + e) * nblk + k] for e in range(N_EXPERTS)]
        blocks.append((tok, slot_ref[0, :, tok], aff_ref[0, :, tok], firsts))
        n_sub = jnp.maximum(n_sub, nsub_sm[bi * nblk + k])

    def part(j, slots, aff, firsts):
        starts = [pl.multiple_of(first + j * ROUTE_SUB, ROUTE_ALIGN) for first in firsts]
        wsel = jnp.concatenate([jnp.where(row == slots[e:e + 1, :] - starts[e], aff[e:e + 1, :], 0.0)
                                for e in range(N_EXPERTS)], axis=0)
        ys = jnp.concatenate([ye_ref[0, e, pl.ds(starts[e], ROUTE_SUB), :] for e in range(N_EXPERTS)], axis=0)
        yield
        return lax.dot_general(wsel.astype(BF16), ys, (((0,), (0,)), ((), ())), preferred_element_type=F32)

    def sub(j, accs):
        adds = _lockstep([part(j, slots, aff, firsts) for _, slots, aff, firsts in blocks])
        return tuple(acc + add for acc, add in zip(accs, adds))

    zero = jnp.zeros((ROUTE_BLK, x_ref.shape[2]), F32)
    accs = lax.fori_loop(0, n_sub, sub, (zero,) * per_step)
    for (tok, _, _, _), acc in zip(blocks, accs):
        o_ref[0, tok, :] = x_ref[0, tok, :] + gate_ref[0] * acc


def _moe_combine(ye, x1, gate, slot, aff_t, start_flat, nsub_flat):
    b, t, d = x1.shape
    cp = ye.shape[2]
    nblk = t // ROUTE_BLK
    dh = d // 2
    per_b = gate.shape[0] > 1
    per_step = _tile(nblk, COMBINE_BLOCKS)
    tok = per_step * ROUTE_BLK
    grid_spec = pltpu.PrefetchScalarGridSpec(
        num_scalar_prefetch=2,
        grid=(b, 2, nblk // per_step),
        in_specs=[pl.BlockSpec((1, N_EXPERTS, cp, dh), lambda bi, j, k, s0, s1: (bi, 0, 0, j)),
                  pl.BlockSpec((1, tok, dh), lambda bi, j, k, s0, s1: (bi, k, j)),
                  pl.BlockSpec((1, 1, dh), (lambda bi, j, k, s0, s1: (bi, 0, j)) if per_b
                               else (lambda bi, j, k, s0, s1: (0, 0, j))),
                  pl.BlockSpec((1, N_EXPERTS, tok), lambda bi, j, k, s0, s1: (bi, 0, k)),
                  pl.BlockSpec((1, N_EXPERTS, tok), lambda bi, j, k, s0, s1: (bi, 0, k))],
        out_specs=pl.BlockSpec((1, tok, dh), lambda bi, j, k, s0, s1: (bi, k, j)),
    )
    return pl.pallas_call(
        functools.partial(_combine_kernel, nblk=nblk, per_step=per_step),
        grid_spec=grid_spec,
        out_shape=jax.ShapeDtypeStruct((b, t, d), F32),
        compiler_params=_cparams(("parallel", "parallel", "arbitrary")),
        name="moe_combine",
    )(start_flat, nsub_flat, ye, x1, gate, slot, aff_t)


def _expert_choice_ffn(x1, h2, aff_t, gate, wg, wu, wd):
    b, t, _ = x1.shape
    cap = CAPACITY_FACTOR * t // N_EXPERTS
    slot, start, nsub = _moe_select(aff_t, cap)
    start_flat, nsub_flat = start.reshape(-1), nsub.reshape(-1)
    xe = _moe_gather(h2, slot, start_flat, nsub_flat, cap)
    ye = _moe_ffn(xe, wg, wu, wd, cap)
    return _moe_combine(ye, x1, gate, slot, aff_t, start_flat, nsub_flat)


def _rope_tables(n_tokens):
    rows = n_tokens // GRID_W
    row = jnp.repeat(jnp.arange(rows, dtype=F32), GRID_W)
    col = jnp.tile(jnp.arange(GRID_W, dtype=F32), rows)
    n_freq = HEAD_DIM // 4
    inv_freq = ROPE_THETA ** (-jnp.arange(n_freq, dtype=F32) / n_freq)
    ang_r = row[:, None] * inv_freq
    ang_c = col[:, None] * inv_freq
    cos = jnp.concatenate([jnp.cos(ang_r)] * 2 + [jnp.cos(ang_c)] * 2, axis=-1)
    sin = jnp.concatenate([jnp.sin(ang_r)] * 2 + [jnp.sin(ang_c)] * 2, axis=-1)
    return jnp.tile(cos, (1, LANES // HEAD_DIM)), jnp.tile(sin, (1, LANES // HEAD_DIM))


def _block_diag(width, block, value):
    idx = np.arange(width) // block
    return jnp.asarray((idx[:, None] == idx[None, :]).astype(np.float32) * value)


def kernel(x, c, ctx, c_ctx, w_mod, b_mod, w_in, w_out, attn_q_norm, attn_k_norm, dn_conv, dn_a_log,
           dn_dt_bias, dn_norm, hg_lower_bounds, hg_norm, moe_router, moe_w_gate, moe_w_up, moe_w_down):
    depth = w_mod.shape[0]
    b, t_lat, d = x.shape
    cos, sin = _rope_tables(t_lat)
    g_head = _block_diag(LANES, HEAD_DIM, 1.0 / HEAD_DIM)
    g_mean = _block_diag(REC_W, HEAD_DIM, 1.0 / HEAD_DIM)
    g_sum = _block_diag(REC_W, HEAD_DIM, 1.0)
    consts = _scan_constants()
    s_zero = jnp.zeros((b, 2, REC_W, REC_W), F32)

    lb_w = jax.nn.softmax(hg_lower_bounds.astype(F32), axis=0)
    hg_lb = jnp.cumsum(lb_w, axis=0) - lb_w[0]

    rows = ((b + 1 + 7) // 8) * 8
    cond = jnp.zeros((rows, d), F32).at[:b].set(c).at[b].set(c_ctx)
    mod = _modulation(cond, w_mod, b_mod)

    n_small = 4 * DN_HEADS
    w_in_r = jnp.concatenate(
        [w_in[:, :, :C_HG], w_in[:, :, C_HG + n_small:], w_in[:, :, C_HG:C_HG + n_small],
         jnp.zeros((depth, d, C_END - C_BA - n_small), w_in.dtype)], axis=-1).astype(BF16)
    w_out_b = _cast_bf16(w_out)
    wg_b, wu_b, wd_b = _cast_bf16(moe_w_gate), _cast_bf16(moe_w_up), _cast_bf16(moe_w_down)

    x_lat, x_ctx = x, ctx
    for l in range(depth):
        ctx_out = l < depth - 1
        m_lat = [mod[l, :b, j * d:(j + 1) * d][:, None, :] for j in range(6)]
        m_ctx = [mod[l, b:b + 1, j * d:(j + 1) * d][:, None, :] for j in range(6)]
        qn = jnp.tile(attn_q_norm[l], LANES // HEAD_DIM)[None, :]
        kn = jnp.tile(attn_k_norm[l], LANES // HEAD_DIM)[None, :]
        conv_w = jnp.zeros((8, 3 * REC_W), F32).at[:CONV_K].set(dn_conv[l])
        pad = jnp.zeros((LANES - 4 * DN_HEADS,), F32)
        neg_a = jnp.concatenate([jnp.zeros((2 * DN_HEADS,), F32), -jnp.exp(dn_a_log[l].reshape(-1)), pad])[None, :]
        dt_b = jnp.concatenate([jnp.zeros((2 * DN_HEADS,), F32), dn_dt_bias[l].reshape(-1), pad])[None, :]
        dnw = jnp.tile(dn_norm[l], DN_HEADS)[None, :]
        hgw = jnp.tile(hg_norm[l], HG_HEADS)[None, :]
        router_t = moe_router[l].T

        streams = {}
        for name, xs, ms, rotate in (("ctx", x_ctx, m_ctx, False), ("lat", x_lat, m_lat, True)):
            t = xs.shape[1]
            q, kt, v, dn, hg, ba = _input_projection(xs, ms[0], ms[1], w_in_r[l], qn, kn,
                                                     cos[:t], sin[:t], g_head, rotate)
            qkv, gb = _dn_prep(dn, ba, conv_w, neg_a, dt_b, g_sum)
            streams[name] = dict(q=q, kv=(kt, v), dn=dn, hg=hg, qkv=qkv, gb=gb)

        sc, sl = streams["ctx"], streams["lat"]
        dcf, dcb, dn_state = _dn_scan(sc["qkv"], sc["gb"], s_zero, consts)
        dlf, dlb, _ = _dn_scan(sl["qkv"], sl["gb"], dn_state, consts)
        gcf, gcb, hg_state = _hg_scan(sc["hg"], hg_lb[l], s_zero, consts)
        glf, glb, _ = _hg_scan(sl["hg"], hg_lb[l], hg_state, consts)

        a_lat = _attention(sl["q"], [sc["kv"], sl["kv"]], t_lat)
        x1, h2, aff_t = _output_projection(
            a_lat, dlf, dlb, sl["dn"], glf, glb, sl["hg"], dnw, hgw, g_mean, w_out_b[l], x_lat,
            m_lat[2], m_lat[3], m_lat[4], router_t)
        x_lat = _expert_choice_ffn(x1, h2, aff_t, m_lat[5], wg_b[l], wu_b[l], wd_b[l])
        if ctx_out:
            a_ctx = _attention(sc["q"], [sc["kv"]], x_ctx.shape[1])
            x1, h2, aff_t = _output_projection(
                a_ctx, dcf, dcb, sc["dn"], gcf, gcb, sc["hg"], dnw, hgw, g_mean, w_out_b[l], x_ctx,
                m_ctx[2], m_ctx[3], m_ctx[4], router_t)
            x_ctx = _expert_choice_ffn(x1, h2, aff_t, m_ctx[5], wg_b[l], wu_b[l], wd_b[l])
    return x_lat
```
